```python
import math
import jax
import jax.numpy as jnp
from jax import lax
import numpy as np

D_MODEL = 4096
BATCH = 2
SEQ = 8192
DEPTH = 2

CHUNK = 64
LRU_WIDTH = D_MODEL // 2
LRU_BLOCKS = 16
LRU_BLOCK = LRU_WIDTH // LRU_BLOCKS
CONV_WIDTH = 4
LRU_C = 8.0
HEAD_DIM = 128
ATT_WIDTH = D_MODEL // 4
ATT_HEADS = ATT_WIDTH // HEAD_DIM
LEFT_CHUNKS = 8
BAND_CHUNKS = LEFT_CHUNKS + 1
BAND = BAND_CHUNKS * CHUNK
REL_CLIP = 128
SSM_WIDTH = D_MODEL // 4
SSM_GROUP = 16
SSM_GROUPS = SSM_WIDTH // SSM_GROUP
SSM_STATE = 64
N_BRANCHES = 3
SPLIT_POINTS = (LRU_WIDTH, LRU_WIDTH + ATT_WIDTH, LRU_WIDTH + 2 * ATT_WIDTH, LRU_WIDTH + 3 * ATT_WIDTH, LRU_WIDTH + 3 * ATT_WIDTH + SSM_WIDTH)
IN_WIDTH = LRU_WIDTH + 3 * ATT_WIDTH + SSM_WIDTH + N_BRANCHES * D_MODEL
MOE_GROUPS = 4
EXPERTS_PER_GROUP = 8
N_EXPERTS = MOE_GROUPS * EXPERTS_PER_GROUP
TOP_K = 2
EXPERT_FF = D_MODEL // 8
PLE_DIM = 256
EPS = 1e-6
NEG_INF = -1e30

kernel_name = 'hybrid_rglru_chunkattn_s5_hmoe_block'


def _rms32(x, gain):
    x32 = x.astype(jnp.float32)
    y = x32 * lax.rsqrt(jnp.mean(x32 * x32, axis=-1, keepdims=True) + EPS)
    return y * gain.astype(jnp.float32)


def rms_norm(x, gain):
    return _rms32(x, gain).astype(x.dtype)


def rglru_branch(u, conv_w, conv_b, w_rg, b_rg, w_ig, b_ig, lam):
    bsz, L, _ = u.shape
    up = jnp.pad(u, ((0, 0), (CONV_WIDTH - 1, 0), (0, 0)))
    xc = conv_b
    for j in range(CONV_WIDTH):
        xc = xc + conv_w[j] * up[:, j:j + L]
    xb = xc.reshape(bsz, L, LRU_BLOCKS, LRU_BLOCK)
    r = jax.nn.sigmoid((jnp.einsum('blhi,hij->blhj', xb, w_rg) + b_rg).astype(jnp.float32)).reshape(bsz, L, LRU_WIDTH)
    ig = jax.nn.sigmoid((jnp.einsum('blhi,hij->blhj', xb, w_ig) + b_ig).astype(jnp.float32)).reshape(bsz, L, LRU_WIDTH)
    log_a = -LRU_C * r * jax.nn.softplus(-lam.astype(jnp.float32))
    a = jnp.exp(log_a)
    xin = jnp.sqrt(-jnp.expm1(2.0 * log_a)) * (ig * xc.astype(jnp.float32))

    def step(h, inp):
        a_t, x_t = inp
        h = a_t * h + x_t
        return h, h

    h0 = jnp.zeros((bsz, LRU_WIDTH), jnp.float32)
    _, hs = lax.scan(step, h0, (jnp.swapaxes(a, 0, 1), jnp.swapaxes(xin, 0, 1)))
    return jnp.swapaxes(hs, 0, 1).astype(u.dtype)


def _band(t, n_chunks):
    bsz = t.shape[0]
    tc = t.reshape(bsz, n_chunks, CHUNK, ATT_HEADS, HEAD_DIM)
    tp = jnp.pad(tc, ((0, 0), (LEFT_CHUNKS, 0), (0, 0), (0, 0), (0, 0)))
    return jnp.concatenate([tp[:, j:j + n_chunks] for j in range(BAND_CHUNKS)], axis=2)


def chunk_band_attention(q, k, v, q_gain, k_gain, rel_bias):
    bsz, L = q.shape[:2]
    n_chunks = L // CHUNK
    qn = _rms32(q, q_gain) * (HEAD_DIM ** -0.5)
    kn = _rms32(k, k_gain)
    qc = qn.reshape(bsz, n_chunks, CHUNK, ATT_HEADS, HEAD_DIM)
    kb = _band(kn, n_chunks)
    vb = _band(v, n_chunks)
    s = jnp.einsum('bcqhd,bckhd->bchqk', qc, kb)
    iq = jnp.arange(CHUNK)
    jk = jnp.arange(BAND)
    dist = LEFT_CHUNKS * CHUNK + iq[:, None] - jk[None, :]
    bias = rel_bias.astype(jnp.float32)[:, jnp.clip(dist, -REL_CLIP, REL_CLIP) + REL_CLIP]
    key_chunk = jnp.arange(n_chunks)[:, None] - LEFT_CHUNKS + (jk // CHUNK)[None, :]
    valid = key_chunk >= 0
    s = jnp.where(valid[None, :, None, None, :], s + bias[None, None], NEG_INF)
    pr = jax.nn.softmax(s, axis=-1).astype(v.dtype)
    o = jnp.einsum('bchqk,bckhd->bcqhd', pr, vb)
    return o.reshape(bsz, L, ATT_WIDTH)


def s5_branch(u, a_re, a_im, log_dt, b_re, b_im, c_re, c_im, d_skip, w_glu, b_glu):
    bsz, L, _ = u.shape
    u32 = u.astype(jnp.float32).reshape(bsz, L, SSM_GROUPS, SSM_GROUP)
    dt = jnp.exp(log_dt.astype(jnp.float32))[:, None]
    ar = a_re.astype(jnp.float32)
    ai = a_im.astype(jnp.float32)
    mag = jnp.exp(dt * ar)
    abar_re = mag * jnp.cos(dt * ai)
    abar_im = mag * jnp.sin(dt * ai)
    den = ar * ar + ai * ai
    nr = abar_re - 1.0
    ni = abar_im
    coef_re = (nr * ar + ni * ai) / den
    coef_im = (ni * ar - nr * ai) / den
    br = b_re.astype(jnp.float32)
    bi = b_im.astype(jnp.float32)
    bbar_re = coef_re[..., None] * br - coef_im[..., None] * bi
    bbar_im = coef_re[..., None] * bi + coef_im[..., None] * br
    bu_re = jnp.einsum('blgc,gpc->blgp', u32, bbar_re)
    bu_im = jnp.einsum('blgc,gpc->blgp', u32, bbar_im)
    at_re = jnp.broadcast_to(abar_re, bu_re.shape)
    at_im = jnp.broadcast_to(abar_im, bu_re.shape)

    def combine(e1, e2):
        a1r, a1i, b1r, b1i = e1
        a2r, a2i, b2r, b2i = e2
        return (a2r * a1r - a2i * a1i,
                a2r * a1i + a2i * a1r,
                a2r * b1r - a2i * b1i + b2r,
                a2r * b1i + a2i * b1r + b2i)

    _, _, h_re, h_im = lax.associative_scan(combine, (at_re, at_im, bu_re, bu_im), axis=1)
    y = jnp.einsum('blgp,gcp->blgc', h_re, c_re.astype(jnp.float32)) - jnp.einsum('blgp,gcp->blgc', h_im, c_im.astype(jnp.float32))
    y = (y + d_skip.astype(jnp.float32).reshape(SSM_GROUPS, SSM_GROUP) * u32).reshape(bsz, L, SSM_WIDTH)
    z = jax.nn.gelu(y).astype(u.dtype) @ w_glu + b_glu
    z_val, z_gate = jnp.split(z, 2, axis=-1)
    return z_val * jax.nn.sigmoid(z_gate)


def hier_moe(xn, w_gr, b_gr, w_er, b_er, w_up, w_gate, w_down):
    gl = (xn @ w_gr).astype(jnp.float32) + b_gr.astype(jnp.float32)
    gsel = jnp.argmax(gl, axis=-1)
    gprob = jnp.take_along_axis(jax.nn.softmax(gl, axis=-1), gsel[..., None], axis=-1)[..., 0]
    el = jnp.einsum('bld,gde->blge', xn, w_er).astype(jnp.float32) + b_er.astype(jnp.float32)
    el = jnp.take_along_axis(el, gsel[..., None, None], axis=2)[:, :, 0]
    top_v, top_i = lax.top_k(el, TOP_K)
    w = jax.nn.softmax(top_v, axis=-1) * gprob[..., None]
    eid = gsel[..., None] * EXPERTS_PER_GROUP + top_i
    gate = jnp.einsum('blk,blke->ble', w, jax.nn.one_hot(eid, N_EXPERTS, dtype=jnp.float32)).astype(xn.dtype)
    out = jnp.zeros_like(xn)
    for g in range(MOE_GROUPS):
        sl = slice(g * EXPERTS_PER_GROUP, (g + 1) * EXPERTS_PER_GROUP)
        hid = jax.nn.silu(jnp.einsum('bld,edf->blef', xn, w_gate[sl])) * jnp.einsum('bld,edf->blef', xn, w_up[sl])
        out = out + jnp.einsum('blef,efd->bld', hid * gate[:, :, sl, None], w_down[sl])
    return out


def setup_inputs(seed: int = 0) -> dict:
    key = jax.random.key(seed)
    ks = iter(jax.random.split(key, 48))
    f32 = jnp.float32

    def nrm(shape, scale):
        return jax.random.normal(next(ks), shape, f32) * scale

    def gain(shape):
        return 1.0 + nrm(shape, 0.02)

    lam_u = jax.random.uniform(next(ks), (DEPTH, LRU_WIDTH), f32, 0.9, 0.999)
    lam_s = lam_u ** (1.0 / LRU_C)
    lru_lambda = jnp.log(lam_s) - jnp.log1p(-lam_s)
    n_idx = jnp.arange(SSM_STATE, dtype=f32)
    ssm_a_im = jnp.broadcast_to(math.pi * n_idx, (DEPTH, SSM_GROUPS, SSM_STATE)) + nrm((DEPTH, SSM_GROUPS, SSM_STATE), 0.01)
    ssm_log_dt = jax.random.uniform(next(ks), (DEPTH, SSM_GROUPS), f32, math.log(1e-3), math.log(1e-1))
    return {
        'x': nrm((BATCH, SEQ, D_MODEL), 1.0),
        'p': nrm((DEPTH, BATCH, SEQ, PLE_DIM), 1.0),
        'mix_gain': gain((DEPTH, D_MODEL)),
        'w_in': nrm((DEPTH, D_MODEL, IN_WIDTH), D_MODEL ** -0.5),
        'conv_w': nrm((DEPTH, CONV_WIDTH, LRU_WIDTH), 0.5),
        'conv_b': nrm((DEPTH, LRU_WIDTH), 0.01),
        'w_rgate': nrm((DEPTH, LRU_BLOCKS, LRU_BLOCK, LRU_BLOCK), LRU_BLOCK ** -0.5),
        'b_rgate': nrm((DEPTH, LRU_BLOCKS, LRU_BLOCK), 0.01),
        'w_igate': nrm((DEPTH, LRU_BLOCKS, LRU_BLOCK, LRU_BLOCK), LRU_BLOCK ** -0.5),
        'b_igate': nrm((DEPTH, LRU_BLOCKS, LRU_BLOCK), 0.01),
        'lru_lambda': lru_lambda,
        'q_gain': gain((DEPTH, HEAD_DIM)),
        'k_gain': gain((DEPTH, HEAD_DIM)),
        'rel_bias': nrm((DEPTH, ATT_HEADS, 2 * REL_CLIP + 1), 0.1),
        'ssm_a_re': -0.5 + nrm((DEPTH, SSM_GROUPS, SSM_STATE), 0.01),
        'ssm_a_im': ssm_a_im,
        'ssm_log_dt': ssm_log_dt,
        'ssm_b_re': nrm((DEPTH, SSM_GROUPS, SSM_STATE, SSM_GROUP), (2 * SSM_GROUP) ** -0.5),
        'ssm_b_im': nrm((DEPTH, SSM_GROUPS, SSM_STATE, SSM_GROUP), (2 * SSM_GROUP) ** -0.5),
        'ssm_c_re': nrm((DEPTH, SSM_GROUPS, SSM_GROUP, SSM_STATE), SSM_STATE ** -0.5),
        'ssm_c_im': nrm((DEPTH, SSM_GROUPS, SSM_GROUP, SSM_STATE), SSM_STATE ** -0.5),
        'ssm_d': nrm((DEPTH, SSM_WIDTH), 1.0),
        'w_glu': nrm((DEPTH, SSM_WIDTH, 2 * SSM_WIDTH), SSM_WIDTH ** -0.5),
        'b_glu': nrm((DEPTH, 2 * SSM_WIDTH), 0.01),
        'w_proj_lru': nrm((DEPTH, LRU_WIDTH, D_MODEL), LRU_WIDTH ** -0.5),
        'w_proj_att': nrm((DEPTH, ATT_WIDTH, D_MODEL), ATT_WIDTH ** -0.5),
        'w_proj_ssm': nrm((DEPTH, SSM_WIDTH, D_MODEL), SSM_WIDTH ** -0.5),
        'w_out': nrm((DEPTH, D_MODEL, D_MODEL), D_MODEL ** -0.5),
        'ffn_gain': gain((DEPTH, D_MODEL)),
        'w_group_router': nrm((DEPTH, D_MODEL, MOE_GROUPS), D_MODEL ** -0.5),
        'b_group_router': nrm((DEPTH, MOE_GROUPS), 0.01),
        'w_expert_router': nrm((DEPTH, MOE_GROUPS, D_MODEL, EXPERTS_PER_GROUP), D_MODEL ** -0.5),
        'b_expert_router': nrm((DEPTH, MOE_GROUPS, EXPERTS_PER_GROUP), 0.01),
        'w_up': nrm((DEPTH, N_EXPERTS, D_MODEL, EXPERT_FF), D_MODEL ** -0.5),
        'w_gate': nrm((DEPTH, N_EXPERTS, D_MODEL, EXPERT_FF), D_MODEL ** -0.5),
        'w_down': nrm((DEPTH, N_EXPERTS, EXPERT_FF, D_MODEL), EXPERT_FF ** -0.5),
        'ple_gain': gain((DEPTH, D_MODEL)),
        'w_ple': nrm((DEPTH, PLE_DIM, D_MODEL), PLE_DIM ** -0.5),
        'w_ple_gate': nrm((DEPTH, D_MODEL, D_MODEL), D_MODEL ** -0.5),
    }


def reference(x, p, mix_gain, w_in, conv_w, conv_b, w_rgate, b_rgate, w_igate, b_igate, lru_lambda,
              q_gain, k_gain, rel_bias, ssm_a_re, ssm_a_im, ssm_log_dt, ssm_b_re, ssm_b_im, ssm_c_re,
              ssm_c_im, ssm_d, w_glu, b_glu, w_proj_lru, w_proj_att, w_proj_ssm, w_out, ffn_gain,
              w_group_router, b_group_router, w_expert_router, b_expert_router, w_up, w_gate, w_down,
              ple_gain, w_ple, w_ple_gate):
    bsz, L, _ = x.shape
    h = x
    for i in range(DEPTH):
        xn = rms_norm(h, mix_gain[i])
        proj = xn @ w_in[i]
        u_lru, q, k, v, u_ssm, gates = jnp.split(proj, SPLIT_POINTS, axis=-1)
        y_lru = rglru_branch(u_lru, conv_w[i], conv_b[i], w_rgate[i], b_rgate[i], w_igate[i], b_igate[i], lru_lambda[i])
        hs = (bsz, L, ATT_HEADS, HEAD_DIM)
        y_att = chunk_band_attention(q.reshape(hs), k.reshape(hs), v.reshape(hs), q_gain[i], k_gain[i], rel_bias[i])
        y_ssm = s5_branch(u_ssm, ssm_a_re[i], ssm_a_im[i], ssm_log_dt[i], ssm_b_re[i], ssm_b_im[i],
                          ssm_c_re[i], ssm_c_im[i], ssm_d[i], w_glu[i], b_glu[i])
        g = jax.nn.sigmoid(gates.astype(jnp.float32)).astype(h.dtype).reshape(bsz, L, N_BRANCHES, D_MODEL)
        merged = (g[:, :, 0] * (y_lru @ w_proj_lru[i])
                  + g[:, :, 1] * (y_att @ w_proj_att[i])
                  + g[:, :, 2] * (y_ssm @ w_proj_ssm[i]))
        h = h + merged @ w_out[i]
        hn = rms_norm(h, ffn_gain[i])
        h = h + hier_moe(hn, w_group_router[i], b_group_router[i], w_expert_router[i], b_expert_router[i],
                         w_up[i], w_gate[i], w_down[i])
        e = p[i] @ w_ple[i]
        pg = jax.nn.sigmoid((rms_norm(h, ple_gain[i]) @ w_ple_gate[i]).astype(jnp.float32)).astype(h.dtype)
        h = h + pg * e
    return h
```

```python
import functools
import math

import jax
import jax.numpy as jnp
from jax import lax
from jax.experimental import pallas as pl
from jax.experimental.pallas import tpu as pltpu

F32 = jnp.float32
BF16 = jnp.bfloat16

CHUNK = 64
LEFT_CHUNKS = 8
REL_CLIP = 128
HEAD_DIM = 128
LRU_C = 8.0
TOP_K = 2
EPS = 1e-6
NEG_INF = -1e30
N_BRANCHES = 3

S5_CHUNK = 16
ATTN_QBLOCK = LEFT_CHUNKS * CHUNK
ROUTER_LANES = 128
V7X_VMEM_LIMIT_BYTES = 56 * 1024 * 1024


def _params(*semantics):
    return pltpu.CompilerParams(dimension_semantics=semantics, vmem_limit_bytes=V7X_VMEM_LIMIT_BYTES)


def _dot(a, b):
    return jnp.dot(a, b, preferred_element_type=F32)


def _rmsnorm_kernel(x_ref, g_ref, o_ref):
    x = x_ref[...].astype(F32)
    ms = jnp.mean(x * x, axis=-1, keepdims=True)
    o_ref[...] = (x * lax.rsqrt(ms + EPS) * g_ref[...]).astype(o_ref.dtype)


def _rmsnorm(x, gain, *, tm=512):
    m, d = x.shape
    return pl.pallas_call(
        _rmsnorm_kernel,
        grid=(m // tm,),
        in_specs=[pl.BlockSpec((tm, d), lambda i: (i, 0)), pl.BlockSpec((1, d), lambda i: (0, 0))],
        out_specs=pl.BlockSpec((tm, d), lambda i: (i, 0)),
        out_shape=jax.ShapeDtypeStruct((m, d), BF16),
        compiler_params=_params("parallel"),
        name="rmsnorm",
    )(x, gain.reshape(1, d).astype(F32))


def _mm_kernel(x_ref, w_ref, o_ref):
    o_ref[...] = _dot(x_ref[...], w_ref[...]).astype(o_ref.dtype)


def _matmul(x, w, *, tm=1024, tn=1024, out_dtype=BF16):
    m, k = x.shape
    n = w.shape[1]
    return pl.pallas_call(
        _mm_kernel,
        grid=(m // tm, n // tn),
        in_specs=[pl.BlockSpec((tm, k), lambda i, j: (i, 0)), pl.BlockSpec((k, tn), lambda i, j: (0, j))],
        out_specs=pl.BlockSpec((tm, tn), lambda i, j: (i, j)),
        out_shape=jax.ShapeDtypeStruct((m, n), out_dtype),
        compiler_params=_params("parallel", "arbitrary"),
        name="matmul",
    )(x, w)


def _mm_residual_kernel(h_ref, x_ref, w_ref, o_ref):
    o_ref[...] = h_ref[...] + _dot(x_ref[...], w_ref[...])


def _matmul_residual(h, x, w, *, tm=1024, tn=512):
    m, k = x.shape
    n = w.shape[1]
    return pl.pallas_call(
        _mm_residual_kernel,
        grid=(m // tm, n // tn),
        in_specs=[
            pl.BlockSpec((tm, tn), lambda i, j: (i, j)),
            pl.BlockSpec((tm, k), lambda i, j: (i, 0)),
            pl.BlockSpec((k, tn), lambda i, j: (0, j)),
        ],
        out_specs=pl.BlockSpec((tm, tn), lambda i, j: (i, j)),
        out_shape=jax.ShapeDtypeStruct((m, n), F32),
        input_output_aliases={0: 0},
        compiler_params=_params("parallel", "arbitrary"),
        name="matmul_residual",
    )(h, x, w)


def _ple_kernel(h_ref, x_ref, wg_ref, p_ref, we_ref, o_ref):
    gate = jax.nn.sigmoid(_dot(x_ref[...], wg_ref[...]))
    emb = _dot(p_ref[...], we_ref[...])
    o_ref[...] = h_ref[...] + gate * emb


def _ple(h, xn, w_gate, p, w_ple, *, tm=1024, tn=512):
    m, k = xn.shape
    n = w_gate.shape[1]
    kp = p.shape[1]
    return pl.pallas_call(
        _ple_kernel,
        grid=(m // tm, n // tn),
        in_specs=[
            pl.BlockSpec((tm, tn), lambda i, j: (i, j)),
            pl.BlockSpec((tm, k), lambda i, j: (i, 0)),
            pl.BlockSpec((k, tn), lambda i, j: (0, j)),
            pl.BlockSpec((tm, kp), lambda i, j: (i, 0)),
            pl.BlockSpec((kp, tn), lambda i, j: (0, j)),
        ],
        out_specs=pl.BlockSpec((tm, tn), lambda i, j: (i, j)),
        out_shape=jax.ShapeDtypeStruct((m, n), F32),
        input_output_aliases={0: 0},
        compiler_params=_params("parallel", "arbitrary"),
        name="ple",
    )(h, xn, w_gate, p, w_ple)


def _glu_kernel(x_ref, w_ref, b_ref, o_ref, *, n):
    z = _dot(x_ref[...], w_ref[...]) + b_ref[...]
    o_ref[...] = (z[:, :n] * jax.nn.sigmoid(z[:, n:])).astype(o_ref.dtype)


def _glu(x, w, b, *, tm=1024):
    m, k = x.shape
    n2 = w.shape[1]
    n = n2 // 2
    return pl.pallas_call(
        functools.partial(_glu_kernel, n=n),
        grid=(m // tm,),
        in_specs=[
            pl.BlockSpec((tm, k), lambda i: (i, 0)),
            pl.BlockSpec((k, n2), lambda i: (0, 0)),
            pl.BlockSpec((1, n2), lambda i: (0, 0)),
        ],
        out_specs=pl.BlockSpec((tm, n), lambda i: (i, 0)),
        out_shape=jax.ShapeDtypeStruct((m, n), BF16),
        compiler_params=_params("parallel"),
        name="glu",
    )(x, w, b.reshape(1, n2).astype(F32))


def _merge_kernel(yl_ref, ya_ref, ys_ref, pl_ref, pa_ref, ps_ref, gl_ref, ga_ref, gs_ref, o_ref):
    acc = jax.nn.sigmoid(gl_ref[...].astype(F32)) * _dot(yl_ref[...], pl_ref[...])
    acc += jax.nn.sigmoid(ga_ref[...].astype(F32)) * _dot(ya_ref[...], pa_ref[...])
    acc += jax.nn.sigmoid(gs_ref[...].astype(F32)) * _dot(ys_ref[...], ps_ref[...])
    o_ref[...] = acc.astype(o_ref.dtype)


def _gated_merge(y_lru, y_att, y_ssm, p_lru, p_att, p_ssm, proj, gate_off, *, tm=1024, tn=512):
    m = y_lru.shape[0]
    d = p_lru.shape[1]
    goff = gate_off // tn
    nd = d // tn

    def y_spec(y):
        return pl.BlockSpec((tm, y.shape[1]), lambda i, j: (i, 0))

    def p_spec(p):
        return pl.BlockSpec((p.shape[0], tn), lambda i, j: (0, j))

    def g_spec(b):
        return pl.BlockSpec((tm, tn), lambda i, j: (i, goff + b * nd + j))

    return pl.pallas_call(
        _merge_kernel,
        grid=(m // tm, nd),
        in_specs=[y_spec(y_lru), y_spec(y_att), y_spec(y_ssm), p_spec(p_lru), p_spec(p_att), p_spec(p_ssm),
                  g_spec(0), g_spec(1), g_spec(2)],
        out_specs=pl.BlockSpec((tm, tn), lambda i, j: (i, j)),
        out_shape=jax.ShapeDtypeStruct((m, d), BF16),
        compiler_params=_params("parallel", "arbitrary"),
        name="gated_merge",
    )(y_lru, y_att, y_ssm, p_lru, p_att, p_ssm, proj, proj, proj)


def _scan_rows8(a8, x8, hprev, row):
    for k in (1, 2, 4):
        keep = row >= k
        a_sh = jnp.where(keep, pltpu.roll(a8, k, 0), 1.0)
        x_sh = jnp.where(keep, pltpu.roll(x8, k, 0), 0.0)
        x8 = a8 * x_sh + x8
        a8 = a8 * a_sh
    return a8 * hprev + x8


def _lru_kernel(u_ref, cw_ref, cb_ref, wr_ref, br_ref, wi_ref, bi_ref, lam_ref, y_ref,
                tail_ref, h_ref, a_s, x_s, *, t_rows, n_blocks, block, conv_width):
    @pl.when(pl.program_id(1) == 0)
    def _():
        tail_ref[...] = jnp.zeros_like(tail_ref)
        h_ref[...] = jnp.zeros_like(h_ref)

    u = u_ref[0].astype(F32)
    ue = jnp.concatenate([tail_ref[...], u], axis=0)
    xc = cb_ref[...]
    for j in range(conv_width):
        off = 8 - (conv_width - 1) + j
        xc = xc + cw_ref[j:j + 1, :] * ue[off:off + t_rows, :]
    tail_ref[...] = u[t_rows - 8:, :]

    xcb = xc.astype(BF16)
    rs, igs = [], []
    for hb in range(n_blocks):
        xb = xcb[:, hb * block:(hb + 1) * block]
        rs.append(_dot(xb, wr_ref[hb]))
        igs.append(_dot(xb, wi_ref[hb]))
    r = jax.nn.sigmoid(jnp.concatenate(rs, axis=1) + br_ref[...])
    ig = jax.nn.sigmoid(jnp.concatenate(igs, axis=1) + bi_ref[...])
    lam = lam_ref[...]
    softplus_neg_lam = jnp.maximum(-lam, 0.0) + jnp.log1p(jnp.exp(-jnp.abs(lam)))
    log_a = (-LRU_C) * r * softplus_neg_lam
    a = jnp.exp(log_a)
    a_s[...] = a
    x_s[...] = jnp.sqrt(-jnp.tanh(log_a) * (a * a + 1.0)) * (ig * xc)

    w = a_s.shape[1]
    row = lax.broadcasted_iota(jnp.int32, (8, w), 0)

    def body(i, hprev):
        r0 = pl.multiple_of(i * 16, 16)
        h_a = _scan_rows8(a_s[pl.ds(r0, 8), :], x_s[pl.ds(r0, 8), :], hprev, row)
        h_b = _scan_rows8(a_s[pl.ds(r0 + 8, 8), :], x_s[pl.ds(r0 + 8, 8), :], h_a[7:8, :], row)
        y_ref[0, pl.ds(r0, 16), :] = jnp.concatenate([h_a, h_b], axis=0).astype(y_ref.dtype)
        return h_b[7:8, :]

    h_ref[...] = lax.fori_loop(0, t_rows // 16, body, h_ref[...])


def _rglru(proj3, conv_w, conv_b, w_rg, b_rg, w_ig, b_ig, lam, *, t_rows=256):
    bsz, seq, _ = proj3.shape
    n_blocks, block, _ = w_rg.shape
    w = n_blocks * block
    cw = conv_w.shape[0]
    vec = lambda v: v.reshape(1, w).astype(F32)
    full2 = lambda r, c: pl.BlockSpec((r, c), lambda b, t: (0, 0))
    full3 = pl.BlockSpec((n_blocks, block, block), lambda b, t: (0, 0, 0))
    return pl.pallas_call(
        functools.partial(_lru_kernel, t_rows=t_rows, n_blocks=n_blocks, block=block, conv_width=cw),
        grid=(bsz, seq // t_rows),
        in_specs=[pl.BlockSpec((1, t_rows, w), lambda b, t: (b, t, 0)), full2(cw, w), full2(1, w),
                  full3, full2(1, w), full3, full2(1, w), full2(1, w)],
        out_specs=pl.BlockSpec((1, t_rows, w), lambda b, t: (b, t, 0)),
        out_shape=jax.ShapeDtypeStruct((bsz, seq, w), BF16),
        scratch_shapes=[pltpu.VMEM((8, w), F32), pltpu.VMEM((1, w), F32),
                        pltpu.VMEM((t_rows, w), F32), pltpu.VMEM((t_rows, w), F32)],
        compiler_params=_params("parallel", "arbitrary"),
        name="rglru",
    )(proj3, conv_w.astype(F32), vec(conv_b), w_rg.astype(BF16), vec(b_rg), w_ig.astype(BF16), vec(b_ig),
      vec(lam))


def _head_rms(x, gain):
    x = x.astype(F32)
    return x * lax.rsqrt(jnp.mean(x * x, axis=-1, keepdims=True) + EPS) * gain


def _attn_kernel(q_ref, kp_ref, kc_ref, vp_ref, vc_ref, bias_ref, qg_ref, kg_ref, o_ref, *, qb):
    qn = (_head_rms(q_ref[0], qg_ref[...]) * (HEAD_DIM ** -0.5)).astype(BF16)
    kn = _head_rms(jnp.concatenate([kp_ref[0], kc_ref[0]], axis=0), kg_ref[...]).astype(BF16)
    s = lax.dot_general(qn, kn, (((1,), (1,)), ((), ())), preferred_element_type=F32)
    s = s + bias_ref[0]
    col = lax.broadcasted_iota(jnp.int32, s.shape, 1)
    first = pl.program_id(2) == 0
    s = jnp.where(jnp.logical_and(first, col < qb), NEG_INF, s)
    p = jnp.exp(s - jnp.max(s, axis=-1, keepdims=True))
    denom = jnp.sum(p, axis=-1, keepdims=True)
    v = jnp.concatenate([vp_ref[0], vc_ref[0]], axis=0)
    o_ref[0] = (_dot(p.astype(BF16), v) / denom).astype(o_ref.dtype)


def _band_bias(rel_bias, qb):
    i = jnp.arange(qb)[:, None]
    j = jnp.arange(2 * qb)[None, :]
    dist = qb + i - j
    q_chunk = qb // CHUNK + i // CHUNK
    k_chunk = j // CHUNK
    in_band = jnp.logical_and(k_chunk >= q_chunk - LEFT_CHUNKS, k_chunk <= q_chunk)
    table = rel_bias.astype(F32)[:, jnp.clip(dist, -REL_CLIP, REL_CLIP) + REL_CLIP]
    return jnp.where(in_band[None], table, NEG_INF)


def _attention(proj3, q_off, k_off, v_off, n_heads, q_gain, k_gain, rel_bias, *, qb=ATTN_QBLOCK):
    assert qb % CHUNK == 0 and qb >= LEFT_CHUNKS * CHUNK
    bsz, seq, _ = proj3.shape
    qo, ko, vo = q_off // HEAD_DIM, k_off // HEAD_DIM, v_off // HEAD_DIM
    cur = lambda off: pl.BlockSpec((1, qb, HEAD_DIM), lambda b, h, n: (b, n, off + h))
    prev = lambda off: pl.BlockSpec((1, qb, HEAD_DIM), lambda b, h, n: (b, jnp.maximum(n - 1, 0), off + h))
    gain = pl.BlockSpec((1, HEAD_DIM), lambda b, h, n: (0, 0))
    return pl.pallas_call(
        functools.partial(_attn_kernel, qb=qb),
        grid=(bsz, n_heads, seq // qb),
        in_specs=[cur(qo), prev(ko), cur(ko), prev(vo), cur(vo),
                  pl.BlockSpec((1, qb, 2 * qb), lambda b, h, n: (h, 0, 0)), gain, gain],
        out_specs=pl.BlockSpec((1, qb, HEAD_DIM), lambda b, h, n: (b, n, h)),
        out_shape=jax.ShapeDtypeStruct((bsz, seq, n_heads * HEAD_DIM), BF16),
        compiler_params=_params("parallel", "parallel", "arbitrary"),
        name="band_attention",
    )(proj3, proj3, proj3, proj3, proj3, _band_bias(rel_bias, qb),
      q_gain.reshape(1, HEAD_DIM).astype(F32), k_gain.reshape(1, HEAD_DIM).astype(F32))


def _s5_tables(a_re, a_im, log_dt, b_re, b_im, c_re, c_im, d_skip, t_chunk):
    hi = lax.Precision.HIGHEST
    g, p = a_re.shape
    c = b_re.shape[-1]
    dt = jnp.exp(log_dt.astype(F32))[:, None]
    ar, ai = a_re.astype(F32), a_im.astype(F32)

    def apow(tau):
        tau = jnp.asarray(tau, F32)[:, None, None]
        mag = jnp.exp(tau * dt * ar)
        return mag * jnp.cos(tau * dt * ai), mag * jnp.sin(tau * dt * ai)

    abar_re, abar_im = (x[0] for x in apow([1.0]))
    den = ar * ar + ai * ai
    nr, ni = abar_re - 1.0, abar_im
    coef_re = (nr * ar + ni * ai) / den
    coef_im = (ni * ar - nr * ai) / den
    br, bi = b_re.astype(F32), b_im.astype(F32)
    bbar_re = coef_re[..., None] * br - coef_im[..., None] * bi
    bbar_im = coef_re[..., None] * bi + coef_im[..., None] * br
    cr, ci = c_re.astype(F32), c_im.astype(F32)

    pr, pi = apow(jnp.arange(t_chunk + 1))
    ca_re = cr[None] * pr[:, :, None, :] - ci[None] * pi[:, :, None, :]
    ca_im = cr[None] * pi[:, :, None, :] + ci[None] * pr[:, :, None, :]
    k_tap = (jnp.einsum("tgcp,gpd->tgcd", ca_re[:t_chunk], bbar_re, precision=hi)
             - jnp.einsum("tgcp,gpd->tgcd", ca_im[:t_chunk], bbar_im, precision=hi))
    s_idx = jnp.arange(t_chunk)[:, None]
    t_idx = jnp.arange(t_chunk)[None, :]
    lag = t_idx - s_idx
    k_st = jnp.where((lag >= 0)[:, :, None, None, None], k_tap[jnp.maximum(lag, 0)], 0.0)
    k_mat = jnp.transpose(k_st, (2, 0, 4, 1, 3)).reshape(g, t_chunk * c, t_chunk * c)
    rev_re, rev_im = pr[t_chunk - 1 - jnp.arange(t_chunk)], pi[t_chunk - 1 - jnp.arange(t_chunk)]
    s_re = rev_re[..., None] * bbar_re[None] - rev_im[..., None] * bbar_im[None]
    s_im = rev_re[..., None] * bbar_im[None] + rev_im[..., None] * bbar_re[None]
    s_mat = jnp.concatenate([jnp.transpose(s_re, (1, 0, 3, 2)), jnp.transpose(s_im, (1, 0, 3, 2))], axis=-1)
    s_mat = s_mat.reshape(g, t_chunk * c, 2 * p)
    w_in = jnp.concatenate([k_mat, s_mat], axis=-1).astype(BF16)
    o_re = jnp.transpose(ca_re[1:], (1, 3, 0, 2)).reshape(g, p, t_chunk * c)
    o_im = jnp.transpose(-ca_im[1:], (1, 3, 0, 2)).reshape(g, p, t_chunk * c)
    w_state = jnp.concatenate([o_re, o_im], axis=1).astype(BF16)
    qr, qi = apow(t_chunk * jnp.arange(1, 9))
    consts = jnp.concatenate([jnp.concatenate([qr, qr], axis=-1), jnp.concatenate([-qi, qi], axis=-1)], axis=0)
    consts = jnp.transpose(consts, (1, 0, 2))
    d_row = jnp.tile(d_skip.astype(F32).reshape(g, 1, c), (1, t_chunk, 1)).reshape(g, 1, t_chunk * c)
    return w_in, w_state, consts, d_row


def _gelu_tanh(y):
    return 0.5 * y * (1.0 + jnp.tanh(math.sqrt(2.0 / math.pi) * (y + 0.044715 * (y * y * y))))


def _s5_kernel(u_ref, win_ref, wst_ref, cst_ref, d_ref, y_ref, inc_s, hprev_s, *, n_rows, tc, p2):
    u = u_ref[0, 0]
    z = _dot(u, win_ref[0])
    y_intra = z[:, :tc]
    inc_s[...] = z[:, tc:]
    cst = cst_ref[0]
    c_re, c_im = cst[0:8, :], cst[8:16, :]
    row = lax.broadcasted_iota(jnp.int32, (8, p2), 0)

    def cmul(x, m_re, m_im):
        return x * m_re + pltpu.roll(x, p2 // 2, 1) * m_im

    def body(i, hprev):
        r0 = pl.multiple_of(i * 8, 8)
        x = inc_s[pl.ds(r0, 8), :]
        for k in (1, 2, 4):
            shifted = jnp.where(row >= k, pltpu.roll(x, k, 0), 0.0)
            x = x + cmul(shifted, c_re[k - 1:k, :], c_im[k - 1:k, :])
        h_in = jnp.broadcast_to(hprev, (8, p2))
        h_out = x + cmul(h_in, c_re, c_im)
        hprev_s[pl.ds(r0, 8), :] = jnp.where(row >= 1, pltpu.roll(h_out, 1, 0), h_in)
        return h_out[7:8, :]

    lax.fori_loop(0, n_rows // 8, body, jnp.zeros((1, p2), F32))
    y_state = _dot(hprev_s[...].astype(BF16), wst_ref[0])
    y = y_intra + y_state + d_ref[0] * u.astype(F32)
    y_ref[0, 0] = _gelu_tanh(y).astype(y_ref.dtype)


def _s5(u, tables, *, t_chunk=S5_CHUNK):
    w_in, w_state, consts, d_row = tables
    bsz, seq, width = u.shape
    g = w_in.shape[0]
    c = width // g
    n_rows = seq // t_chunk
    tc = t_chunk * c
    p2 = w_state.shape[1]
    ug = jnp.transpose(u.reshape(bsz, n_rows, t_chunk, g, c), (3, 0, 1, 2, 4)).reshape(g, bsz, n_rows, tc)
    yg = pl.pallas_call(
        functools.partial(_s5_kernel, n_rows=n_rows, tc=tc, p2=p2),
        grid=(g, bsz),
        in_specs=[
            pl.BlockSpec((1, 1, n_rows, tc), lambda gi, b: (gi, b, 0, 0)),
            pl.BlockSpec((1, tc, tc + p2), lambda gi, b: (gi, 0, 0)),
            pl.BlockSpec((1, p2, tc), lambda gi, b: (gi, 0, 0)),
            pl.BlockSpec((1, 16, p2), lambda gi, b: (gi, 0, 0)),
            pl.BlockSpec((1, 1, tc), lambda gi, b: (gi, 0, 0)),
        ],
        out_specs=pl.BlockSpec((1, 1, n_rows, tc), lambda gi, b: (gi, b, 0, 0)),
        out_shape=jax.ShapeDtypeStruct((g, bsz, n_rows, tc), BF16),
        scratch_shapes=[pltpu.VMEM((n_rows, p2), F32), pltpu.VMEM((n_rows, p2), F32)],
        compiler_params=_params("parallel", "parallel"),
        name="s5",
    )(ug, w_in, w_state, consts, d_row)
    return jnp.transpose(yg.reshape(g, bsz, n_rows, t_chunk, c), (1, 2, 3, 0, 4)).reshape(bsz, seq, width)


def _router_kernel(x_ref, w_ref, b_ref, gate_ref, *, n_groups, per_group):
    n_exp = n_groups * per_group
    logits = _dot(x_ref[...], w_ref[...]) + b_ref[...]
    lane = lax.broadcasted_iota(jnp.int32, logits.shape, 1).astype(F32)
    big = float(ROUTER_LANES)
    is_group = jnp.logical_and(lane >= n_exp, lane < n_exp + n_groups)
    gl = jnp.where(is_group, logits, -jnp.inf)
    gmax = jnp.max(gl, axis=-1, keepdims=True)
    gsel = jnp.min(jnp.where(gl == gmax, lane, big), axis=-1, keepdims=True) - n_exp
    gprob = 1.0 / jnp.sum(jnp.where(is_group, jnp.exp(logits - gmax), 0.0), axis=-1, keepdims=True)
    in_group = jnp.logical_and(lane >= gsel * per_group, lane < (gsel + 1.0) * per_group)
    el = jnp.where(in_group, logits, -jnp.inf)
    v1 = jnp.max(el, axis=-1, keepdims=True)
    i1 = jnp.min(jnp.where(el == v1, lane, big), axis=-1, keepdims=True)
    el2 = jnp.where(lane == i1, -jnp.inf, el)
    v2 = jnp.max(el2, axis=-1, keepdims=True)
    i2 = jnp.min(jnp.where(el2 == v2, lane, big), axis=-1, keepdims=True)
    e2 = jnp.exp(v2 - v1)
    w1 = gprob / (1.0 + e2)
    w2 = gprob * e2 / (1.0 + e2)
    gate_ref[...] = jnp.where(lane == i1, w1, 0.0) + jnp.where(lane == i2, w2, 0.0)


def _router(xn, w_gr, b_gr, w_er, b_er, *, tm=1024):
    m, d = xn.shape
    n_groups, _, per_group = w_er.shape
    n_exp = n_groups * per_group
    assert n_exp + n_groups <= ROUTER_LANES
    w = jnp.concatenate([jnp.transpose(w_er, (1, 0, 2)).reshape(d, n_exp), w_gr], axis=1)
    w = jnp.pad(w, ((0, 0), (0, ROUTER_LANES - n_exp - n_groups))).astype(BF16)
    b = jnp.concatenate([b_er.reshape(n_exp), b_gr]).astype(F32)
    b = jnp.pad(b, (0, ROUTER_LANES - n_exp - n_groups)).reshape(1, ROUTER_LANES)
    return pl.pallas_call(
        functools.partial(_router_kernel, n_groups=n_groups, per_group=per_group),
        grid=(m // tm,),
        in_specs=[pl.BlockSpec((tm, d), lambda i: (i, 0)), pl.BlockSpec((d, ROUTER_LANES), lambda i: (0, 0)),
                  pl.BlockSpec((1, ROUTER_LANES), lambda i: (0, 0))],
        out_specs=pl.BlockSpec((tm, ROUTER_LANES), lambda i: (i, 0)),
        out_shape=jax.ShapeDtypeStruct((m, ROUTER_LANES), F32),
        compiler_params=_params("parallel"),
        name="moe_router",
    )(xn, w, b)


def _expert_up_kernel(x_ref, wg_ref, wu_ref, gate_ref, o_ref):
    e = pl.program_id(1)
    x = x_ref[...]
    g = _dot(x, wg_ref[0])
    u = _dot(x, wu_ref[0])
    gate = gate_ref[...]
    lane = lax.broadcasted_iota(jnp.int32, gate.shape, 1)
    col = jnp.sum(jnp.where(lane == e, gate, 0.0), axis=-1, keepdims=True)
    o_ref[...] = (g * jax.nn.sigmoid(g) * u * col).astype(o_ref.dtype)


def _expert_up(xn, w_gate, w_up, gate, *, tm=1024):
    m, d = xn.shape
    n_exp, _, ff = w_gate.shape
    return pl.pallas_call(
        _expert_up_kernel,
        grid=(m // tm, n_exp),
        in_specs=[pl.BlockSpec((tm, d), lambda i, e: (i, 0)),
                  pl.BlockSpec((1, d, ff), lambda i, e: (e, 0, 0)),
                  pl.BlockSpec((1, d, ff), lambda i, e: (e, 0, 0)),
                  pl.BlockSpec((tm, ROUTER_LANES), lambda i, e: (i, 0))],
        out_specs=pl.BlockSpec((tm, ff), lambda i, e: (i, e)),
        out_shape=jax.ShapeDtypeStruct((m, n_exp * ff), BF16),
        compiler_params=_params("parallel", "arbitrary"),
        name="moe_expert_up",
    )(xn, w_gate, w_up, gate)


def _expert_down_kernel(h_ref, x_ref, w_ref, o_ref, acc_ref):
    k = pl.program_id(2)

    @pl.when(k == 0)
    def _():
        acc_ref[...] = jnp.zeros_like(acc_ref)

    ne, ff, tn = w_ref.shape
    acc_ref[...] += _dot(x_ref[...], w_ref[...].reshape(ne * ff, tn))

    @pl.when(k == pl.num_programs(2) - 1)
    def _():
        o_ref[...] = h_ref[...] + acc_ref[...]


def _expert_down(h, hid, w_down, *, tm=1024, tn=1024, experts_per_step=4):
    m, kk = hid.shape
    n_exp, ff, d = w_down.shape
    tk = experts_per_step * ff
    return pl.pallas_call(
        _expert_down_kernel,
        grid=(m // tm, d // tn, kk // tk),
        in_specs=[pl.BlockSpec((tm, tn), lambda i, j, k: (i, j)),
                  pl.BlockSpec((tm, tk), lambda i, j, k: (i, k)),
                  pl.BlockSpec((experts_per_step, ff, tn), lambda i, j, k: (k, 0, j))],
        out_specs=pl.BlockSpec((tm, tn), lambda i, j, k: (i, j)),
        out_shape=jax.ShapeDtypeStruct((m, d), F32),
        scratch_shapes=[pltpu.VMEM((tm, tn), F32)],
        input_output_aliases={0: 0},
        compiler_params=_params("parallel", "parallel", "arbitrary"),
        name="moe_expert_down",
    )(h, hid, w_down)


def kernel(x, p, mix_gain, w_in, conv_w, conv_b, w_rgate, b_rgate, w_igate, b_igate, lru_lambda, q_gain, k_gain, rel_bias, ssm_a_re, ssm_a_im, ssm_log_dt, ssm_b_re, ssm_b_im, ssm_c_re, ssm_c_im, ssm_d, w_glu, b_glu, w_proj_lru, w_proj_att, w_proj_ssm, w_out, ffn_gain, w_group_router, b_group_router, w_expert_router, b_expert_router, w_up, w_gate, w_down, ple_gain, w_ple, w_ple_gate):
    bsz, seq, d = x.shape
    depth = w_in.shape[0]
    m = bsz * seq
    lru_w = w_proj_lru.shape[1]
    att_w = w_proj_att.shape[1]
    ssm_w = w_proj_ssm.shape[1]
    n_heads = att_w // HEAD_DIM
    q_off, k_off, v_off = lru_w, lru_w + att_w, lru_w + 2 * att_w
    ssm_off = lru_w + 3 * att_w
    gate_off = ssm_off + ssm_w
    in_width = w_in.shape[2]

    h = x.reshape(m, d).astype(F32)
    for i in range(depth):
        xn = _rmsnorm(h, mix_gain[i])
        proj = _matmul(xn, w_in[i].astype(BF16))
        proj3 = proj.reshape(bsz, seq, in_width)
        y_lru = _rglru(proj3, conv_w[i], conv_b[i], w_rgate[i], b_rgate[i], w_igate[i], b_igate[i], lru_lambda[i])
        y_att = _attention(proj3, q_off, k_off, v_off, n_heads, q_gain[i], k_gain[i], rel_bias[i])
        tables = _s5_tables(ssm_a_re[i], ssm_a_im[i], ssm_log_dt[i], ssm_b_re[i], ssm_b_im[i],
                            ssm_c_re[i], ssm_c_im[i], ssm_d[i], S5_CHUNK)
        y_act = _s5(proj3[:, :, ssm_off:ssm_off + ssm_w], tables)
        y_ssm = _glu(y_act.reshape(m, ssm_w), w_glu[i].astype(BF16), b_glu[i])
        merged = _gated_merge(y_lru.reshape(m, lru_w), y_att.reshape(m, att_w), y_ssm,
                              w_proj_lru[i].astype(BF16), w_proj_att[i].astype(BF16),
                              w_proj_ssm[i].astype(BF16), proj, gate_off)
        h = _matmul_residual(h, merged, w_out[i].astype(BF16))
        hn = _rmsnorm(h, ffn_gain[i])
        gate = _router(hn, w_group_router[i], b_group_router[i], w_expert_router[i], b_expert_router[i])
        hid = _expert_up(hn, w_gate[i].astype(BF16), w_up[i].astype(BF16), gate)
        h = _expert_down(h, hid, w_down[i].astype(BF16))
        h = _ple(h, _rmsnorm(h, ple_gain[i]), w_ple_gate[i].astype(BF16),
                 p[i].reshape(m, -1).astype(BF16), w_ple[i].astype(BF16))
    return h.reshape(bsz, seq, d)
```

```python
import functools
import math

import jax
import jax.numpy as jnp
from jax import lax
from jax.experimental import pallas as pl
from jax.experimental.pallas import tpu as pltpu

F32 = jnp.float32
BF16 = jnp.bfloat16
U32 = jnp.uint32

CHUNK = 64
LEFT_CHUNKS = 8
REL_CLIP = 128
HEAD_DIM = 128
LRU_C = 8.0
TOP_K = 2
EPS = 1e-6
NEG_INF = -1e30

LANES = 128
S5_CHUNK = 8
ATTN_QBLOCK = LEFT_CHUNKS * CHUNK
MOE_ROW_TILE = 256
MOE_TOKEN_TILE = 256
V7X_VMEM_LIMIT_BYTES = 56 * 1024 * 1024
HI16 = 0xFFFF0000


def _params(*semantics):
    return pltpu.CompilerParams(dimension_semantics=semantics, vmem_limit_bytes=V7X_VMEM_LIMIT_BYTES)


def _dot(a, b):
    return jnp.dot(a, b, preferred_element_type=F32)


def _pack_halves(x):
    n = x.shape[1] // 2
    lo = pltpu.bitcast(x[:, :n].astype(BF16).astype(F32), U32)
    hi = pltpu.bitcast(x[:, n:].astype(BF16).astype(F32), U32)
    return (lo >> 16) | (hi & jnp.uint32(HI16))


def _unpack_halves(u):
    return pltpu.bitcast(u << 16, F32), pltpu.bitcast(u & jnp.uint32(HI16), F32)


def _rmsnorm_kernel(x_ref, g_ref, o_ref):
    x = x_ref[...].astype(F32)
    ms = jnp.mean(x * x, axis=-1, keepdims=True)
    o_ref[...] = (x * lax.rsqrt(ms + EPS) * g_ref[...]).astype(o_ref.dtype)


def _rmsnorm(x, gain, *, tm=512):
    m, d = x.shape
    return pl.pallas_call(
        _rmsnorm_kernel,
        grid=(m // tm,),
        in_specs=[pl.BlockSpec((tm, d), lambda i: (i, 0)), pl.BlockSpec((1, d), lambda i: (0, 0))],
        out_specs=pl.BlockSpec((tm, d), lambda i: (i, 0)),
        out_shape=jax.ShapeDtypeStruct((m, d), BF16),
        compiler_params=_params("parallel"),
        name="rmsnorm",
    )(x, gain.reshape(1, d).astype(F32))


def _mm_kernel(x_ref, w_ref, o_ref):
    o_ref[...] = _dot(x_ref[...], w_ref[...]).astype(o_ref.dtype)


def _matmul(x, w, *, tm=1024, tn=1024, out_dtype=BF16):
    m, k = x.shape
    n = w.shape[1]
    return pl.pallas_call(
        _mm_kernel,
        grid=(m // tm, n // tn),
        in_specs=[pl.BlockSpec((tm, k), lambda i, j: (i, 0)), pl.BlockSpec((k, tn), lambda i, j: (0, j))],
        out_specs=pl.BlockSpec((tm, tn), lambda i, j: (i, j)),
        out_shape=jax.ShapeDtypeStruct((m, n), out_dtype),
        compiler_params=_params("parallel", "arbitrary"),
        name="matmul",
    )(x, w)


def _mm_residual_kernel(h_ref, x_ref, w_ref, o_ref):
    o_ref[...] = h_ref[...] + _dot(x_ref[...], w_ref[...])


def _matmul_residual(h, x, w, *, tm=1024, tn=512):
    m, k = x.shape
    n = w.shape[1]
    return pl.pallas_call(
        _mm_residual_kernel,
        grid=(m // tm, n // tn),
        in_specs=[
            pl.BlockSpec((tm, tn), lambda i, j: (i, j)),
            pl.BlockSpec((tm, k), lambda i, j: (i, 0)),
            pl.BlockSpec((k, tn), lambda i, j: (0, j)),
        ],
        out_specs=pl.BlockSpec((tm, tn), lambda i, j: (i, j)),
        out_shape=jax.ShapeDtypeStruct((m, n), F32),
        input_output_aliases={0: 0},
        compiler_params=_params("parallel", "arbitrary"),
        name="matmul_residual",
    )(h, x, w)


def _ple_kernel(h_ref, x_ref, wg_ref, p_ref, we_ref, o_ref):
    gate = jax.nn.sigmoid(_dot(x_ref[...], wg_ref[...]))
    emb = _dot(p_ref[...], we_ref[...])
    o_ref[...] = h_ref[...] + gate * emb


def _ple(h, xn, w_gate, p, w_ple, *, tm=1024, tn=512):
    m, k = xn.shape
    n = w_gate.shape[1]
    kp = p.shape[1]
    return pl.pallas_call(
        _ple_kernel,
        grid=(m // tm, n // tn),
        in_specs=[
            pl.BlockSpec((tm, tn), lambda i, j: (i, j)),
            pl.BlockSpec((tm, k), lambda i, j: (i, 0)),
            pl.BlockSpec((k, tn), lambda i, j: (0, j)),
            pl.BlockSpec((tm, kp), lambda i, j: (i, 0)),
            pl.BlockSpec((kp, tn), lambda i, j: (0, j)),
        ],
        out_specs=pl.BlockSpec((tm, tn), lambda i, j: (i, j)),
        out_shape=jax.ShapeDtypeStruct((m, n), F32),
        input_output_aliases={0: 0},
        compiler_params=_params("parallel", "arbitrary"),
        name="ple",
    )(h, xn, w_gate, p, w_ple)


def _merge_kernel(yl_ref, ya_ref, ys_ref, pl_ref, pa_ref, ps_ref, gl_ref, ga_ref, gs_ref, o_ref):
    acc = jax.nn.sigmoid(gl_ref[...].astype(F32)) * _dot(yl_ref[...], pl_ref[...])
    acc += jax.nn.sigmoid(ga_ref[...].astype(F32)) * _dot(ya_ref[...], pa_ref[...])
    acc += jax.nn.sigmoid(gs_ref[...].astype(F32)) * _dot(ys_ref[...], ps_ref[...])
    o_ref[...] = acc.astype(o_ref.dtype)


def _gated_merge(y_lru, y_att, y_ssm, p_lru, p_att, p_ssm, proj, gate_off, *, tm=1024, tn=512):
    m = y_lru.shape[0]
    d = p_lru.shape[1]
    goff = gate_off // tn
    nd = d // tn

    def y_spec(y):
        return pl.BlockSpec((tm, y.shape[1]), lambda i, j: (i, 0))

    def p_spec(p):
        return pl.BlockSpec((p.shape[0], tn), lambda i, j: (0, j))

    def g_spec(b):
        return pl.BlockSpec((tm, tn), lambda i, j: (i, goff + b * nd + j))

    return pl.pallas_call(
        _merge_kernel,
        grid=(m // tm, nd),
        in_specs=[y_spec(y_lru), y_spec(y_att), y_spec(y_ssm), p_spec(p_lru), p_spec(p_att), p_spec(p_ssm),
                  g_spec(0), g_spec(1), g_spec(2)],
        out_specs=pl.BlockSpec((tm, tn), lambda i, j: (i, j)),
        out_shape=jax.ShapeDtypeStruct((m, d), BF16),
        compiler_params=_params("parallel", "arbitrary"),
        name="gated_merge",
    )(y_lru, y_att, y_ssm, p_lru, p_att, p_ssm, proj, proj, proj)


def _scan_rows8(a8, x8, hprev, row):
    for k in (1, 2, 4):
        keep = row >= k
        a_sh = jnp.where(keep, pltpu.roll(a8, k, 0), 1.0)
        x_sh = jnp.where(keep, pltpu.roll(x8, k, 0), 0.0)
        x8 = a8 * x_sh + x8
        a8 = a8 * a_sh
    return a8 * hprev + x8


def _lru_kernel(u_ref, cw_ref, cb_ref, wr_ref, br_ref, wi_ref, bi_ref, lam_ref, y_ref,
                tail_ref, h_ref, a_s, x_s, *, t_rows, n_blocks, block, conv_width):
    @pl.when(pl.program_id(1) == 0)
    def _():
        tail_ref[...] = jnp.zeros_like(tail_ref)
        h_ref[...] = jnp.zeros_like(h_ref)

    u = u_ref[0].astype(F32)
    ue = jnp.concatenate([tail_ref[...], u], axis=0)
    xc = cb_ref[...]
    for j in range(conv_width):
        off = 8 - (conv_width - 1) + j
        xc = xc + cw_ref[j:j + 1, :] * ue[off:off + t_rows, :]
    tail_ref[...] = u[t_rows - 8:, :]

    xcb = xc.astype(BF16)
    rs, igs = [], []
    for hb in range(n_blocks):
        xb = xcb[:, hb * block:(hb + 1) * block]
        rs.append(_dot(xb, wr_ref[hb]))
        igs.append(_dot(xb, wi_ref[hb]))
    r = jax.nn.sigmoid(jnp.concatenate(rs, axis=1) + br_ref[...])
    ig = jax.nn.sigmoid(jnp.concatenate(igs, axis=1) + bi_ref[...])
    lam = lam_ref[...]
    softplus_neg_lam = jnp.maximum(-lam, 0.0) + jnp.log1p(jnp.exp(-jnp.abs(lam)))
    log_a = (-LRU_C) * r * softplus_neg_lam
    a = jnp.exp(log_a)
    a_s[...] = a
    x_s[...] = jnp.sqrt(-jnp.tanh(log_a) * (a * a + 1.0)) * (ig * xc)

    w = a_s.shape[1]
    row = lax.broadcasted_iota(jnp.int32, (8, w), 0)

    def body(i, hprev):
        r0 = pl.multiple_of(i * 16, 16)
        h_a = _scan_rows8(a_s[pl.ds(r0, 8), :], x_s[pl.ds(r0, 8), :], hprev, row)
        h_b = _scan_rows8(a_s[pl.ds(r0 + 8, 8), :], x_s[pl.ds(r0 + 8, 8), :], h_a[7:8, :], row)
        y_ref[0, pl.ds(r0, 16), :] = jnp.concatenate([h_a, h_b], axis=0).astype(y_ref.dtype)
        return h_b[7:8, :]

    h_ref[...] = lax.fori_loop(0, t_rows // 16, body, h_ref[...])


def _rglru(proj3, conv_w, conv_b, w_rg, b_rg, w_ig, b_ig, lam, *, t_rows=256):
    bsz, seq, _ = proj3.shape
    n_blocks, block, _ = w_rg.shape
    w = n_blocks * block
    cw = conv_w.shape[0]
    vec = lambda v: v.reshape(1, w).astype(F32)
    full2 = lambda r, c: pl.BlockSpec((r, c), lambda b, t: (0, 0))
    full3 = pl.BlockSpec((n_blocks, block, block), lambda b, t: (0, 0, 0))
    return pl.pallas_call(
        functools.partial(_lru_kernel, t_rows=t_rows, n_blocks=n_blocks, block=block, conv_width=cw),
        grid=(bsz, seq // t_rows),
        in_specs=[pl.BlockSpec((1, t_rows, w), lambda b, t: (b, t, 0)), full2(cw, w), full2(1, w),
                  full3, full2(1, w), full3, full2(1, w), full2(1, w)],
        out_specs=pl.BlockSpec((1, t_rows, w), lambda b, t: (b, t, 0)),
        out_shape=jax.ShapeDtypeStruct((bsz, seq, w), BF16),
        scratch_shapes=[pltpu.VMEM((8, w), F32), pltpu.VMEM((1, w), F32),
                        pltpu.VMEM((t_rows, w), F32), pltpu.VMEM((t_rows, w), F32)],
        compiler_params=_params("parallel", "arbitrary"),
        name="rglru",
    )(proj3, conv_w.astype(F32), vec(conv_b), w_rg.astype(BF16), vec(b_rg), w_ig.astype(BF16), vec(b_ig),
      vec(lam))


def _head_rms(x, gain):
    x = x.astype(F32)
    return x * lax.rsqrt(jnp.mean(x * x, axis=-1, keepdims=True) + EPS) * gain


def _attn_kernel(q_ref, kp_ref, kc_ref, vp_ref, vc_ref, bias_ref, qg_ref, kg_ref, o_ref, *, qb):
    qn = (_head_rms(q_ref[0], qg_ref[...]) * (HEAD_DIM ** -0.5)).astype(BF16)
    kn = _head_rms(jnp.concatenate([kp_ref[0], kc_ref[0]], axis=0), kg_ref[...]).astype(BF16)
    s = lax.dot_general(qn, kn, (((1,), (1,)), ((), ())), preferred_element_type=F32)
    s = s + bias_ref[0]
    col = lax.broadcasted_iota(jnp.int32, s.shape, 1)
    first = pl.program_id(2) == 0
    s = jnp.where(jnp.logical_and(first, col < qb), NEG_INF, s)
    p = jnp.exp(s - jnp.max(s, axis=-1, keepdims=True))
    denom = jnp.sum(p, axis=-1, keepdims=True)
    v = jnp.concatenate([vp_ref[0], vc_ref[0]], axis=0)
    o_ref[0] = (_dot(p.astype(BF16), v) / denom).astype(o_ref.dtype)


def _band_bias(rel_bias, qb):
    period = 3 * qb
    mm = jnp.arange(period)
    j_minus_i = jnp.where(mm < 2 * qb, mm, mm - period)
    dist = qb - j_minus_i
    vec = rel_bias.astype(F32)[:, jnp.clip(dist, -REL_CLIP, REL_CLIP) + REL_CLIP]
    n_heads = vec.shape[0]
    table = jnp.tile(vec, (1, qb))[:, :qb * (period - 1)].reshape(n_heads, qb, period - 1)[:, :, :2 * qb]
    i = jnp.arange(qb)[:, None]
    j = jnp.arange(2 * qb)[None, :]
    q_chunk = qb // CHUNK + i // CHUNK
    k_chunk = j // CHUNK
    in_band = jnp.logical_and(k_chunk >= q_chunk - LEFT_CHUNKS, k_chunk <= q_chunk)
    return jnp.where(in_band[None], table, NEG_INF)


def _attention(proj3, q_off, k_off, v_off, n_heads, q_gain, k_gain, rel_bias, *, qb=ATTN_QBLOCK):
    assert qb % CHUNK == 0 and qb >= LEFT_CHUNKS * CHUNK
    bsz, seq, _ = proj3.shape
    qo, ko, vo = q_off // HEAD_DIM, k_off // HEAD_DIM, v_off // HEAD_DIM
    cur = lambda off: pl.BlockSpec((1, qb, HEAD_DIM), lambda b, h, n: (b, n, off + h))
    prev = lambda off: pl.BlockSpec((1, qb, HEAD_DIM), lambda b, h, n: (b, jnp.maximum(n - 1, 0), off + h))
    gain = pl.BlockSpec((1, HEAD_DIM), lambda b, h, n: (0, 0))
    return pl.pallas_call(
        functools.partial(_attn_kernel, qb=qb),
        grid=(bsz, n_heads, seq // qb),
        in_specs=[cur(qo), prev(ko), cur(ko), prev(vo), cur(vo),
                  pl.BlockSpec((1, qb, 2 * qb), lambda b, h, n: (h, 0, 0)), gain, gain],
        out_specs=pl.BlockSpec((1, qb, HEAD_DIM), lambda b, h, n: (b, n, h)),
        out_shape=jax.ShapeDtypeStruct((bsz, seq, n_heads * HEAD_DIM), BF16),
        compiler_params=_params("parallel", "parallel", "arbitrary"),
        name="band_attention",
    )(proj3, proj3, proj3, proj3, proj3, _band_bias(rel_bias, qb),
      q_gain.reshape(1, HEAD_DIM).astype(F32), k_gain.reshape(1, HEAD_DIM).astype(F32))


def _s5_tables(a_re, a_im, log_dt, b_re, b_im, c_re, c_im, d_skip, t_chunk):
    hi = lax.Precision.HIGHEST
    g, p = a_re.shape
    c = b_re.shape[-1]
    gl = LANES // c
    nj = g // gl
    dt = jnp.exp(log_dt.astype(F32))[:, None]
    ar, ai = a_re.astype(F32), a_im.astype(F32)

    def apow(tau):
        tau = jnp.asarray(tau, F32)[:, None, None]
        mag = jnp.exp(tau * dt * ar)
        return mag * jnp.cos(tau * dt * ai), mag * jnp.sin(tau * dt * ai)

    pr, pi = apow(jnp.arange(t_chunk + 1))
    abar_re, abar_im = pr[1], pi[1]
    den = ar * ar + ai * ai
    nr, ni = abar_re - 1.0, abar_im
    coef_re = (nr * ar + ni * ai) / den
    coef_im = (ni * ar - nr * ai) / den
    br, bi = b_re.astype(F32), b_im.astype(F32)
    bbar_re = coef_re[..., None] * br - coef_im[..., None] * bi
    bbar_im = coef_re[..., None] * bi + coef_im[..., None] * br
    cr, ci = c_re.astype(F32), c_im.astype(F32)
    ca_re = cr[None] * pr[:, :, None, :] - ci[None] * pi[:, :, None, :]
    ca_im = cr[None] * pi[:, :, None, :] + ci[None] * pr[:, :, None, :]
    eye = jnp.eye(gl, dtype=F32)

    k_tap = (jnp.einsum("tgcp,gpd->tgcd", ca_re[:t_chunk], bbar_re, precision=hi)
             - jnp.einsum("tgcp,gpd->tgcd", ca_im[:t_chunk], bbar_im, precision=hi))
    lag = jnp.arange(t_chunk)[None, :] - jnp.arange(t_chunk)[:, None]
    k_st = jnp.where((lag >= 0)[:, :, None, None, None], k_tap[jnp.maximum(lag, 0)], 0.0)
    k_base = jnp.transpose(k_st.reshape(t_chunk, t_chunk, nj, gl, c, c), (2, 0, 3, 5, 1, 4))
    k_mat = (k_base[:, :, :, :, :, None, :] * eye[None, None, :, None, None, :, None])
    k_mat = k_mat.reshape(nj, t_chunk * LANES, t_chunk * LANES)

    rev = t_chunk - 1 - jnp.arange(t_chunk)
    s_re = pr[rev][..., None] * bbar_re[None] - pi[rev][..., None] * bbar_im[None]
    s_im = pr[rev][..., None] * bbar_im[None] + pi[rev][..., None] * bbar_re[None]

    def inc_mat(x):
        base = jnp.transpose(x.reshape(t_chunk, nj, gl, p, c), (1, 0, 2, 4, 3))
        full = base[:, :, :, :, None, :] * eye[None, None, :, None, :, None]
        return full.reshape(nj, t_chunk * LANES, gl * p)

    w_in = jnp.concatenate([k_mat, inc_mat(s_re), inc_mat(s_im)], axis=-1).astype(BF16)

    def out_mat(x):
        base = jnp.transpose(x.reshape(t_chunk, nj, gl, c, p), (1, 2, 4, 0, 3))
        full = base[:, :, :, :, None, :] * eye[None, :, None, None, :, None]
        return full.reshape(nj, gl * p, t_chunk * LANES)

    w_state = jnp.concatenate([out_mat(ca_re[1:]), out_mat(-ca_im[1:])], axis=1).astype(BF16)

    qr, qi = apow(t_chunk * jnp.arange(1, 9))
    lanes = lambda x: jnp.transpose(x.reshape(8, nj, gl * p), (1, 0, 2))
    consts = jnp.concatenate([lanes(qr), lanes(qi)], axis=1)
    d_row = jnp.tile(d_skip.astype(F32).reshape(nj, 1, LANES), (1, 1, t_chunk))
    return w_in, w_state, consts, d_row


def _gelu_tanh(y):
    return 0.5 * y * (1.0 + jnp.tanh(math.sqrt(2.0 / math.pi) * (y + 0.044715 * (y * y * y))))


def _s5_kernel(x_ref, win_ref, wst_ref, cst_ref, d_ref, y_ref, stage, inc_re, inc_im, hp_re, hp_im,
               *, t_chunk, n_rows, tc, gp):
    stage[...] = x_ref[...].astype(F32)
    u = jnp.concatenate([stage[pl.ds(s, n_rows, stride=t_chunk), :] for s in range(t_chunk)], axis=1)
    z = _dot(u.astype(BF16), win_ref[0])
    inc_re[...] = z[:, tc:tc + gp]
    inc_im[...] = z[:, tc + gp:]
    cst = cst_ref[0]
    c_re, c_im = cst[0:8, :], cst[8:16, :]
    row = lax.broadcasted_iota(jnp.int32, (8, gp), 0)

    def body(i, carry):
        h_re, h_im = carry
        r0 = pl.multiple_of(i * 8, 8)
        x_re = inc_re[pl.ds(r0, 8), :]
        x_im = inc_im[pl.ds(r0, 8), :]
        for k in (1, 2, 4):
            keep = row >= k
            s_re = jnp.where(keep, pltpu.roll(x_re, k, 0), 0.0)
            s_im = jnp.where(keep, pltpu.roll(x_im, k, 0), 0.0)
            m_re, m_im = c_re[k - 1:k, :], c_im[k - 1:k, :]
            x_re, x_im = x_re + m_re * s_re - m_im * s_im, x_im + m_re * s_im + m_im * s_re
        o_re = x_re + c_re * h_re - c_im * h_im
        o_im = x_im + c_re * h_im + c_im * h_re
        first = row >= 1
        hp_re[pl.ds(r0, 8), :] = jnp.where(first, pltpu.roll(o_re, 1, 0), h_re)
        hp_im[pl.ds(r0, 8), :] = jnp.where(first, pltpu.roll(o_im, 1, 0), h_im)
        return o_re[7:8, :], o_im[7:8, :]

    zero = jnp.zeros((1, gp), F32)
    lax.fori_loop(0, n_rows // 8, body, (zero, zero))
    h_prev = jnp.concatenate([hp_re[...], hp_im[...]], axis=1).astype(BF16)
    y = _gelu_tanh(z[:, :tc] + _dot(h_prev, wst_ref[0]) + d_ref[0] * u)
    for t in range(t_chunk):
        stage[pl.ds(t, n_rows, stride=t_chunk), :] = y[:, t * LANES:(t + 1) * LANES]
    y_ref[...] = stage[...].astype(y_ref.dtype)


def _s5(proj, ssm_off, ssm_w, bsz, tables, *, t_chunk=S5_CHUNK):
    w_in, w_state, consts, d_row = tables
    m = proj.shape[0]
    seq = m // bsz
    nj = ssm_w // LANES
    n_rows = seq // t_chunk
    tc = t_chunk * LANES
    gp = w_state.shape[1] // 2
    jb0 = ssm_off // LANES
    return pl.pallas_call(
        functools.partial(_s5_kernel, t_chunk=t_chunk, n_rows=n_rows, tc=tc, gp=gp),
        grid=(nj, bsz),
        in_specs=[
            pl.BlockSpec((seq, LANES), lambda j, b: (b, jb0 + j)),
            pl.BlockSpec((1, tc, tc + 2 * gp), lambda j, b: (j, 0, 0)),
            pl.BlockSpec((1, 2 * gp, tc), lambda j, b: (j, 0, 0)),
            pl.BlockSpec((1, 16, gp), lambda j, b: (j, 0, 0)),
            pl.BlockSpec((1, 1, tc), lambda j, b: (j, 0, 0)),
        ],
        out_specs=pl.BlockSpec((seq, LANES), lambda j, b: (b, j)),
        out_shape=jax.ShapeDtypeStruct((m, ssm_w), BF16),
        scratch_shapes=[pltpu.VMEM((seq, LANES), F32)] + [pltpu.VMEM((n_rows, gp), F32) for _ in range(4)],
        compiler_params=_params("parallel", "parallel"),
        name="s5",
    )(proj, w_in, w_state, consts, d_row)


def _glu_kernel(x_ref, w_ref, b_ref, o_ref, *, n):
    z = _dot(x_ref[...], w_ref[...]) + b_ref[...]
    o_ref[...] = (z[:, :n] * jax.nn.sigmoid(z[:, n:])).astype(o_ref.dtype)


def _glu(x, w, b, *, tm=1024):
    m, k = x.shape
    n2 = w.shape[1]
    n = n2 // 2
    return pl.pallas_call(
        functools.partial(_glu_kernel, n=n),
        grid=(m // tm,),
        in_specs=[
            pl.BlockSpec((tm, k), lambda i: (i, 0)),
            pl.BlockSpec((k, n2), lambda i: (0, 0)),
            pl.BlockSpec((1, n2), lambda i: (0, 0)),
        ],
        out_specs=pl.BlockSpec((tm, n), lambda i: (i, 0)),
        out_shape=jax.ShapeDtypeStruct((m, n), BF16),
        compiler_params=_params("parallel"),
        name="glu",
    )(x, w, b.reshape(1, n2).astype(F32))


def _route_kernel(h_ref, g_ref, w_ref, b_ref, xp_ref, info_ref, *, n_groups, per_group):
    x = h_ref[...]
    xn = x * lax.rsqrt(jnp.mean(x * x, axis=-1, keepdims=True) + EPS) * g_ref[...]
    xp_ref[...] = _pack_halves(xn)
    n_exp = n_groups * per_group
    logits = _dot(xn.astype(BF16), w_ref[...]) + b_ref[...]
    lane = lax.broadcasted_iota(jnp.int32, logits.shape, 1).astype(F32)
    big = float(LANES)
    is_group = jnp.logical_and(lane >= n_exp, lane < n_exp + n_groups)
    gl = jnp.where(is_group, logits, -jnp.inf)
    gmax = jnp.max(gl, axis=-1, keepdims=True)
    gsel = jnp.min(jnp.where(gl == gmax, lane, big), axis=-1, keepdims=True) - n_exp
    gprob = 1.0 / jnp.sum(jnp.where(is_group, jnp.exp(logits - gmax), 0.0), axis=-1, keepdims=True)
    in_group = jnp.logical_and(lane >= gsel * per_group, lane < (gsel + 1.0) * per_group)
    el = jnp.where(in_group, logits, -jnp.inf)
    v1 = jnp.max(el, axis=-1, keepdims=True)
    i1 = jnp.min(jnp.where(el == v1, lane, big), axis=-1, keepdims=True)
    el2 = jnp.where(lane == i1, -jnp.inf, el)
    v2 = jnp.max(el2, axis=-1, keepdims=True)
    i2 = jnp.min(jnp.where(el2 == v2, lane, big), axis=-1, keepdims=True)
    e2 = jnp.exp(v2 - v1)
    w1 = gprob / (1.0 + e2)
    w2 = gprob * e2 / (1.0 + e2)
    info_ref[...] = (jnp.where(lane == 0.0, i1, 0.0) + jnp.where(lane == 1.0, i2, 0.0)
                     + jnp.where(lane == 2.0, w1, 0.0) + jnp.where(lane == 3.0, w2, 0.0))


def _route(h, gain, w_gr, b_gr, w_er, b_er, *, tm=512):
    m, d = h.shape
    n_groups, _, per_group = w_er.shape
    n_exp = n_groups * per_group
    assert n_exp + n_groups <= LANES
    w = jnp.concatenate([jnp.transpose(w_er, (1, 0, 2)).reshape(d, n_exp), w_gr], axis=1)
    w = jnp.pad(w, ((0, 0), (0, LANES - n_exp - n_groups))).astype(BF16)
    b = jnp.concatenate([b_er.reshape(n_exp), b_gr]).astype(F32)
    b = jnp.pad(b, (0, LANES - n_exp - n_groups)).reshape(1, LANES)
    return pl.pallas_call(
        functools.partial(_route_kernel, n_groups=n_groups, per_group=per_group),
        grid=(m // tm,),
        in_specs=[pl.BlockSpec((tm, d), lambda i: (i, 0)), pl.BlockSpec((1, d), lambda i: (0, 0)),
                  pl.BlockSpec((d, LANES), lambda i: (0, 0)), pl.BlockSpec((1, LANES), lambda i: (0, 0))],
        out_specs=[pl.BlockSpec((tm, d // 2), lambda i: (i, 0)), pl.BlockSpec((tm, LANES), lambda i: (i, 0))],
        out_shape=[jax.ShapeDtypeStruct((m, d // 2), U32), jax.ShapeDtypeStruct((m, LANES), F32)],
        compiler_params=_params("parallel"),
        name="moe_route",
    )(h, gain.reshape(1, d).astype(F32), w, b)


def _sorted_layout(expert_ids, n_exp, row_tile, n_tiles):
    e = expert_ids.reshape(-1)
    onehot = (e[:, None] == jnp.arange(n_exp, dtype=jnp.int32)[None, :]).astype(jnp.int32)
    csum = jnp.cumsum(onehot, axis=0)
    counts = csum[-1]
    rank = jnp.sum(onehot * csum, axis=1) - 1
    padded = ((counts + row_tile - 1) // row_tile) * row_tile
    ends = jnp.cumsum(padded)
    starts = ends - padded
    pos = starts[e] + rank
    tile_start = jnp.arange(n_tiles, dtype=jnp.int32) * row_tile
    tile_expert = jnp.minimum(jnp.sum((tile_start[:, None] >= ends[None, :]).astype(jnp.int32), axis=1), n_exp - 1)
    n_used = (ends[-1] // row_tile).astype(jnp.int32).reshape(1)
    return pos.astype(jnp.int32), tile_expert.astype(jnp.int32), n_used


def _row_copy(src_ref, src_row, dst_ref, dst_row, sem):
    return pltpu.make_async_copy(src_ref.at[pl.ds(src_row, 1)], dst_ref.at[pl.ds(dst_row, 1)], sem)


def _dispatch_kernel(pos_ref, x_ref, init_ref, out_ref, sem, *, tt, n_slots):
    del init_ref

    def start(t, c):
        for k in range(n_slots):
            _row_copy(x_ref, t, out_ref, pos_ref[0, 0, n_slots * t + k], sem).start()
        return c

    lax.fori_loop(0, tt, start, 0)

    def wait(t, c):
        _row_copy(x_ref, 0, out_ref, 0, sem).wait()
        return c

    lax.fori_loop(0, tt * n_slots, wait, 0)


def _dispatch(xp, pos, n_rows, *, tt=MOE_TOKEN_TILE, n_slots=TOP_K):
    m, d2 = xp.shape
    pos3 = pos.reshape(m // tt, 1, tt * n_slots)
    return pl.pallas_call(
        functools.partial(_dispatch_kernel, tt=tt, n_slots=n_slots),
        grid=(m // tt,),
        in_specs=[pl.BlockSpec((1, 1, tt * n_slots), lambda i: (i, 0, 0), memory_space=pltpu.SMEM),
                  pl.BlockSpec((tt, d2), lambda i: (i, 0)),
                  pl.BlockSpec(memory_space=pl.ANY)],
        out_specs=pl.BlockSpec(memory_space=pl.ANY),
        out_shape=jax.ShapeDtypeStruct((n_rows, d2), U32),
        scratch_shapes=[pltpu.SemaphoreType.DMA],
        input_output_aliases={2: 0},
        compiler_params=_params("arbitrary"),
        name="moe_dispatch",
    )(pos3, xp, jnp.zeros((n_rows, d2), U32))


def _expert_kernel(te_ref, nu_ref, x_ref, wg_ref, wu_ref, wd_ref, o_ref):
    del te_ref
    t = pl.program_id(0)

    @pl.when(t < nu_ref[0])
    def _():
        lo, hi = _unpack_halves(x_ref[...])
        x = jnp.concatenate([lo, hi], axis=1).astype(BF16)
        g = _dot(x, wg_ref[...])
        u = _dot(x, wu_ref[...])
        hid = (g * jax.nn.sigmoid(g) * u).astype(BF16)
        o_ref[...] = _pack_halves(_dot(hid, wd_ref[...]))

    @pl.when(t >= nu_ref[0])
    def _():
        o_ref[...] = jnp.zeros_like(o_ref)


def _expert_ffn(xs, tile_expert, n_used, w_gate, w_up, w_down, *, tm=MOE_ROW_TILE):
    rows, d2 = xs.shape
    _, d, ff = w_gate.shape
    grid_spec = pltpu.PrefetchScalarGridSpec(
        num_scalar_prefetch=2,
        grid=(rows // tm,),
        in_specs=[pl.BlockSpec((tm, d2), lambda t, te, nu: (t, 0)),
                  pl.BlockSpec((None, d, ff), lambda t, te, nu: (te[t], 0, 0)),
                  pl.BlockSpec((None, d, ff), lambda t, te, nu: (te[t], 0, 0)),
                  pl.BlockSpec((None, ff, d), lambda t, te, nu: (te[t], 0, 0))],
        out_specs=pl.BlockSpec((tm, d2), lambda t, te, nu: (t, 0)),
    )
    return pl.pallas_call(
        _expert_kernel,
        grid_spec=grid_spec,
        out_shape=jax.ShapeDtypeStruct((rows, d2), U32),
        compiler_params=_params("arbitrary"),
        name="moe_expert_ffn",
    )(tile_expert, n_used, xs, w_gate, w_up, w_down)


def _combine_kernel(pos_ref, info_ref, h_ref, y_ref, o_ref, buf, sem, *, tt, n_slots):
    def start(t, c):
        for k in range(n_slots):
            _row_copy(y_ref, pos_ref[0, 0, n_slots * t + k], buf.at[k], t, sem).start()
        return c

    lax.fori_loop(0, tt, start, 0)

    def wait(t, c):
        _row_copy(y_ref, 0, buf.at[0], 0, sem).wait()
        return c

    lax.fori_loop(0, tt * n_slots, wait, 0)
    info = info_ref[...]
    lane = lax.broadcasted_iota(jnp.int32, info.shape, 1)
    lo_acc = hi_acc = None
    for k in range(n_slots):
        wk = jnp.sum(jnp.where(lane == n_slots + k, info, 0.0), axis=-1, keepdims=True)
        lo, hi = _unpack_halves(buf[k])
        lo_acc = wk * lo if lo_acc is None else lo_acc + wk * lo
        hi_acc = wk * hi if hi_acc is None else hi_acc + wk * hi
    o_ref[...] = h_ref[...] + jnp.concatenate([lo_acc, hi_acc], axis=1)


def _combine(h, info, ys, pos, *, tt=MOE_TOKEN_TILE, n_slots=TOP_K):
    m, d = h.shape
    d2 = ys.shape[1]
    pos3 = pos.reshape(m // tt, 1, tt * n_slots)
    return pl.pallas_call(
        functools.partial(_combine_kernel, tt=tt, n_slots=n_slots),
        grid=(m // tt,),
        in_specs=[pl.BlockSpec((1, 1, tt * n_slots), lambda i: (i, 0, 0), memory_space=pltpu.SMEM),
                  pl.BlockSpec((tt, LANES), lambda i: (i, 0)),
                  pl.BlockSpec((tt, d), lambda i: (i, 0)),
                  pl.BlockSpec(memory_space=pl.ANY)],
        out_specs=pl.BlockSpec((tt, d), lambda i: (i, 0)),
        out_shape=jax.ShapeDtypeStruct((m, d), F32),
        scratch_shapes=[pltpu.VMEM((n_slots, tt, d2), U32), pltpu.SemaphoreType.DMA],
        input_output_aliases={2: 0},
        compiler_params=_params("arbitrary"),
        name="moe_combine",
    )(pos3, info, h, ys)


def _moe(h, gain, w_gr, b_gr, w_er, b_er, w_gate, w_up, w_down):
    m = h.shape[0]
    n_exp = w_gate.shape[0]
    n_tiles = -(-(m * TOP_K + n_exp * (MOE_ROW_TILE - 1)) // MOE_ROW_TILE)
    xp, info = _route(h, gain, w_gr, b_gr, w_er, b_er)
    expert_ids = info[:, :TOP_K].astype(jnp.int32)
    pos, tile_expert, n_used = _sorted_layout(expert_ids, n_exp, MOE_ROW_TILE, n_tiles)
    xs = _dispatch(xp, pos, n_tiles * MOE_ROW_TILE)
    ys = _expert_ffn(xs, tile_expert, n_used, w_gate, w_up, w_down)
    return _combine(h, info, ys, pos)


def kernel(x, p, mix_gain, w_in, conv_w, conv_b, w_rgate, b_rgate, w_igate, b_igate, lru_lambda, q_gain, k_gain, rel_bias, ssm_a_re, ssm_a_im, ssm_log_dt, ssm_b_re, ssm_b_im, ssm_c_re, ssm_c_im, ssm_d, w_glu, b_glu, w_proj_lru, w_proj_att, w_proj_ssm, w_out, ffn_gain, w_group_router, b_group_router, w_expert_router, b_expert_router, w_up, w_gate, w_down, ple_gain, w_ple, w_ple_gate):
    bsz, seq, d = x.shape
    depth = w_in.shape[0]
    m = bsz * seq
    lru_w = w_proj_lru.shape[1]
    att_w = w_proj_att.shape[1]
    ssm_w = w_proj_ssm.shape[1]
    n_heads = att_w // HEAD_DIM
    q_off, k_off, v_off = lru_w, lru_w + att_w, lru_w + 2 * att_w
    ssm_off = lru_w + 3 * att_w
    gate_off = ssm_off + ssm_w
    in_width = w_in.shape[2]

    h = x.reshape(m, d).astype(F32)
    for i in range(depth):
        xn = _rmsnorm(h, mix_gain[i])
        proj = _matmul(xn, w_in[i].astype(BF16))
        proj3 = proj.reshape(bsz, seq, in_width)
        y_lru = _rglru(proj3, conv_w[i], conv_b[i], w_rgate[i], b_rgate[i], w_igate[i], b_igate[i], lru_lambda[i])
        y_att = _attention(proj3, q_off, k_off, v_off, n_heads, q_gain[i], k_gain[i], rel_bias[i])
        tables = _s5_tables(ssm_a_re[i], ssm_a_im[i], ssm_log_dt[i], ssm_b_re[i], ssm_b_im[i],
                            ssm_c_re[i], ssm_c_im[i], ssm_d[i], S5_CHUNK)
        y_ssm = _glu(_s5(proj, ssm_off, ssm_w, bsz, tables), w_glu[i].astype(BF16), b_glu[i])
        merged = _gated_merge(y_lru.reshape(m, lru_w), y_att.reshape(m, att_w), y_ssm,
                              w_proj_lru[i].astype(BF16), w_proj_att[i].astype(BF16),
                              w_proj_ssm[i].astype(BF16), proj, gate_off)
        h = _matmul_residual(h, merged, w_out[i].astype(BF16))
        h = _moe(h, ffn_gain[i], w_group_router[i], b_group_router[i], w_expert_router[i], b_expert_router[i],
                 w_gate[i].astype(BF16), w_up[i].astype(BF16), w_down[i].astype(BF16))
        h = _ple(h, _rmsnorm(h, ple_gain[i]), w_ple_gate[i].astype(BF16),
                 p[i].reshape(m, -1).astype(BF16), w_ple[i].astype(BF16))
    return h.reshape(bsz, seq, d)
```

```python
import functools
import math

import jax
import jax.numpy as jnp
from jax import lax
from jax.experimental import pallas as pl
from jax.experimental.pallas import tpu as pltpu

F32 = jnp.float32
BF16 = jnp.bfloat16
U32 = jnp.uint32

CHUNK = 64
LEFT_CHUNKS = 8
REL_CLIP = 128
HEAD_DIM = 128
LRU_C = 8.0
TOP_K = 2
EPS = 1e-6
NEG_INF = -1e30

LANES = 128
S5_CHUNK = 8
ATTN_QBLOCK = LEFT_CHUNKS * CHUNK
MOE_ROW_TILE = 256
MOE_TOKEN_TILE = 256
ROW_DMA_UNROLL = 8
V7X_VMEM_LIMIT_BYTES = 56 * 1024 * 1024
HI16 = 0xFFFF0000


def _params(*semantics):
    return pltpu.CompilerParams(dimension_semantics=semantics, vmem_limit_bytes=V7X_VMEM_LIMIT_BYTES)


def _dot(a, b):
    return jnp.dot(a, b, preferred_element_type=F32)


def _pack_halves(x):
    n = x.shape[1] // 2
    lo = pltpu.bitcast(x[:, :n].astype(BF16).astype(F32), U32)
    hi = pltpu.bitcast(x[:, n:].astype(BF16).astype(F32), U32)
    return (lo >> 16) | (hi & jnp.uint32(HI16))


def _unpack_halves(u):
    return pltpu.bitcast(u << 16, F32), pltpu.bitcast(u & jnp.uint32(HI16), F32)


def _rmsnorm_kernel(x_ref, g_ref, o_ref):
    x = x_ref[...].astype(F32)
    ms = jnp.mean(x * x, axis=-1, keepdims=True)
    o_ref[...] = (x * lax.rsqrt(ms + EPS) * g_ref[...]).astype(o_ref.dtype)


def _rmsnorm(x, gain, *, tm=512):
    m, d = x.shape
    return pl.pallas_call(
        _rmsnorm_kernel,
        grid=(m // tm,),
        in_specs=[pl.BlockSpec((tm, d), lambda i: (i, 0)), pl.BlockSpec((1, d), lambda i: (0, 0))],
        out_specs=pl.BlockSpec((tm, d), lambda i: (i, 0)),
        out_shape=jax.ShapeDtypeStruct((m, d), BF16),
        compiler_params=_params("parallel"),
        name="rmsnorm",
    )(x, gain.reshape(1, d).astype(F32))


def _mm_kernel(x_ref, w_ref, o_ref):
    o_ref[...] = _dot(x_ref[...], w_ref[...]).astype(o_ref.dtype)


def _layer_spec(layer, block, index_map):
    return pl.BlockSpec((None,) + tuple(block), lambda *a: (layer,) + tuple(index_map(*a)))


def _matmul(x, w, layer, *, tm=1024, tn=1024, out_dtype=BF16):
    m, k = x.shape
    n = w.shape[2]
    return pl.pallas_call(
        _mm_kernel,
        grid=(m // tm, n // tn),
        in_specs=[pl.BlockSpec((tm, k), lambda i, j: (i, 0)), _layer_spec(layer, (k, tn), lambda i, j: (0, j))],
        out_specs=pl.BlockSpec((tm, tn), lambda i, j: (i, j)),
        out_shape=jax.ShapeDtypeStruct((m, n), out_dtype),
        compiler_params=_params("parallel", "arbitrary"),
        name="matmul",
    )(x, w)


def _mm_residual_kernel(h_ref, x_ref, w_ref, o_ref):
    o_ref[...] = h_ref[...] + _dot(x_ref[...], w_ref[...])


def _matmul_residual(h, x, w, layer, *, in_place, tm=1024, tn=512):
    m, k = x.shape
    n = w.shape[2]
    return pl.pallas_call(
        _mm_residual_kernel,
        grid=(m // tm, n // tn),
        in_specs=[
            pl.BlockSpec((tm, tn), lambda i, j: (i, j)),
            pl.BlockSpec((tm, k), lambda i, j: (i, 0)),
            _layer_spec(layer, (k, tn), lambda i, j: (0, j)),
        ],
        out_specs=pl.BlockSpec((tm, tn), lambda i, j: (i, j)),
        out_shape=jax.ShapeDtypeStruct((m, n), F32),
        input_output_aliases={0: 0} if in_place else {},
        compiler_params=_params("parallel", "arbitrary"),
        name="matmul_residual",
    )(h, x, w)


def _ple_kernel(h_ref, x_ref, wg_ref, p_ref, we_ref, o_ref):
    gate = jax.nn.sigmoid(_dot(x_ref[...], wg_ref[...]))
    emb = _dot(p_ref[...], we_ref[...])
    o_ref[...] = h_ref[...] + gate * emb


def _ple(h, xn, w_gate, p, w_ple, layer, *, tm=1024, tn=512):
    m, k = xn.shape
    n = w_gate.shape[2]
    kp = p.shape[2]
    return pl.pallas_call(
        _ple_kernel,
        grid=(m // tm, n // tn),
        in_specs=[
            pl.BlockSpec((tm, tn), lambda i, j: (i, j)),
            pl.BlockSpec((tm, k), lambda i, j: (i, 0)),
            _layer_spec(layer, (k, tn), lambda i, j: (0, j)),
            _layer_spec(layer, (tm, kp), lambda i, j: (i, 0)),
            _layer_spec(layer, (kp, tn), lambda i, j: (0, j)),
        ],
        out_specs=pl.BlockSpec((tm, tn), lambda i, j: (i, j)),
        out_shape=jax.ShapeDtypeStruct((m, n), F32),
        input_output_aliases={0: 0},
        compiler_params=_params("parallel", "arbitrary"),
        name="ple",
    )(h, xn, w_gate, p, w_ple)


def _merge_kernel(yl_ref, ya_ref, ys_ref, pl_ref, pa_ref, ps_ref, gl_ref, ga_ref, gs_ref, o_ref):
    acc = jax.nn.sigmoid(gl_ref[...].astype(F32)) * _dot(yl_ref[...], pl_ref[...])
    acc += jax.nn.sigmoid(ga_ref[...].astype(F32)) * _dot(ya_ref[...], pa_ref[...])
    acc += jax.nn.sigmoid(gs_ref[...].astype(F32)) * _dot(ys_ref[...], ps_ref[...])
    o_ref[...] = acc.astype(o_ref.dtype)


def _gated_merge(y_lru, y_att, y_ssm, p_lru, p_att, p_ssm, layer, proj, gate_off, *, tm=1024, tn=512):
    m = y_lru.shape[0]
    d = p_lru.shape[2]
    goff = gate_off // tn
    nd = d // tn

    def y_spec(y):
        return pl.BlockSpec((tm, y.shape[1]), lambda i, j: (i, 0))

    def p_spec(p):
        return _layer_spec(layer, (p.shape[1], tn), lambda i, j: (0, j))

    def g_spec(b):
        return pl.BlockSpec((tm, tn), lambda i, j: (i, goff + b * nd + j))

    return pl.pallas_call(
        _merge_kernel,
        grid=(m // tm, nd),
        in_specs=[y_spec(y_lru), y_spec(y_att), y_spec(y_ssm), p_spec(p_lru), p_spec(p_att), p_spec(p_ssm),
                  g_spec(0), g_spec(1), g_spec(2)],
        out_specs=pl.BlockSpec((tm, tn), lambda i, j: (i, j)),
        out_shape=jax.ShapeDtypeStruct((m, d), BF16),
        compiler_params=_params("parallel", "arbitrary"),
        name="gated_merge",
    )(y_lru, y_att, y_ssm, p_lru, p_att, p_ssm, proj, proj, proj)


def _scan_rows8(a8, x8, hprev, row):
    for k in (1, 2, 4):
        keep = row >= k
        a_sh = jnp.where(keep, pltpu.roll(a8, k, 0), 1.0)
        x_sh = jnp.where(keep, pltpu.roll(x8, k, 0), 0.0)
        x8 = a8 * x_sh + x8
        a8 = a8 * a_sh
    return a8 * hprev + x8


def _lru_kernel(u_ref, cw_ref, cb_ref, wr_ref, br_ref, wi_ref, bi_ref, lam_ref, y_ref,
                tail_ref, h_ref, a_s, x_s, *, t_rows, n_blocks, block, conv_width):
    @pl.when(pl.program_id(1) == 0)
    def _():
        tail_ref[...] = jnp.zeros_like(tail_ref)
        h_ref[...] = jnp.zeros_like(h_ref)

    u = u_ref[0].astype(F32)
    ue = jnp.concatenate([tail_ref[...], u], axis=0)
    xc = cb_ref[...]
    for j in range(conv_width):
        off = 8 - (conv_width - 1) + j
        xc = xc + cw_ref[j:j + 1, :] * ue[off:off + t_rows, :]
    tail_ref[...] = u[t_rows - 8:, :]

    xcb = xc.astype(BF16)
    rs, igs = [], []
    for hb in range(n_blocks):
        xb = xcb[:, hb * block:(hb + 1) * block]
        rs.append(_dot(xb, wr_ref[hb]))
        igs.append(_dot(xb, wi_ref[hb]))
    r = jax.nn.sigmoid(jnp.concatenate(rs, axis=1) + br_ref[...])
    ig = jax.nn.sigmoid(jnp.concatenate(igs, axis=1) + bi_ref[...])
    lam = lam_ref[...]
    softplus_neg_lam = jnp.maximum(-lam, 0.0) + jnp.log1p(jnp.exp(-jnp.abs(lam)))
    log_a = (-LRU_C) * r * softplus_neg_lam
    a = jnp.exp(log_a)
    a_s[...] = a
    x_s[...] = jnp.sqrt(-jnp.tanh(log_a) * (a * a + 1.0)) * (ig * xc)

    w = a_s.shape[1]
    row = lax.broadcasted_iota(jnp.int32, (8, w), 0)

    def body(i, hprev):
        r0 = pl.multiple_of(i * 16, 16)
        h_a = _scan_rows8(a_s[pl.ds(r0, 8), :], x_s[pl.ds(r0, 8), :], hprev, row)
        h_b = _scan_rows8(a_s[pl.ds(r0 + 8, 8), :], x_s[pl.ds(r0 + 8, 8), :], h_a[7:8, :], row)
        y_ref[0, pl.ds(r0, 16), :] = jnp.concatenate([h_a, h_b], axis=0).astype(y_ref.dtype)
        return h_b[7:8, :]

    h_ref[...] = lax.fori_loop(0, t_rows // 16, body, h_ref[...])


def _rglru(proj3, conv_w, conv_b, w_rg, b_rg, w_ig, b_ig, lam, *, t_rows=256):
    bsz, seq, _ = proj3.shape
    n_blocks, block, _ = w_rg.shape
    w = n_blocks * block
    cw = conv_w.shape[0]
    vec = lambda v: v.reshape(1, w).astype(F32)
    full2 = lambda r, c: pl.BlockSpec((r, c), lambda b, t: (0, 0))
    full3 = pl.BlockSpec((n_blocks, block, block), lambda b, t: (0, 0, 0))
    return pl.pallas_call(
        functools.partial(_lru_kernel, t_rows=t_rows, n_blocks=n_blocks, block=block, conv_width=cw),
        grid=(bsz, seq // t_rows),
        in_specs=[pl.BlockSpec((1, t_rows, w), lambda b, t: (b, t, 0)), full2(cw, w), full2(1, w),
                  full3, full2(1, w), full3, full2(1, w), full2(1, w)],
        out_specs=pl.BlockSpec((1, t_rows, w), lambda b, t: (b, t, 0)),
        out_shape=jax.ShapeDtypeStruct((bsz, seq, w), BF16),
        scratch_shapes=[pltpu.VMEM((8, w), F32), pltpu.VMEM((1, w), F32),
                        pltpu.VMEM((t_rows, w), F32), pltpu.VMEM((t_rows, w), F32)],
        compiler_params=_params("parallel", "arbitrary"),
        name="rglru",
    )(proj3, conv_w.astype(F32), vec(conv_b), w_rg.astype(BF16), vec(b_rg), w_ig.astype(BF16), vec(b_ig),
      vec(lam))


def _head_rms(x, gain):
    x = x.astype(F32)
    return x * lax.rsqrt(jnp.mean(x * x, axis=-1, keepdims=True) + EPS) * gain


def _attn_kernel(q_ref, kp_ref, kc_ref, vp_ref, vc_ref, bias_ref, qg_ref, kg_ref, o_ref, *, qb):
    qn = (_head_rms(q_ref[0], qg_ref[...]) * (HEAD_DIM ** -0.5)).astype(BF16)
    kn = _head_rms(jnp.concatenate([kp_ref[0], kc_ref[0]], axis=0), kg_ref[...]).astype(BF16)
    s = lax.dot_general(qn, kn, (((1,), (1,)), ((), ())), preferred_element_type=F32)
    s = s + bias_ref[0]
    col = lax.broadcasted_iota(jnp.int32, s.shape, 1)
    first = pl.program_id(2) == 0
    s = jnp.where(jnp.logical_and(first, col < qb), NEG_INF, s)
    p = jnp.exp(s - jnp.max(s, axis=-1, keepdims=True))
    denom = jnp.sum(p, axis=-1, keepdims=True)
    v = jnp.concatenate([vp_ref[0], vc_ref[0]], axis=0)
    o_ref[0] = (_dot(p.astype(BF16), v) / denom).astype(o_ref.dtype)


def _band_bias(rel_bias, qb):
    period = 3 * qb
    mm = jnp.arange(period)
    j_minus_i = jnp.where(mm < 2 * qb, mm, mm - period)
    dist = qb - j_minus_i
    vec = rel_bias.astype(F32)[:, jnp.clip(dist, -REL_CLIP, REL_CLIP) + REL_CLIP]
    n_heads = vec.shape[0]
    table = jnp.tile(vec, (1, qb))[:, :qb * (period - 1)].reshape(n_heads, qb, period - 1)[:, :, :2 * qb]
    i = jnp.arange(qb)[:, None]
    j = jnp.arange(2 * qb)[None, :]
    q_chunk = qb // CHUNK + i // CHUNK
    k_chunk = j // CHUNK
    in_band = jnp.logical_and(k_chunk >= q_chunk - LEFT_CHUNKS, k_chunk <= q_chunk)
    return jnp.where(in_band[None], table, NEG_INF)


def _attention(proj3, q_off, k_off, v_off, n_heads, q_gain, k_gain, rel_bias, *, qb=ATTN_QBLOCK):
    assert qb % CHUNK == 0 and qb >= LEFT_CHUNKS * CHUNK
    bsz, seq, _ = proj3.shape
    qo, ko, vo = q_off // HEAD_DIM, k_off // HEAD_DIM, v_off // HEAD_DIM
    cur = lambda off: pl.BlockSpec((1, qb, HEAD_DIM), lambda b, h, n: (b, n, off + h))
    prev = lambda off: pl.BlockSpec((1, qb, HEAD_DIM), lambda b, h, n: (b, jnp.maximum(n - 1, 0), off + h))
    gain = pl.BlockSpec((1, HEAD_DIM), lambda b, h, n: (0, 0))
    return pl.pallas_call(
        functools.partial(_attn_kernel, qb=qb),
        grid=(bsz, n_heads, seq // qb),
        in_specs=[cur(qo), prev(ko), cur(ko), prev(vo), cur(vo),
                  pl.BlockSpec((1, qb, 2 * qb), lambda b, h, n: (h, 0, 0)), gain, gain],
        out_specs=pl.BlockSpec((1, qb, HEAD_DIM), lambda b, h, n: (b, n, h)),
        out_shape=jax.ShapeDtypeStruct((bsz, seq, n_heads * HEAD_DIM), BF16),
        compiler_params=_params("parallel", "parallel", "arbitrary"),
        name="band_attention",
    )(proj3, proj3, proj3, proj3, proj3, _band_bias(rel_bias, qb),
      q_gain.reshape(1, HEAD_DIM).astype(F32), k_gain.reshape(1, HEAD_DIM).astype(F32))


def _s5_tables(a_re, a_im, log_dt, b_re, b_im, c_re, c_im, d_skip, t_chunk):
    hi = lax.Precision.HIGHEST
    g, p = a_re.shape
    c = b_re.shape[-1]
    gl = LANES // c
    nj = g // gl
    dt = jnp.exp(log_dt.astype(F32))[:, None]
    ar, ai = a_re.astype(F32), a_im.astype(F32)

    def apow(tau):
        tau = jnp.asarray(tau, F32)[:, None, None]
        mag = jnp.exp(tau * dt * ar)
        return mag * jnp.cos(tau * dt * ai), mag * jnp.sin(tau * dt * ai)

    pr, pi = apow(jnp.arange(t_chunk + 1))
    abar_re, abar_im = pr[1], pi[1]
    den = ar * ar + ai * ai
    nr, ni = abar_re - 1.0, abar_im
    coef_re = (nr * ar + ni * ai) / den
    coef_im = (ni * ar - nr * ai) / den
    br, bi = b_re.astype(F32), b_im.astype(F32)
    bbar_re = coef_re[..., None] * br - coef_im[..., None] * bi
    bbar_im = coef_re[..., None] * bi + coef_im[..., None] * br
    cr, ci = c_re.astype(F32), c_im.astype(F32)
    ca_re = cr[None] * pr[:, :, None, :] - ci[None] * pi[:, :, None, :]
    ca_im = cr[None] * pi[:, :, None, :] + ci[None] * pr[:, :, None, :]
    lane = jnp.arange(LANES)
    col = jnp.arange(gl * p)
    rep_c = (lane[None, :] % c == jnp.arange(c)[:, None]).astype(BF16)
    rep_p = (col[None, :] % p == jnp.arange(p)[:, None]).astype(BF16)
    lane_lane = lane[:, None] // c == lane[None, :] // c
    lane_col = lane[:, None] // c == col[None, :] // p
    col_lane = col[:, None] // p == lane[None, :] // c

    k_tap = (jnp.einsum("tgcp,gpd->tgcd", ca_re[:t_chunk], bbar_re, precision=hi)
             - jnp.einsum("tgcp,gpd->tgcd", ca_im[:t_chunk], bbar_im, precision=hi))
    lag = jnp.arange(t_chunk)[None, :] - jnp.arange(t_chunk)[:, None]
    k_st = jnp.where((lag >= 0)[:, :, None, None, None], k_tap[jnp.maximum(lag, 0)], 0.0)
    k_rows = jnp.transpose(k_st.reshape(t_chunk, t_chunk, nj, gl, c, c), (2, 0, 1, 3, 5, 4))
    k_rep = jnp.dot(k_rows.reshape(-1, c).astype(BF16), rep_c).reshape(nj, t_chunk, t_chunk, LANES, LANES)
    k_rep = jnp.where(lane_lane[None, None, None], k_rep, 0)
    k_mat = jnp.transpose(k_rep, (0, 1, 3, 2, 4)).reshape(nj, t_chunk * LANES, t_chunk * LANES)

    rev = t_chunk - 1 - jnp.arange(t_chunk)
    s_re = pr[rev][..., None] * bbar_re[None] - pi[rev][..., None] * bbar_im[None]
    s_im = pr[rev][..., None] * bbar_im[None] + pi[rev][..., None] * bbar_re[None]

    def inc_mat(x):
        rows = jnp.transpose(x.reshape(t_chunk, nj, gl, p, c), (1, 0, 2, 4, 3))
        rep = jnp.dot(rows.reshape(-1, p).astype(BF16), rep_p).reshape(nj, t_chunk, LANES, gl * p)
        return jnp.where(lane_col[None, None], rep, 0).reshape(nj, t_chunk * LANES, gl * p)

    w_in = jnp.concatenate([k_mat, inc_mat(s_re), inc_mat(s_im)], axis=-1)

    def out_mat(x):
        rows = jnp.transpose(x.reshape(t_chunk, nj, gl, c, p), (1, 2, 4, 0, 3))
        rep = jnp.dot(rows.reshape(-1, c).astype(BF16), rep_c).reshape(nj, gl * p, t_chunk, LANES)
        return jnp.where(col_lane[None, :, None, :], rep, 0).reshape(nj, gl * p, t_chunk * LANES)

    w_state = jnp.concatenate([out_mat(ca_re[1:]), out_mat(-ca_im[1:])], axis=1)

    qr, qi = apow(t_chunk * jnp.arange(1, 9))
    lanes = lambda x: jnp.transpose(x.reshape(8, nj, gl * p), (1, 0, 2))
    consts = jnp.concatenate([lanes(qr), lanes(qi)], axis=1)
    d_row = jnp.tile(d_skip.astype(F32).reshape(nj, 1, LANES), (1, 1, t_chunk))
    return w_in, w_state, consts, d_row


def _gelu_tanh(y):
    return 0.5 * y * (1.0 + jnp.tanh(math.sqrt(2.0 / math.pi) * (y + 0.044715 * (y * y * y))))


def _s5_kernel(x_ref, win_ref, wst_ref, cst_ref, d_ref, y_ref, stage, inc_re, inc_im, hp_re, hp_im,
               *, t_chunk, n_rows, tc, gp):
    stage[...] = x_ref[...].astype(F32)
    u = jnp.concatenate([stage[pl.ds(s, n_rows, stride=t_chunk), :] for s in range(t_chunk)], axis=1)
    z = _dot(u.astype(BF16), win_ref[0])
    inc_re[...] = z[:, tc:tc + gp]
    inc_im[...] = z[:, tc + gp:]
    cst = cst_ref[0]
    c_re, c_im = cst[0:8, :], cst[8:16, :]
    row = lax.broadcasted_iota(jnp.int32, (8, gp), 0)

    def body(i, carry):
        h_re, h_im = carry
        r0 = pl.multiple_of(i * 8, 8)
        x_re = inc_re[pl.ds(r0, 8), :]
        x_im = inc_im[pl.ds(r0, 8), :]
        for k in (1, 2, 4):
            keep = row >= k
            s_re = jnp.where(keep, pltpu.roll(x_re, k, 0), 0.0)
            s_im = jnp.where(keep, pltpu.roll(x_im, k, 0), 0.0)
            m_re, m_im = c_re[k - 1:k, :], c_im[k - 1:k, :]
            x_re, x_im = x_re + m_re * s_re - m_im * s_im, x_im + m_re * s_im + m_im * s_re
        o_re = x_re + c_re * h_re - c_im * h_im
        o_im = x_im + c_re * h_im + c_im * h_re
        first = row >= 1
        hp_re[pl.ds(r0, 8), :] = jnp.where(first, pltpu.roll(o_re, 1, 0), h_re)
        hp_im[pl.ds(r0, 8), :] = jnp.where(first, pltpu.roll(o_im, 1, 0), h_im)
        return o_re[7:8, :], o_im[7:8, :]

    zero = jnp.zeros((1, gp), F32)
    lax.fori_loop(0, n_rows // 8, body, (zero, zero))
    h_prev = jnp.concatenate([hp_re[...], hp_im[...]], axis=1).astype(BF16)
    y = _gelu_tanh(z[:, :tc] + _dot(h_prev, wst_ref[0]) + d_ref[0] * u)
    for t in range(t_chunk):
        stage[pl.ds(t, n_rows, stride=t_chunk), :] = y[:, t * LANES:(t + 1) * LANES]
    y_ref[...] = stage[...].astype(y_ref.dtype)


def _s5(proj, ssm_off, ssm_w, bsz, tables, *, t_chunk=S5_CHUNK):
    w_in, w_state, consts, d_row = tables
    m = proj.shape[0]
    seq = m // bsz
    nj = ssm_w // LANES
    n_rows = seq // t_chunk
    tc = t_chunk * LANES
    gp = w_state.shape[1] // 2
    jb0 = ssm_off // LANES
    return pl.pallas_call(
        functools.partial(_s5_kernel, t_chunk=t_chunk, n_rows=n_rows, tc=tc, gp=gp),
        grid=(nj, bsz),
        in_specs=[
            pl.BlockSpec((seq, LANES), lambda j, b: (b, jb0 + j)),
            pl.BlockSpec((1, tc, tc + 2 * gp), lambda j, b: (j, 0, 0)),
            pl.BlockSpec((1, 2 * gp, tc), lambda j, b: (j, 0, 0)),
            pl.BlockSpec((1, 16, gp), lambda j, b: (j, 0, 0)),
            pl.BlockSpec((1, 1, tc), lambda j, b: (j, 0, 0)),
        ],
        out_specs=pl.BlockSpec((seq, LANES), lambda j, b: (b, j)),
        out_shape=jax.ShapeDtypeStruct((m, ssm_w), BF16),
        scratch_shapes=[pltpu.VMEM((seq, LANES), F32)] + [pltpu.VMEM((n_rows, gp), F32) for _ in range(4)],
        compiler_params=_params("parallel", "parallel"),
        name="s5",
    )(proj, w_in, w_state, consts, d_row)


def _glu_kernel(x_ref, w_ref, b_ref, o_ref, *, n):
    z = _dot(x_ref[...], w_ref[...]) + b_ref[...]
    o_ref[...] = (z[:, :n] * jax.nn.sigmoid(z[:, n:])).astype(o_ref.dtype)


def _glu(x, w, layer, b, *, tm=1024):
    m, k = x.shape
    n2 = w.shape[2]
    n = n2 // 2
    return pl.pallas_call(
        functools.partial(_glu_kernel, n=n),
        grid=(m // tm,),
        in_specs=[
            pl.BlockSpec((tm, k), lambda i: (i, 0)),
            _layer_spec(layer, (k, n2), lambda i: (0, 0)),
            pl.BlockSpec((1, n2), lambda i: (0, 0)),
        ],
        out_specs=pl.BlockSpec((tm, n), lambda i: (i, 0)),
        out_shape=jax.ShapeDtypeStruct((m, n), BF16),
        compiler_params=_params("parallel"),
        name="glu",
    )(x, w, b.reshape(1, n2).astype(F32))


def _route_kernel(h_ref, g_ref, w_ref, b_ref, xp_ref, info_ref, *, n_groups, per_group):
    x = h_ref[...]
    xn = x * lax.rsqrt(jnp.mean(x * x, axis=-1, keepdims=True) + EPS) * g_ref[...]
    xp_ref[...] = _pack_halves(xn)
    n_exp = n_groups * per_group
    logits = _dot(xn.astype(BF16), w_ref[...]) + b_ref[...]
    lane = lax.broadcasted_iota(jnp.int32, logits.shape, 1).astype(F32)
    big = float(LANES)
    is_group = jnp.logical_and(lane >= n_exp, lane < n_exp + n_groups)
    gl = jnp.where(is_group, logits, -jnp.inf)
    gmax = jnp.max(gl, axis=-1, keepdims=True)
    gsel = jnp.min(jnp.where(gl == gmax, lane, big), axis=-1, keepdims=True) - n_exp
    gprob = 1.0 / jnp.sum(jnp.where(is_group, jnp.exp(logits - gmax), 0.0), axis=-1, keepdims=True)
    in_group = jnp.logical_and(lane >= gsel * per_group, lane < (gsel + 1.0) * per_group)
    el = jnp.where(in_group, logits, -jnp.inf)
    v1 = jnp.max(el, axis=-1, keepdims=True)
    i1 = jnp.min(jnp.where(el == v1, lane, big), axis=-1, keepdims=True)
    el2 = jnp.where(lane == i1, -jnp.inf, el)
    v2 = jnp.max(el2, axis=-1, keepdims=True)
    i2 = jnp.min(jnp.where(el2 == v2, lane, big), axis=-1, keepdims=True)
    e2 = jnp.exp(v2 - v1)
    w1 = gprob / (1.0 + e2)
    w2 = gprob * e2 / (1.0 + e2)
    info_ref[...] = (jnp.where(lane == 0.0, i1, 0.0) + jnp.where(lane == 1.0, i2, 0.0)
                     + jnp.where(lane == 2.0, w1, 0.0) + jnp.where(lane == 3.0, w2, 0.0))


def _route(h, gain, w_gr, b_gr, w_er, b_er, *, tm=512):
    m, d = h.shape
    n_groups, _, per_group = w_er.shape
    n_exp = n_groups * per_group
    assert n_exp + n_groups <= LANES
    w = jnp.concatenate([jnp.transpose(w_er, (1, 0, 2)).reshape(d, n_exp), w_gr], axis=1)
    w = jnp.pad(w, ((0, 0), (0, LANES - n_exp - n_groups))).astype(BF16)
    b = jnp.concatenate([b_er.reshape(n_exp), b_gr]).astype(F32)
    b = jnp.pad(b, (0, LANES - n_exp - n_groups)).reshape(1, LANES)
    return pl.pallas_call(
        functools.partial(_route_kernel, n_groups=n_groups, per_group=per_group),
        grid=(m // tm,),
        in_specs=[pl.BlockSpec((tm, d), lambda i: (i, 0)), pl.BlockSpec((1, d), lambda i: (0, 0)),
                  pl.BlockSpec((d, LANES), lambda i: (0, 0)), pl.BlockSpec((1, LANES), lambda i: (0, 0))],
        out_specs=[pl.BlockSpec((tm, d // 2), lambda i: (i, 0)), pl.BlockSpec((tm, LANES), lambda i: (i, 0))],
        out_shape=[jax.ShapeDtypeStruct((m, d // 2), U32), jax.ShapeDtypeStruct((m, LANES), F32)],
        compiler_params=_params("parallel"),
        name="moe_route",
    )(h, gain.reshape(1, d).astype(F32), w, b)


def _sorted_layout(expert_ids, n_exp, row_tile, n_tiles):
    e = expert_ids.reshape(-1)
    blk = LANES
    nb = e.shape[0] // blk
    onehot = (e[:, None] == jnp.arange(n_exp, dtype=jnp.int32)[None, :]).reshape(nb, blk, n_exp)
    tri = jnp.tril(jnp.ones((blk, blk), BF16))
    within = jnp.einsum("ij,bjk->bik", tri, onehot.astype(BF16), preferred_element_type=F32)
    totals = within[:, -1, :]
    before = jnp.dot(jnp.tril(jnp.ones((nb, nb), F32), -1), totals, precision=lax.Precision.HIGHEST)
    counts = (before[-1] + totals[-1]).astype(jnp.int32)
    rank = jnp.sum(jnp.where(onehot, within + before[:, None, :], 0.0), axis=-1).reshape(-1).astype(jnp.int32) - 1
    padded = ((counts + row_tile - 1) // row_tile) * row_tile
    ends = jnp.cumsum(padded)
    starts = ends - padded
    pos = starts[e] + rank
    tile_start = jnp.arange(n_tiles, dtype=jnp.int32) * row_tile
    tile_expert = jnp.minimum(jnp.sum((tile_start[:, None] >= ends[None, :]).astype(jnp.int32), axis=1), n_exp - 1)
    n_used = (ends[-1] // row_tile).astype(jnp.int32).reshape(1)
    return pos.astype(jnp.int32), tile_expert.astype(jnp.int32), n_used


def _row_copy(src_ref, src_row, dst_ref, dst_row, sem):
    return pltpu.make_async_copy(src_ref.at[pl.ds(src_row, 1)], dst_ref.at[pl.ds(dst_row, 1)], sem)


def _start_row_copies(n_rows, start_row):
    def body(t, c):
        start_row(t)
        return c

    lax.fori_loop(0, n_rows, body, 0, unroll=ROW_DMA_UNROLL)


def _wait_row_copies(n_copies, example_copy):
    for _ in range(n_copies):
        example_copy.wait()


def _dispatch_kernel(pos_ref, x_ref, init_ref, out_ref, sem, *, tt, n_slots):
    del init_ref

    def start_row(t):
        for k in range(n_slots):
            _row_copy(x_ref, t, out_ref, pos_ref[0, 0, n_slots * t + k], sem).start(priority=k % 2)

    _start_row_copies(tt, start_row)
    _wait_row_copies(tt * n_slots, _row_copy(x_ref, 0, out_ref, 0, sem))


def _dispatch(xp, pos, n_rows, *, tt=MOE_TOKEN_TILE, n_slots=TOP_K):
    m, d2 = xp.shape
    pos3 = pos.reshape(m // tt, 1, tt * n_slots)
    return pl.pallas_call(
        functools.partial(_dispatch_kernel, tt=tt, n_slots=n_slots),
        grid=(m // tt,),
        in_specs=[pl.BlockSpec((1, 1, tt * n_slots), lambda i: (i, 0, 0), memory_space=pltpu.SMEM),
                  pl.BlockSpec((tt, d2), lambda i: (i, 0)),
                  pl.BlockSpec(memory_space=pl.ANY)],
        out_specs=pl.BlockSpec(memory_space=pl.ANY),
        out_shape=jax.ShapeDtypeStruct((n_rows, d2), U32),
        scratch_shapes=[pltpu.SemaphoreType.DMA],
        input_output_aliases={2: 0},
        compiler_params=_params("arbitrary"),
        name="moe_dispatch",
    )(pos3, xp, jnp.zeros((n_rows, d2), U32))


def _expert_kernel(te_ref, nu_ref, x_ref, wg_ref, wu_ref, wd_ref, o_ref):
    del te_ref
    t = pl.program_id(0)

    @pl.when(t < nu_ref[0])
    def _():
        lo, hi = _unpack_halves(x_ref[...])
        x = jnp.concatenate([lo, hi], axis=1).astype(BF16)
        g = _dot(x, wg_ref[...])
        u = _dot(x, wu_ref[...])
        hid = (g * jax.nn.sigmoid(g) * u).astype(BF16)
        o_ref[...] = _pack_halves(_dot(hid, wd_ref[...]))

    @pl.when(t >= nu_ref[0])
    def _():
        o_ref[...] = jnp.zeros_like(o_ref)


def _expert_ffn(xs, tile_expert, n_used, w_gate, w_up, w_down, layer, *, tm=MOE_ROW_TILE):
    rows, d2 = xs.shape
    _, _, d, ff = w_gate.shape
    grid_spec = pltpu.PrefetchScalarGridSpec(
        num_scalar_prefetch=2,
        grid=(rows // tm,),
        in_specs=[pl.BlockSpec((tm, d2), lambda t, te, nu: (t, 0)),
                  pl.BlockSpec((None, None, d, ff), lambda t, te, nu: (layer, te[t], 0, 0)),
                  pl.BlockSpec((None, None, d, ff), lambda t, te, nu: (layer, te[t], 0, 0)),
                  pl.BlockSpec((None, None, ff, d), lambda t, te, nu: (layer, te[t], 0, 0))],
        out_specs=pl.BlockSpec((tm, d2), lambda t, te, nu: (t, 0)),
    )
    return pl.pallas_call(
        _expert_kernel,
        grid_spec=grid_spec,
        out_shape=jax.ShapeDtypeStruct((rows, d2), U32),
        compiler_params=_params("arbitrary"),
        name="moe_expert_ffn",
    )(tile_expert, n_used, xs, w_gate, w_up, w_down)


def _combine_kernel(pos_ref, info_ref, h_ref, y_ref, o_ref, buf, sem, *, tt, n_slots):
    def start_row(t):
        for k in range(n_slots):
            _row_copy(y_ref, pos_ref[0, 0, n_slots * t + k], buf.at[k], t, sem).start(priority=k % 2)

    _start_row_copies(tt, start_row)
    _wait_row_copies(tt * n_slots, _row_copy(y_ref, 0, buf.at[0], 0, sem))
    info = info_ref[...]
    lane = lax.broadcasted_iota(jnp.int32, info.shape, 1)
    lo_acc = hi_acc = None
    for k in range(n_slots):
        wk = jnp.sum(jnp.where(lane == n_slots + k, info, 0.0), axis=-1, keepdims=True)
        lo, hi = _unpack_halves(buf[k])
        lo_acc = wk * lo if lo_acc is None else lo_acc + wk * lo
        hi_acc = wk * hi if hi_acc is None else hi_acc + wk * hi
    o_ref[...] = h_ref[...] + jnp.concatenate([lo_acc, hi_acc], axis=1)


def _combine(h, info, ys, pos, *, tt=MOE_TOKEN_TILE, n_slots=TOP_K):
    m, d = h.shape
    d2 = ys.shape[1]
    pos3 = pos.reshape(m // tt, 1, tt * n_slots)
    return pl.pallas_call(
        functools.partial(_combine_kernel, tt=tt, n_slots=n_slots),
        grid=(m // tt,),
        in_specs=[pl.BlockSpec((1, 1, tt * n_slots), lambda i: (i, 0, 0), memory_space=pltpu.SMEM),
                  pl.BlockSpec((tt, LANES), lambda i: (i, 0)),
                  pl.BlockSpec((tt, d), lambda i: (i, 0)),
                  pl.BlockSpec(memory_space=pl.ANY)],
        out_specs=pl.BlockSpec((tt, d), lambda i: (i, 0)),
        out_shape=jax.ShapeDtypeStruct((m, d), F32),
        scratch_shapes=[pltpu.VMEM((n_slots, tt, d2), U32), pltpu.SemaphoreType.DMA],
        input_output_aliases={2: 0},
        compiler_params=_params("arbitrary"),
        name="moe_combine",
    )(pos3, info, h, ys)


def _moe(h, gain, w_gr, b_gr, w_er, b_er, w_gate, w_up, w_down, layer):
    m = h.shape[0]
    n_exp = w_gate.shape[1]
    n_tiles = -(-(m * TOP_K + n_exp * (MOE_ROW_TILE - 1)) // MOE_ROW_TILE)
    xp, info = _route(h, gain, w_gr, b_gr, w_er, b_er)
    expert_ids = info[:, :TOP_K].astype(jnp.int32)
    pos, tile_expert, n_used = _sorted_layout(expert_ids, n_exp, MOE_ROW_TILE, n_tiles)
    xs = _dispatch(xp, pos, n_tiles * MOE_ROW_TILE)
    ys = _expert_ffn(xs, tile_expert, n_used, w_gate, w_up, w_down, layer)
    return _combine(h, info, ys, pos)


def kernel(x, p, mix_gain, w_in, conv_w, conv_b, w_rgate, b_rgate, w_igate, b_igate, lru_lambda, q_gain, k_gain, rel_bias, ssm_a_re, ssm_a_im, ssm_log_dt, ssm_b_re, ssm_b_im, ssm_c_re, ssm_c_im, ssm_d, w_glu, b_glu, w_proj_lru, w_proj_att, w_proj_ssm, w_out, ffn_gain, w_group_router, b_group_router, w_expert_router, b_expert_router, w_up, w_gate, w_down, ple_gain, w_ple, w_ple_gate):
    bsz, seq, d = x.shape
    depth = w_in.shape[0]
    m = bsz * seq
    lru_w = w_proj_lru.shape[1]
    att_w = w_proj_att.shape[1]
    ssm_w = w_proj_ssm.shape[1]
    n_heads = att_w // HEAD_DIM
    q_off, k_off, v_off = lru_w, lru_w + att_w, lru_w + 2 * att_w
    ssm_off = lru_w + 3 * att_w
    gate_off = ssm_off + ssm_w
    in_width = w_in.shape[2]

    bf = lambda w: w.astype(BF16)
    w_in, w_glu, w_proj_lru, w_proj_att, w_proj_ssm, w_out = map(
        bf, (w_in, w_glu, w_proj_lru, w_proj_att, w_proj_ssm, w_out))
    w_gate, w_up, w_down, w_ple_gate, w_ple = map(bf, (w_gate, w_up, w_down, w_ple_gate, w_ple))
    p = bf(p.reshape(depth, m, p.shape[-1]))

    h = x.reshape(m, d).astype(F32)
    for i in range(depth):
        xn = _rmsnorm(h, mix_gain[i])
        proj = _matmul(xn, w_in, i)
        proj3 = proj.reshape(bsz, seq, in_width)
        y_lru = _rglru(proj3, conv_w[i], conv_b[i], w_rgate[i], b_rgate[i], w_igate[i], b_igate[i], lru_lambda[i])
        y_att = _attention(proj3, q_off, k_off, v_off, n_heads, q_gain[i], k_gain[i], rel_bias[i])
        tables = _s5_tables(ssm_a_re[i], ssm_a_im[i], ssm_log_dt[i], ssm_b_re[i], ssm_b_im[i],
                            ssm_c_re[i], ssm_c_im[i], ssm_d[i], S5_CHUNK)
        y_ssm = _glu(_s5(proj, ssm_off, ssm_w, bsz, tables), w_glu, i, b_glu[i])
        merged = _gated_merge(y_lru.reshape(m, lru_w), y_att.reshape(m, att_w), y_ssm,
                              w_proj_lru, w_proj_att, w_proj_ssm, i, proj, gate_off)
        h = _matmul_residual(h, merged, w_out, i, in_place=i > 0)
        h = _moe(h, ffn_gain[i], w_group_router[i], b_group_router[i], w_expert_router[i], b_expert_router[i],
                 w_gate, w_up, w_down, i)
        h = _ple(h, _rmsnorm(h, ple_gain[i]), w_ple_gate, p, w_ple, i)
    return h.reshape(bsz, seq, d)
```

```python
import functools
import math

import jax
import jax.numpy as jnp
from jax import lax
from jax.experimental import pallas as pl
from jax.experimental.pallas import tpu as pltpu

F32 = jnp.float32
BF16 = jnp.bfloat16
U32 = jnp.uint32

CHUNK = 64
CHUNK_SHIFT = CHUNK.bit_length() - 1
assert 1 << CHUNK_SHIFT == CHUNK
LEFT_CHUNKS = 8
REL_CLIP = 128
HEAD_DIM = 128
LRU_C = 8.0
TOP_K = 2
EPS = 1e-6
NEG_INF = -1e30

LANES = 128
S5_CHUNK = 8
ATTN_QBLOCK = LEFT_CHUNKS * CHUNK
MOE_ROW_TILE = 256
MOE_TOKEN_TILE = 256
ROW_DMA_UNROLL = 8
V7X_VMEM_LIMIT_BYTES = 56 * 1024 * 1024
HI16 = 0xFFFF0000


def _params(*semantics):
    return pltpu.CompilerParams(dimension_semantics=semantics, vmem_limit_bytes=V7X_VMEM_LIMIT_BYTES)


def _dot(a, b):
    return jnp.dot(a, b, preferred_element_type=F32)


def _pack_halves(x):
    n = x.shape[1] // 2
    lo = pltpu.bitcast(x[:, :n].astype(BF16).astype(F32), U32)
    hi = pltpu.bitcast(x[:, n:].astype(BF16).astype(F32), U32)
    return (lo >> 16) | (hi & jnp.uint32(HI16))


def _unpack_halves(u):
    return pltpu.bitcast(u << 16, F32), pltpu.bitcast(u & jnp.uint32(HI16), F32)


def _rmsnorm_kernel(x_ref, g_ref, o_ref):
    x = x_ref[...].astype(F32)
    ms = jnp.mean(x * x, axis=-1, keepdims=True)
    o_ref[...] = (x * lax.rsqrt(ms + EPS) * g_ref[...]).astype(o_ref.dtype)


def _rmsnorm(x, gain, *, tm=512):
    m, d = x.shape
    return pl.pallas_call(
        _rmsnorm_kernel,
        grid=(m // tm,),
        in_specs=[pl.BlockSpec((tm, d), lambda i: (i, 0)), pl.BlockSpec((1, d), lambda i: (0, 0))],
        out_specs=pl.BlockSpec((tm, d), lambda i: (i, 0)),
        out_shape=jax.ShapeDtypeStruct((m, d), BF16),
        compiler_params=_params("parallel"),
        name="rmsnorm",
    )(x, gain.reshape(1, d).astype(F32))


def _mm_kernel(x_ref, w_ref, o_ref):
    o_ref[...] = _dot(x_ref[...], w_ref[...]).astype(o_ref.dtype)


def _layer_spec(layer, block, index_map):
    return pl.BlockSpec((None,) + tuple(block), lambda *a: (layer,) + tuple(index_map(*a)))


def _matmul(x, w, layer, *, tm=1024, tn=1024, out_dtype=BF16):
    m, k = x.shape
    n = w.shape[2]
    return pl.pallas_call(
        _mm_kernel,
        grid=(m // tm, n // tn),
        in_specs=[pl.BlockSpec((tm, k), lambda i, j: (i, 0)), _layer_spec(layer, (k, tn), lambda i, j: (0, j))],
        out_specs=pl.BlockSpec((tm, tn), lambda i, j: (i, j)),
        out_shape=jax.ShapeDtypeStruct((m, n), out_dtype),
        compiler_params=_params("parallel", "arbitrary"),
        name="matmul",
    )(x, w)


def _mm_residual_kernel(h_ref, x_ref, w_ref, o_ref):
    o_ref[...] = h_ref[...] + _dot(x_ref[...], w_ref[...])


def _matmul_residual(h, x, w, layer, *, in_place, tm=1024, tn=512):
    m, k = x.shape
    n = w.shape[2]
    return pl.pallas_call(
        _mm_residual_kernel,
        grid=(m // tm, n // tn),
        in_specs=[
            pl.BlockSpec((tm, tn), lambda i, j: (i, j)),
            pl.BlockSpec((tm, k), lambda i, j: (i, 0)),
            _layer_spec(layer, (k, tn), lambda i, j: (0, j)),
        ],
        out_specs=pl.BlockSpec((tm, tn), lambda i, j: (i, j)),
        out_shape=jax.ShapeDtypeStruct((m, n), F32),
        input_output_aliases={0: 0} if in_place else {},
        compiler_params=_params("parallel", "arbitrary"),
        name="matmul_residual",
    )(h, x, w)


def _ple_kernel(h_ref, x_ref, wg_ref, p_ref, we_ref, o_ref):
    gate = jax.nn.sigmoid(_dot(x_ref[...], wg_ref[...]))
    emb = _dot(p_ref[...], we_ref[...])
    o_ref[...] = h_ref[...] + gate * emb


def _ple(h, xn, w_gate, p, w_ple, layer, *, tm=1024, tn=512):
    m, k = xn.shape
    n = w_gate.shape[2]
    kp = p.shape[2]
    return pl.pallas_call(
        _ple_kernel,
        grid=(m // tm, n // tn),
        in_specs=[
            pl.BlockSpec((tm, tn), lambda i, j: (i, j)),
            pl.BlockSpec((tm, k), lambda i, j: (i, 0)),
            _layer_spec(layer, (k, tn), lambda i, j: (0, j)),
            _layer_spec(layer, (tm, kp), lambda i, j: (i, 0)),
            _layer_spec(layer, (kp, tn), lambda i, j: (0, j)),
        ],
        out_specs=pl.BlockSpec((tm, tn), lambda i, j: (i, j)),
        out_shape=jax.ShapeDtypeStruct((m, n), F32),
        input_output_aliases={0: 0},
        compiler_params=_params("parallel", "arbitrary"),
        name="ple",
    )(h, xn, w_gate, p, w_ple)


def _merge_kernel(yl_ref, ya_ref, ys_ref, pl_ref, pa_ref, ps_ref, gl_ref, ga_ref, gs_ref, o_ref):
    acc = jax.nn.sigmoid(gl_ref[...].astype(F32)) * _dot(yl_ref[...], pl_ref[...])
    acc += jax.nn.sigmoid(ga_ref[...].astype(F32)) * _dot(ya_ref[...], pa_ref[...])
    acc += jax.nn.sigmoid(gs_ref[...].astype(F32)) * _dot(ys_ref[...], ps_ref[...])
    o_ref[...] = acc.astype(o_ref.dtype)


def _gated_merge(y_lru, y_att, y_ssm, p_lru, p_att, p_ssm, layer, proj, gate_off, *, tm=1024, tn=512):
    m = y_lru.shape[0]
    d = p_lru.shape[2]
    goff = gate_off // tn
    nd = d // tn

    def y_spec(y):
        return pl.BlockSpec((tm, y.shape[1]), lambda i, j: (i, 0))

    def p_spec(p):
        return _layer_spec(layer, (p.shape[1], tn), lambda i, j: (0, j))

    def g_spec(b):
        return pl.BlockSpec((tm, tn), lambda i, j: (i, goff + b * nd + j))

    return pl.pallas_call(
        _merge_kernel,
        grid=(m // tm, nd),
        in_specs=[y_spec(y_lru), y_spec(y_att), y_spec(y_ssm), p_spec(p_lru), p_spec(p_att), p_spec(p_ssm),
                  g_spec(0), g_spec(1), g_spec(2)],
        out_specs=pl.BlockSpec((tm, tn), lambda i, j: (i, j)),
        out_shape=jax.ShapeDtypeStruct((m, d), BF16),
        compiler_params=_params("parallel", "arbitrary"),
        name="gated_merge",
    )(y_lru, y_att, y_ssm, p_lru, p_att, p_ssm, proj, proj, proj)


def _scan_rows8(a8, x8, hprev, row):
    for k in (1, 2, 4):
        keep = row >= k
        a_sh = jnp.where(keep, pltpu.roll(a8, k, 0), 1.0)
        x_sh = jnp.where(keep, pltpu.roll(x8, k, 0), 0.0)
        x8 = a8 * x_sh + x8
        a8 = a8 * a_sh
    return a8 * hprev + x8


def _lru_kernel(u_ref, cw_ref, cb_ref, wr_ref, br_ref, wi_ref, bi_ref, lam_ref, y_ref,
                tail_ref, h_ref, a_s, x_s, *, t_rows, n_blocks, block, conv_width):
    @pl.when(pl.program_id(1) == 0)
    def _():
        tail_ref[...] = jnp.zeros_like(tail_ref)
        h_ref[...] = jnp.zeros_like(h_ref)

    u = u_ref[0].astype(F32)
    ue = jnp.concatenate([tail_ref[...], u], axis=0)
    xc = cb_ref[...]
    for j in range(conv_width):
        off = 8 - (conv_width - 1) + j
        xc = xc + cw_ref[j:j + 1, :] * ue[off:off + t_rows, :]
    tail_ref[...] = u[t_rows - 8:, :]

    xcb = xc.astype(BF16)
    rs, igs = [], []
    for hb in range(n_blocks):
        xb = xcb[:, hb * block:(hb + 1) * block]
        rs.append(_dot(xb, wr_ref[hb]))
        igs.append(_dot(xb, wi_ref[hb]))
    r = jax.nn.sigmoid(jnp.concatenate(rs, axis=1) + br_ref[...])
    ig = jax.nn.sigmoid(jnp.concatenate(igs, axis=1) + bi_ref[...])
    lam = lam_ref[...]
    softplus_neg_lam = jnp.maximum(-lam, 0.0) + jnp.log1p(jnp.exp(-jnp.abs(lam)))
    log_a = (-LRU_C) * r * softplus_neg_lam
    a = jnp.exp(log_a)
    a_s[...] = a
    x_s[...] = jnp.sqrt(-jnp.tanh(log_a) * (a * a + 1.0)) * (ig * xc)

    w = a_s.shape[1]
    row = lax.broadcasted_iota(jnp.int32, (8, w), 0)

    def body(i, hprev):
        r0 = pl.multiple_of(i * 16, 16)
        h_a = _scan_rows8(a_s[pl.ds(r0, 8), :], x_s[pl.ds(r0, 8), :], hprev, row)
        h_b = _scan_rows8(a_s[pl.ds(r0 + 8, 8), :], x_s[pl.ds(r0 + 8, 8), :], h_a[7:8, :], row)
        y_ref[0, pl.ds(r0, 16), :] = jnp.concatenate([h_a, h_b], axis=0).astype(y_ref.dtype)
        return h_b[7:8, :]

    h_ref[...] = lax.fori_loop(0, t_rows // 16, body, h_ref[...])


def _rglru(proj3, conv_w, conv_b, w_rg, b_rg, w_ig, b_ig, lam, *, t_rows=256):
    bsz, seq, _ = proj3.shape
    n_blocks, block, _ = w_rg.shape
    w = n_blocks * block
    cw = conv_w.shape[0]
    vec = lambda v: v.reshape(1, w).astype(F32)
    full2 = lambda r, c: pl.BlockSpec((r, c), lambda b, t: (0, 0))
    full3 = pl.BlockSpec((n_blocks, block, block), lambda b, t: (0, 0, 0))
    return pl.pallas_call(
        functools.partial(_lru_kernel, t_rows=t_rows, n_blocks=n_blocks, block=block, conv_width=cw),
        grid=(bsz, seq // t_rows),
        in_specs=[pl.BlockSpec((1, t_rows, w), lambda b, t: (b, t, 0)), full2(cw, w), full2(1, w),
                  full3, full2(1, w), full3, full2(1, w), full2(1, w)],
        out_specs=pl.BlockSpec((1, t_rows, w), lambda b, t: (b, t, 0)),
        out_shape=jax.ShapeDtypeStruct((bsz, seq, w), BF16),
        scratch_shapes=[pltpu.VMEM((8, w), F32), pltpu.VMEM((1, w), F32),
                        pltpu.VMEM((t_rows, w), F32), pltpu.VMEM((t_rows, w), F32)],
        compiler_params=_params("parallel", "arbitrary"),
        name="rglru",
    )(proj3, conv_w.astype(F32), vec(conv_b), w_rg.astype(BF16), vec(b_rg), w_ig.astype(BF16), vec(b_ig),
      vec(lam))


def _head_rms(x, gain):
    x = x.astype(F32)
    return x * lax.rsqrt(jnp.mean(x * x, axis=-1, keepdims=True) + EPS) * gain


def _attn_kernel(q_ref, kp_ref, kc_ref, vp_ref, vc_ref, bvec_ref, qg_ref, kg_ref, o_ref, bias_s, *, qb):
    first = pl.program_id(2) == 0

    @pl.when(first)
    def _():
        vec = jnp.broadcast_to(bvec_ref[0], (qb, 2 * qb))
        table = pltpu.roll(vec, 0, 1, stride=1, stride_axis=0)
        q_chunk = (lax.broadcasted_iota(jnp.int32, table.shape, 0) >> CHUNK_SHIFT) + (qb >> CHUNK_SHIFT)
        k_chunk = lax.broadcasted_iota(jnp.int32, table.shape, 1) >> CHUNK_SHIFT
        in_band = jnp.logical_and(k_chunk >= q_chunk - LEFT_CHUNKS, k_chunk <= q_chunk)
        bias_s[...] = jnp.where(in_band, table, NEG_INF)

    qn = (_head_rms(q_ref[0], qg_ref[...]) * (HEAD_DIM ** -0.5)).astype(BF16)
    kn = _head_rms(jnp.concatenate([kp_ref[0], kc_ref[0]], axis=0), kg_ref[...]).astype(BF16)
    s = lax.dot_general(qn, kn, (((1,), (1,)), ((), ())), preferred_element_type=F32)
    s = s + bias_s[...]
    col = lax.broadcasted_iota(jnp.int32, s.shape, 1)
    s = jnp.where(jnp.logical_and(first, col < qb), NEG_INF, s)
    p = jnp.exp(s - jnp.max(s, axis=-1, keepdims=True))
    denom = jnp.sum(p, axis=-1, keepdims=True)
    v = jnp.concatenate([vp_ref[0], vc_ref[0]], axis=0)
    o_ref[0] = (_dot(p.astype(BF16), v) / denom).astype(o_ref.dtype)


def _bias_vector(rel_bias, qb):
    mm = jnp.arange(2 * qb)
    j_minus_i = jnp.where(mm < 2 * qb - CHUNK, mm, mm - 2 * qb)
    dist = qb - j_minus_i
    return rel_bias.astype(F32)[:, None, jnp.clip(dist, -REL_CLIP, REL_CLIP) + REL_CLIP]


def _attention(proj3, q_off, k_off, v_off, n_heads, q_gain, k_gain, rel_bias, *, qb=ATTN_QBLOCK):
    assert qb % CHUNK == 0 and qb >= LEFT_CHUNKS * CHUNK and (LEFT_CHUNKS + 2) * CHUNK <= 2 * qb
    bsz, seq, _ = proj3.shape
    qo, ko, vo = q_off // HEAD_DIM, k_off // HEAD_DIM, v_off // HEAD_DIM
    cur = lambda off: pl.BlockSpec((1, qb, HEAD_DIM), lambda b, h, n: (b, n, off + h))
    prev = lambda off: pl.BlockSpec((1, qb, HEAD_DIM), lambda b, h, n: (b, jnp.maximum(n - 1, 0), off + h))
    gain = pl.BlockSpec((1, HEAD_DIM), lambda b, h, n: (0, 0))
    return pl.pallas_call(
        functools.partial(_attn_kernel, qb=qb),
        grid=(bsz, n_heads, seq // qb),
        in_specs=[cur(qo), prev(ko), cur(ko), prev(vo), cur(vo),
                  pl.BlockSpec((1, 1, 2 * qb), lambda b, h, n: (h, 0, 0)), gain, gain],
        out_specs=pl.BlockSpec((1, qb, HEAD_DIM), lambda b, h, n: (b, n, h)),
        out_shape=jax.ShapeDtypeStruct((bsz, seq, n_heads * HEAD_DIM), BF16),
        scratch_shapes=[pltpu.VMEM((qb, 2 * qb), F32)],
        compiler_params=_params("parallel", "parallel", "arbitrary"),
        name="band_attention",
    )(proj3, proj3, proj3, proj3, proj3, _bias_vector(rel_bias, qb),
      q_gain.reshape(1, HEAD_DIM).astype(F32), k_gain.reshape(1, HEAD_DIM).astype(F32))


def _s5_tables(a_re, a_im, log_dt, b_re, b_im, c_re, c_im, d_skip, t_chunk):
    hi = lax.Precision.HIGHEST
    g, p = a_re.shape
    c = b_re.shape[-1]
    gl = LANES // c
    nj = g // gl
    dt = jnp.exp(log_dt.astype(F32))[:, None]
    ar, ai = a_re.astype(F32), a_im.astype(F32)

    def apow(tau):
        tau = jnp.asarray(tau, F32)[:, None, None]
        mag = jnp.exp(tau * dt * ar)
        return mag * jnp.cos(tau * dt * ai), mag * jnp.sin(tau * dt * ai)

    pr, pi = apow(jnp.arange(t_chunk + 1))
    abar_re, abar_im = pr[1], pi[1]
    den = ar * ar + ai * ai
    nr, ni = abar_re - 1.0, abar_im
    coef_re = (nr * ar + ni * ai) / den
    coef_im = (ni * ar - nr * ai) / den
    br, bi = b_re.astype(F32), b_im.astype(F32)
    bbar_re = coef_re[..., None] * br - coef_im[..., None] * bi
    bbar_im = coef_re[..., None] * bi + coef_im[..., None] * br
    cr, ci = c_re.astype(F32), c_im.astype(F32)
    ca_re = cr[None] * pr[:, :, None, :] - ci[None] * pi[:, :, None, :]
    ca_im = cr[None] * pi[:, :, None, :] + ci[None] * pr[:, :, None, :]
    lane = jnp.arange(LANES)
    col = jnp.arange(gl * p)
    rep_c = (lane[None, :] % c == jnp.arange(c)[:, None]).astype(BF16)
    rep_p = (col[None, :] % p == jnp.arange(p)[:, None]).astype(BF16)
    lane_lane = lane[:, None] // c == lane[None, :] // c
    lane_col = lane[:, None] // c == col[None, :] // p
    col_lane = col[:, None] // p == lane[None, :] // c

    k_tap = (jnp.einsum("tgcp,gpd->tgcd", ca_re[:t_chunk], bbar_re, precision=hi)
             - jnp.einsum("tgcp,gpd->tgcd", ca_im[:t_chunk], bbar_im, precision=hi))
    lag = jnp.arange(t_chunk)[None, :] - jnp.arange(t_chunk)[:, None]
    k_st = jnp.where((lag >= 0)[:, :, None, None, None], k_tap[jnp.maximum(lag, 0)], 0.0)
    k_rows = jnp.transpose(k_st.reshape(t_chunk, t_chunk, nj, gl, c, c), (2, 0, 1, 3, 5, 4))
    k_rep = jnp.dot(k_rows.reshape(-1, c).astype(BF16), rep_c).reshape(nj, t_chunk, t_chunk, LANES, LANES)
    k_rep = jnp.where(lane_lane[None, None, None], k_rep, 0)
    k_mat = jnp.transpose(k_rep, (0, 1, 3, 2, 4)).reshape(nj, t_chunk * LANES, t_chunk * LANES)

    rev = t_chunk - 1 - jnp.arange(t_chunk)
    s_re = pr[rev][..., None] * bbar_re[None] - pi[rev][..., None] * bbar_im[None]
    s_im = pr[rev][..., None] * bbar_im[None] + pi[rev][..., None] * bbar_re[None]

    def inc_mat(x):
        rows = jnp.transpose(x.reshape(t_chunk, nj, gl, p, c), (1, 0, 2, 4, 3))
        rep = jnp.dot(rows.reshape(-1, p).astype(BF16), rep_p).reshape(nj, t_chunk, LANES, gl * p)
        return jnp.where(lane_col[None, None], rep, 0).reshape(nj, t_chunk * LANES, gl * p)

    w_in = jnp.concatenate([k_mat, inc_mat(s_re), inc_mat(s_im)], axis=-1)

    def out_mat(x):
        rows = jnp.transpose(x.reshape(t_chunk, nj, gl, c, p), (1, 2, 4, 0, 3))
        rep = jnp.dot(rows.reshape(-1, c).astype(BF16), rep_c).reshape(nj, gl * p, t_chunk, LANES)
        return jnp.where(col_lane[None, :, None, :], rep, 0).reshape(nj, gl * p, t_chunk * LANES)

    w_state = jnp.concatenate([out_mat(ca_re[1:]), out_mat(-ca_im[1:])], axis=1)

    qr, qi = apow(t_chunk * jnp.arange(1, 9))
    lanes = lambda x: jnp.transpose(x.reshape(8, nj, gl * p), (1, 0, 2))
    consts = jnp.concatenate([lanes(qr), lanes(qi)], axis=1)
    d_row = jnp.tile(d_skip.astype(F32).reshape(nj, 1, LANES), (1, 1, t_chunk))
    return w_in, w_state, consts, d_row


def _gelu_tanh(y):
    return 0.5 * y * (1.0 + jnp.tanh(math.sqrt(2.0 / math.pi) * (y + 0.044715 * (y * y * y))))


def _s5_kernel(x_ref, win_ref, wst_ref, cst_ref, d_ref, y_ref, stage, inc_re, inc_im, hp_re, hp_im,
               *, t_chunk, n_rows, tc, gp):
    stage[...] = x_ref[...].astype(F32)
    u = jnp.concatenate([stage[pl.ds(s, n_rows, stride=t_chunk), :] for s in range(t_chunk)], axis=1)
    z = _dot(u.astype(BF16), win_ref[0])
    inc_re[...] = z[:, tc:tc + gp]
    inc_im[...] = z[:, tc + gp:]
    cst = cst_ref[0]
    c_re, c_im = cst[0:8, :], cst[8:16, :]
    row = lax.broadcasted_iota(jnp.int32, (8, gp), 0)

    def body(i, carry):
        h_re, h_im = carry
        r0 = pl.multiple_of(i * 8, 8)
        x_re = inc_re[pl.ds(r0, 8), :]
        x_im = inc_im[pl.ds(r0, 8), :]
        for k in (1, 2, 4):
            keep = row >= k
            s_re = jnp.where(keep, pltpu.roll(x_re, k, 0), 0.0)
            s_im = jnp.where(keep, pltpu.roll(x_im, k, 0), 0.0)
            m_re, m_im = c_re[k - 1:k, :], c_im[k - 1:k, :]
            x_re, x_im = x_re + m_re * s_re - m_im * s_im, x_im + m_re * s_im + m_im * s_re
        o_re = x_re + c_re * h_re - c_im * h_im
        o_im = x_im + c_re * h_im + c_im * h_re
        first = row >= 1
        hp_re[pl.ds(r0, 8), :] = jnp.where(first, pltpu.roll(o_re, 1, 0), h_re)
        hp_im[pl.ds(r0, 8), :] = jnp.where(first, pltpu.roll(o_im, 1, 0), h_im)
        return o_re[7:8, :], o_im[7:8, :]

    zero = jnp.zeros((1, gp), F32)
    lax.fori_loop(0, n_rows // 8, body, (zero, zero))
    h_prev = jnp.concatenate([hp_re[...], hp_im[...]], axis=1).astype(BF16)
    y = _gelu_tanh(z[:, :tc] + _dot(h_prev, wst_ref[0]) + d_ref[0] * u)
    for t in range(t_chunk):
        stage[pl.ds(t, n_rows, stride=t_chunk), :] = y[:, t * LANES:(t + 1) * LANES]
    y_ref[...] = stage[...].astype(y_ref.dtype)


def _s5(proj, ssm_off, ssm_w, bsz, tables, *, t_chunk=S5_CHUNK):
    w_in, w_state, consts, d_row = tables
    m = proj.shape[0]
    seq = m // bsz
    nj = ssm_w // LANES
    n_rows = seq // t_chunk
    tc = t_chunk * LANES
    gp = w_state.shape[1] // 2
    jb0 = ssm_off // LANES
    return pl.pallas_call(
        functools.partial(_s5_kernel, t_chunk=t_chunk, n_rows=n_rows, tc=tc, gp=gp),
        grid=(nj, bsz),
        in_specs=[
            pl.BlockSpec((seq, LANES), lambda j, b: (b, jb0 + j)),
            pl.BlockSpec((1, tc, tc + 2 * gp), lambda j, b: (j, 0, 0)),
            pl.BlockSpec((1, 2 * gp, tc), lambda j, b: (j, 0, 0)),
            pl.BlockSpec((1, 16, gp), lambda j, b: (j, 0, 0)),
            pl.BlockSpec((1, 1, tc), lambda j, b: (j, 0, 0)),
        ],
        out_specs=pl.BlockSpec((seq, LANES), lambda j, b: (b, j)),
        out_shape=jax.ShapeDtypeStruct((m, ssm_w), BF16),
        scratch_shapes=[pltpu.VMEM((seq, LANES), F32)] + [pltpu.VMEM((n_rows, gp), F32) for _ in range(4)],
        compiler_params=_params("parallel", "parallel"),
        name="s5",
    )(proj, w_in, w_state, consts, d_row)


def _glu_kernel(x_ref, w_ref, b_ref, o_ref, *, n):
    z = _dot(x_ref[...], w_ref[...]) + b_ref[...]
    o_ref[...] = (z[:, :n] * jax.nn.sigmoid(z[:, n:])).astype(o_ref.dtype)


def _glu(x, w, layer, b, *, tm=1024):
    m, k = x.shape
    n2 = w.shape[2]
    n = n2 // 2
    return pl.pallas_call(
        functools.partial(_glu_kernel, n=n),
        grid=(m // tm,),
        in_specs=[
            pl.BlockSpec((tm, k), lambda i: (i, 0)),
            _layer_spec(layer, (k, n2), lambda i: (0, 0)),
            pl.BlockSpec((1, n2), lambda i: (0, 0)),
        ],
        out_specs=pl.BlockSpec((tm, n), lambda i: (i, 0)),
        out_shape=jax.ShapeDtypeStruct((m, n), BF16),
        compiler_params=_params("parallel"),
        name="glu",
    )(x, w, b.reshape(1, n2).astype(F32))


def _route_kernel(h_ref, g_ref, w_ref, b_ref, xp_ref, info_ref, *, n_groups, per_group):
    x = h_ref[...]
    xn = x * lax.rsqrt(jnp.mean(x * x, axis=-1, keepdims=True) + EPS) * g_ref[...]
    xp_ref[...] = _pack_halves(xn)
    n_exp = n_groups * per_group
    logits = _dot(xn.astype(BF16), w_ref[...]) + b_ref[...]
    lane = lax.broadcasted_iota(jnp.int32, logits.shape, 1).astype(F32)
    big = float(LANES)
    is_group = jnp.logical_and(lane >= n_exp, lane < n_exp + n_groups)
    gl = jnp.where(is_group, logits, -jnp.inf)
    gmax = jnp.max(gl, axis=-1, keepdims=True)
    gsel = jnp.min(jnp.where(gl == gmax, lane, big), axis=-1, keepdims=True) - n_exp
    gprob = 1.0 / jnp.sum(jnp.where(is_group, jnp.exp(logits - gmax), 0.0), axis=-1, keepdims=True)
    in_group = jnp.logical_and(lane >= gsel * per_group, lane < (gsel + 1.0) * per_group)
    el = jnp.where(in_group, logits, -jnp.inf)
    v1 = jnp.max(el, axis=-1, keepdims=True)
    i1 = jnp.min(jnp.where(el == v1, lane, big), axis=-1, keepdims=True)
    el2 = jnp.where(lane == i1, -jnp.inf, el)
    v2 = jnp.max(el2, axis=-1, keepdims=True)
    i2 = jnp.min(jnp.where(el2 == v2, lane, big), axis=-1, keepdims=True)
    e2 = jnp.exp(v2 - v1)
    w1 = gprob / (1.0 + e2)
    w2 = gprob * e2 / (1.0 + e2)
    info_ref[...] = (jnp.where(lane == 0.0, i1, 0.0) + jnp.where(lane == 1.0, i2, 0.0)
                     + jnp.where(lane == 2.0, w1, 0.0) + jnp.where(lane == 3.0, w2, 0.0))


def _route(h, gain, w_gr, b_gr, w_er, b_er, *, tm=512):
    m, d = h.shape
    n_groups, _, per_group = w_er.shape
    n_exp = n_groups * per_group
    assert n_exp + n_groups <= LANES
    w = jnp.concatenate([jnp.transpose(w_er, (1, 0, 2)).reshape(d, n_exp), w_gr], axis=1)
    w = jnp.pad(w, ((0, 0), (0, LANES - n_exp - n_groups))).astype(BF16)
    b = jnp.concatenate([b_er.reshape(n_exp), b_gr]).astype(F32)
    b = jnp.pad(b, (0, LANES - n_exp - n_groups)).reshape(1, LANES)
    return pl.pallas_call(
        functools.partial(_route_kernel, n_groups=n_groups, per_group=per_group),
        grid=(m // tm,),
        in_specs=[pl.BlockSpec((tm, d), lambda i: (i, 0)), pl.BlockSpec((1, d), lambda i: (0, 0)),
                  pl.BlockSpec((d, LANES), lambda i: (0, 0)), pl.BlockSpec((1, LANES), lambda i: (0, 0))],
        out_specs=[pl.BlockSpec((tm, d // 2), lambda i: (i, 0)), pl.BlockSpec((tm, LANES), lambda i: (i, 0))],
        out_shape=[jax.ShapeDtypeStruct((m, d // 2), U32), jax.ShapeDtypeStruct((m, LANES), F32)],
        compiler_params=_params("parallel"),
        name="moe_route",
    )(h, gain.reshape(1, d).astype(F32), w, b)


def _sorted_layout(expert_ids, n_exp, row_tile, n_tiles):
    e = expert_ids.reshape(-1)
    blk = LANES
    nb = e.shape[0] // blk
    onehot = (e[:, None] == jnp.arange(n_exp, dtype=jnp.int32)[None, :]).reshape(nb, blk, n_exp)
    tri = jnp.tril(jnp.ones((blk, blk), BF16))
    within = jnp.einsum("ij,bjk->bik", tri, onehot.astype(BF16), preferred_element_type=F32)
    totals = within[:, -1, :]
    before = jnp.dot(jnp.tril(jnp.ones((nb, nb), F32), -1), totals, precision=lax.Precision.HIGHEST)
    counts = (before[-1] + totals[-1]).astype(jnp.int32)
    rank = jnp.sum(jnp.where(onehot, within + before[:, None, :], 0.0), axis=-1).reshape(-1).astype(jnp.int32) - 1
    padded = ((counts + row_tile - 1) // row_tile) * row_tile
    ends = jnp.cumsum(padded)
    starts = ends - padded
    pos = starts[e] + rank
    tile_start = jnp.arange(n_tiles, dtype=jnp.int32) * row_tile
    tile_expert = jnp.minimum(jnp.sum((tile_start[:, None] >= ends[None, :]).astype(jnp.int32), axis=1), n_exp - 1)
    n_used = (ends[-1] // row_tile).astype(jnp.int32).reshape(1)
    return pos.astype(jnp.int32), tile_expert.astype(jnp.int32), n_used


def _row_copy(src_ref, src_row, dst_ref, dst_row, sem):
    return pltpu.make_async_copy(src_ref.at[pl.ds(src_row, 1)], dst_ref.at[pl.ds(dst_row, 1)], sem)


def _start_row_copies(n_rows, start_row):
    def body(t, c):
        start_row(t)
        return c

    lax.fori_loop(0, n_rows, body, 0, unroll=ROW_DMA_UNROLL)


def _wait_row_copies(n_copies, example_copy):
    for _ in range(n_copies):
        example_copy.wait()


def _dispatch_kernel(pos_ref, x_ref, init_ref, out_ref, sem, *, tt, n_slots):
    del init_ref

    def start_row(t):
        for k in range(n_slots):
            _row_copy(x_ref, t, out_ref, pos_ref[0, 0, n_slots * t + k], sem).start(priority=k % 2)

    _start_row_copies(tt, start_row)
    _wait_row_copies(tt * n_slots, _row_copy(x_ref, 0, out_ref, 0, sem))


def _dispatch(xp, pos, n_rows, *, tt=MOE_TOKEN_TILE, n_slots=TOP_K):
    m, d2 = xp.shape
    pos3 = pos.reshape(m // tt, 1, tt * n_slots)
    return pl.pallas_call(
        functools.partial(_dispatch_kernel, tt=tt, n_slots=n_slots),
        grid=(m // tt,),
        in_specs=[pl.BlockSpec((1, 1, tt * n_slots), lambda i: (i, 0, 0), memory_space=pltpu.SMEM),
                  pl.BlockSpec((tt, d2), lambda i: (i, 0)),
                  pl.BlockSpec(memory_space=pl.ANY)],
        out_specs=pl.BlockSpec(memory_space=pl.ANY),
        out_shape=jax.ShapeDtypeStruct((n_rows, d2), U32),
        scratch_shapes=[pltpu.SemaphoreType.DMA],
        input_output_aliases={2: 0},
        compiler_params=_params("arbitrary"),
        name="moe_dispatch",
    )(pos3, xp, jnp.zeros((n_rows, d2), U32))


def _expert_kernel(te_ref, nu_ref, x_ref, wg_ref, wu_ref, wd_ref, o_ref):
    del te_ref
    t = pl.program_id(0)

    @pl.when(t < nu_ref[0])
    def _():
        lo, hi = _unpack_halves(x_ref[...])
        x = jnp.concatenate([lo, hi], axis=1).astype(BF16)
        g = _dot(x, wg_ref[...])
        u = _dot(x, wu_ref[...])
        hid = (g * jax.nn.sigmoid(g) * u).astype(BF16)
        o_ref[...] = _pack_halves(_dot(hid, wd_ref[...]))

    @pl.when(t >= nu_ref[0])
    def _():
        o_ref[...] = jnp.zeros_like(o_ref)


def _expert_ffn(xs, tile_expert, n_used, w_gate, w_up, w_down, layer, *, tm=MOE_ROW_TILE):
    rows, d2 = xs.shape
    _, _, d, ff = w_gate.shape
    grid_spec = pltpu.PrefetchScalarGridSpec(
        num_scalar_prefetch=2,
        grid=(rows // tm,),
        in_specs=[pl.BlockSpec((tm, d2), lambda t, te, nu: (t, 0)),
                  pl.BlockSpec((None, None, d, ff), lambda t, te, nu: (layer, te[t], 0, 0)),
                  pl.BlockSpec((None, None, d, ff), lambda t, te, nu: (layer, te[t], 0, 0)),
                  pl.BlockSpec((None, None, ff, d), lambda t, te, nu: (layer, te[t], 0, 0))],
        out_specs=pl.BlockSpec((tm, d2), lambda t, te, nu: (t, 0)),
    )
    return pl.pallas_call(
        _expert_kernel,
        grid_spec=grid_spec,
        out_shape=jax.ShapeDtypeStruct((rows, d2), U32),
        compiler_params=_params("arbitrary"),
        name="moe_expert_ffn",
    )(tile_expert, n_used, xs, w_gate, w_up, w_down)


def _combine_kernel(pos_ref, info_ref, h_ref, g_ref, y_ref, o_ref, on_ref, buf, sem, *, tt, n_slots):
    def start_row(t):
        for k in range(n_slots):
            _row_copy(y_ref, pos_ref[0, 0, n_slots * t + k], buf.at[k], t, sem).start(priority=k % 2)

    _start_row_copies(tt, start_row)
    _wait_row_copies(tt * n_slots, _row_copy(y_ref, 0, buf.at[0], 0, sem))
    info = info_ref[...]
    lane = lax.broadcasted_iota(jnp.int32, info.shape, 1)
    lo_acc = hi_acc = None
    for k in range(n_slots):
        wk = jnp.sum(jnp.where(lane == n_slots + k, info, 0.0), axis=-1, keepdims=True)
        lo, hi = _unpack_halves(buf[k])
        lo_acc = wk * lo if lo_acc is None else lo_acc + wk * lo
        hi_acc = wk * hi if hi_acc is None else hi_acc + wk * hi
    out = h_ref[...] + jnp.concatenate([lo_acc, hi_acc], axis=1)
    o_ref[...] = out
    ms = jnp.mean(out * out, axis=-1, keepdims=True)
    on_ref[...] = (out * lax.rsqrt(ms + EPS) * g_ref[...]).astype(on_ref.dtype)


def _combine(h, info, ys, pos, next_gain, *, tt=MOE_TOKEN_TILE, n_slots=TOP_K):
    m, d = h.shape
    d2 = ys.shape[1]
    pos3 = pos.reshape(m // tt, 1, tt * n_slots)
    return pl.pallas_call(
        functools.partial(_combine_kernel, tt=tt, n_slots=n_slots),
        grid=(m // tt,),
        in_specs=[pl.BlockSpec((1, 1, tt * n_slots), lambda i: (i, 0, 0), memory_space=pltpu.SMEM),
                  pl.BlockSpec((tt, LANES), lambda i: (i, 0)),
                  pl.BlockSpec((tt, d), lambda i: (i, 0)),
                  pl.BlockSpec((1, d), lambda i: (0, 0)),
                  pl.BlockSpec(memory_space=pl.ANY)],
        out_specs=[pl.BlockSpec((tt, d), lambda i: (i, 0)), pl.BlockSpec((tt, d), lambda i: (i, 0))],
        out_shape=[jax.ShapeDtypeStruct((m, d), F32), jax.ShapeDtypeStruct((m, d), BF16)],
        scratch_shapes=[pltpu.VMEM((n_slots, tt, d2), U32), pltpu.SemaphoreType.DMA],
        input_output_aliases={2: 0},
        compiler_params=_params("arbitrary"),
        name="moe_combine",
    )(pos3, info, h, next_gain.reshape(1, d).astype(F32), ys)


def _moe(h, gain, w_gr, b_gr, w_er, b_er, w_gate, w_up, w_down, layer, next_gain):
    m = h.shape[0]
    n_exp = w_gate.shape[1]
    n_tiles = -(-(m * TOP_K + n_exp * (MOE_ROW_TILE - 1)) // MOE_ROW_TILE)
    xp, info = _route(h, gain, w_gr, b_gr, w_er, b_er)
    expert_ids = info[:, :TOP_K].astype(jnp.int32)
    pos, tile_expert, n_used = _sorted_layout(expert_ids, n_exp, MOE_ROW_TILE, n_tiles)
    xs = _dispatch(xp, pos, n_tiles * MOE_ROW_TILE)
    ys = _expert_ffn(xs, tile_expert, n_used, w_gate, w_up, w_down, layer)
    return _combine(h, info, ys, pos, next_gain)


def kernel(x, p, mix_gain, w_in, conv_w, conv_b, w_rgate, b_rgate, w_igate, b_igate, lru_lambda, q_gain, k_gain, rel_bias, ssm_a_re, ssm_a_im, ssm_log_dt, ssm_b_re, ssm_b_im, ssm_c_re, ssm_c_im, ssm_d, w_glu, b_glu, w_proj_lru, w_proj_att, w_proj_ssm, w_out, ffn_gain, w_group_router, b_group_router, w_expert_router, b_expert_router, w_up, w_gate, w_down, ple_gain, w_ple, w_ple_gate):
    bsz, seq, d = x.shape
    depth = w_in.shape[0]
    m = bsz * seq
    lru_w = w_proj_lru.shape[1]
    att_w = w_proj_att.shape[1]
    ssm_w = w_proj_ssm.shape[1]
    n_heads = att_w // HEAD_DIM
    q_off, k_off, v_off = lru_w, lru_w + att_w, lru_w + 2 * att_w
    ssm_off = lru_w + 3 * att_w
    gate_off = ssm_off + ssm_w
    in_width = w_in.shape[2]

    bf = lambda w: w.astype(BF16)
    w_in, w_glu, w_proj_lru, w_proj_att, w_proj_ssm, w_out = map(
        bf, (w_in, w_glu, w_proj_lru, w_proj_att, w_proj_ssm, w_out))
    w_gate, w_up, w_down, w_ple_gate, w_ple = map(bf, (w_gate, w_up, w_down, w_ple_gate, w_ple))
    p = bf(p.reshape(depth, m, p.shape[-1]))

    h = x.reshape(m, d).astype(F32)
    for i in range(depth):
        xn = _rmsnorm(h, mix_gain[i])
        proj = _matmul(xn, w_in, i)
        proj3 = proj.reshape(bsz, seq, in_width)
        y_lru = _rglru(proj3, conv_w[i], conv_b[i], w_rgate[i], b_rgate[i], w_igate[i], b_igate[i], lru_lambda[i])
        y_att = _attention(proj3, q_off, k_off, v_off, n_heads, q_gain[i], k_gain[i], rel_bias[i])
        tables = _s5_tables(ssm_a_re[i], ssm_a_im[i], ssm_log_dt[i], ssm_b_re[i], ssm_b_im[i],
                            ssm_c_re[i], ssm_c_im[i], ssm_d[i], S5_CHUNK)
        y_ssm = _glu(_s5(proj, ssm_off, ssm_w, bsz, tables), w_glu, i, b_glu[i])
        merged = _gated_merge(y_lru.reshape(m, lru_w), y_att.reshape(m, att_w), y_ssm,
                              w_proj_lru, w_proj_att, w_proj_ssm, i, proj, gate_off)
        h = _matmul_residual(h, merged, w_out, i, in_place=i > 0)
        h, hn = _moe(h, ffn_gain[i], w_group_router[i], b_group_router[i], w_expert_router[i], b_expert_router[i],
                     w_gate, w_up, w_down, i, ple_gain[i])
        h = _ple(h, hn, w_ple_gate, p, w_ple, i)
    return h.reshape(bsz, seq, d)
```

```python
import functools
import math

import jax
import jax.numpy as jnp
from jax import lax
from jax.experimental import pallas as pl
from jax.experimental.pallas import tpu as pltpu

F32 = jnp.float32
BF16 = jnp.bfloat16
U32 = jnp.uint32

CHUNK = 64
CHUNK_SHIFT = CHUNK.bit_length() - 1
assert 1 << CHUNK_SHIFT == CHUNK
LEFT_CHUNKS = 8
REL_CLIP = 128
HEAD_DIM = 128
LRU_C = 8.0
TOP_K = 2
EPS = 1e-6
NEG_INF = -1e30

LANES = 128
S5_CHUNK = 8
ATTN_QBLOCK = LEFT_CHUNKS * CHUNK
MOE_ROW_TILE = 256
MOE_TOKEN_TILE = 256
ROW_DMA_UNROLL = 8
V7X_VMEM_LIMIT_BYTES = 56 * 1024 * 1024
HI16 = 0xFFFF0000


def _params(*semantics):
    return pltpu.CompilerParams(dimension_semantics=semantics, vmem_limit_bytes=V7X_VMEM_LIMIT_BYTES)


def _dot(a, b):
    return jnp.dot(a, b, preferred_element_type=F32)


def _pack_halves(x):
    n = x.shape[1] // 2
    lo = pltpu.bitcast(x[:, :n].astype(BF16).astype(F32), U32)
    hi = pltpu.bitcast(x[:, n:].astype(BF16).astype(F32), U32)
    return (lo >> 16) | (hi & jnp.uint32(HI16))


def _unpack_halves(u):
    return pltpu.bitcast(u << 16, F32), pltpu.bitcast(u & jnp.uint32(HI16), F32)


def _rmsnorm_kernel(x_ref, g_ref, o_ref):
    x = x_ref[...].astype(F32)
    ms = jnp.mean(x * x, axis=-1, keepdims=True)
    o_ref[...] = (x * lax.rsqrt(ms + EPS) * g_ref[...]).astype(o_ref.dtype)


def _rmsnorm(x, gain, *, tm=512):
    m, d = x.shape
    return pl.pallas_call(
        _rmsnorm_kernel,
        grid=(m // tm,),
        in_specs=[pl.BlockSpec((tm, d), lambda i: (i, 0)), pl.BlockSpec((1, d), lambda i: (0, 0))],
        out_specs=pl.BlockSpec((tm, d), lambda i: (i, 0)),
        out_shape=jax.ShapeDtypeStruct((m, d), BF16),
        compiler_params=_params("parallel"),
        name="rmsnorm",
    )(x, gain.reshape(1, d).astype(F32))


def _mm_kernel(x_ref, w_ref, o_ref):
    o_ref[...] = _dot(x_ref[...], w_ref[...]).astype(o_ref.dtype)


def _layer_spec(layer, block, index_map):
    return pl.BlockSpec((None,) + tuple(block), lambda *a: (layer,) + tuple(index_map(*a)))


def _matmul(x, w, layer, *, tm=1024, tn=1024, out_dtype=BF16):
    m, k = x.shape
    n = w.shape[2]
    return pl.pallas_call(
        _mm_kernel,
        grid=(m // tm, n // tn),
        in_specs=[pl.BlockSpec((tm, k), lambda i, j: (i, 0)), _layer_spec(layer, (k, tn), lambda i, j: (0, j))],
        out_specs=pl.BlockSpec((tm, tn), lambda i, j: (i, j)),
        out_shape=jax.ShapeDtypeStruct((m, n), out_dtype),
        compiler_params=_params("parallel", "arbitrary"),
        name="matmul",
    )(x, w)


def _mm_residual_kernel(h_ref, x_ref, w_ref, o_ref):
    o_ref[...] = h_ref[...] + _dot(x_ref[...], w_ref[...])


def _matmul_residual(h, x, w, layer, *, in_place, tm=1024, tn=512):
    m, k = x.shape
    n = w.shape[2]
    return pl.pallas_call(
        _mm_residual_kernel,
        grid=(m // tm, n // tn),
        in_specs=[
            pl.BlockSpec((tm, tn), lambda i, j: (i, j)),
            pl.BlockSpec((tm, k), lambda i, j: (i, 0)),
            _layer_spec(layer, (k, tn), lambda i, j: (0, j)),
        ],
        out_specs=pl.BlockSpec((tm, tn), lambda i, j: (i, j)),
        out_shape=jax.ShapeDtypeStruct((m, n), F32),
        input_output_aliases={0: 0} if in_place else {},
        compiler_params=_params("parallel", "arbitrary"),
        name="matmul_residual",
    )(h, x, w)


def _ple_kernel(h_ref, x_ref, wg_ref, p_ref, we_ref, o_ref):
    gate = jax.nn.sigmoid(_dot(x_ref[...], wg_ref[...]))
    emb = _dot(p_ref[...], we_ref[...])
    o_ref[...] = h_ref[...] + gate * emb


def _ple(h, xn, w_gate, p, w_ple, layer, *, tm=1024, tn=512):
    m, k = xn.shape
    n = w_gate.shape[2]
    kp = p.shape[2]
    return pl.pallas_call(
        _ple_kernel,
        grid=(m // tm, n // tn),
        in_specs=[
            pl.BlockSpec((tm, tn), lambda i, j: (i, j)),
            pl.BlockSpec((tm, k), lambda i, j: (i, 0)),
            _layer_spec(layer, (k, tn), lambda i, j: (0, j)),
            _layer_spec(layer, (tm, kp), lambda i, j: (i, 0)),
            _layer_spec(layer, (kp, tn), lambda i, j: (0, j)),
        ],
        out_specs=pl.BlockSpec((tm, tn), lambda i, j: (i, j)),
        out_shape=jax.ShapeDtypeStruct((m, n), F32),
        input_output_aliases={0: 0},
        compiler_params=_params("parallel", "arbitrary"),
        name="ple",
    )(h, xn, w_gate, p, w_ple)


def _merge_kernel(yl_ref, ya_ref, ys_ref, pl_ref, pa_ref, ps_ref, gl_ref, ga_ref, gs_ref, o_ref):
    acc = jax.nn.sigmoid(gl_ref[...].astype(F32)) * _dot(yl_ref[...], pl_ref[...])
    acc += jax.nn.sigmoid(ga_ref[...].astype(F32)) * _dot(ya_ref[...], pa_ref[...])
    acc += jax.nn.sigmoid(gs_ref[...].astype(F32)) * _dot(ys_ref[...], ps_ref[...])
    o_ref[...] = acc.astype(o_ref.dtype)


def _gated_merge(y_lru, y_att, y_ssm, p_lru, p_att, p_ssm, layer, proj, gate_off, *, tm=1024, tn=512):
    m = y_lru.shape[0]
    d = p_lru.shape[2]
    goff = gate_off // tn
    nd = d // tn

    def y_spec(y):
        return pl.BlockSpec((tm, y.shape[1]), lambda i, j: (i, 0))

    def p_spec(p):
        return _layer_spec(layer, (p.shape[1], tn), lambda i, j: (0, j))

    def g_spec(b):
        return pl.BlockSpec((tm, tn), lambda i, j: (i, goff + b * nd + j))

    return pl.pallas_call(
        _merge_kernel,
        grid=(m // tm, nd),
        in_specs=[y_spec(y_lru), y_spec(y_att), y_spec(y_ssm), p_spec(p_lru), p_spec(p_att), p_spec(p_ssm),
                  g_spec(0), g_spec(1), g_spec(2)],
        out_specs=pl.BlockSpec((tm, tn), lambda i, j: (i, j)),
        out_shape=jax.ShapeDtypeStruct((m, d), BF16),
        compiler_params=_params("parallel", "arbitrary"),
        name="gated_merge",
    )(y_lru, y_att, y_ssm, p_lru, p_att, p_ssm, proj, proj, proj)


def _scan_rows8(a8, x8, hprev, row):
    for k in (1, 2, 4):
        keep = row >= k
        a_sh = jnp.where(keep, pltpu.roll(a8, k, 0), 1.0)
        x_sh = jnp.where(keep, pltpu.roll(x8, k, 0), 0.0)
        x8 = a8 * x_sh + x8
        a8 = a8 * a_sh
    return a8 * hprev + x8


def _lru_kernel(u_ref, cw_ref, cb_ref, wr_ref, br_ref, wi_ref, bi_ref, lam_ref, y_ref,
                tail_ref, h_ref, a_s, x_s, *, t_rows, n_blocks, block, conv_width):
    @pl.when(pl.program_id(1) == 0)
    def _():
        tail_ref[...] = jnp.zeros_like(tail_ref)
        h_ref[...] = jnp.zeros_like(h_ref)

    u = u_ref[0].astype(F32)
    ue = jnp.concatenate([tail_ref[...], u], axis=0)
    xc = cb_ref[...]
    for j in range(conv_width):
        off = 8 - (conv_width - 1) + j
        xc = xc + cw_ref[j:j + 1, :] * ue[off:off + t_rows, :]
    tail_ref[...] = u[t_rows - 8:, :]

    xcb = xc.astype(BF16)
    rs, igs = [], []
    for hb in range(n_blocks):
        xb = xcb[:, hb * block:(hb + 1) * block]
        rs.append(_dot(xb, wr_ref[hb]))
        igs.append(_dot(xb, wi_ref[hb]))
    r = jax.nn.sigmoid(jnp.concatenate(rs, axis=1) + br_ref[...])
    ig = jax.nn.sigmoid(jnp.concatenate(igs, axis=1) + bi_ref[...])
    lam = lam_ref[...]
    softplus_neg_lam = jnp.maximum(-lam, 0.0) + jnp.log1p(jnp.exp(-jnp.abs(lam)))
    log_a = (-LRU_C) * r * softplus_neg_lam
    a = jnp.exp(log_a)
    a_s[...] = a
    x_s[...] = jnp.sqrt(-jnp.tanh(log_a) * (a * a + 1.0)) * (ig * xc)

    w = a_s.shape[1]
    row = lax.broadcasted_iota(jnp.int32, (8, w), 0)

    def body(i, hprev):
        r0 = pl.multiple_of(i * 16, 16)
        h_a = _scan_rows8(a_s[pl.ds(r0, 8), :], x_s[pl.ds(r0, 8), :], hprev, row)
        h_b = _scan_rows8(a_s[pl.ds(r0 + 8, 8), :], x_s[pl.ds(r0 + 8, 8), :], h_a[7:8, :], row)
        y_ref[0, pl.ds(r0, 16), :] = jnp.concatenate([h_a, h_b], axis=0).astype(y_ref.dtype)
        return h_b[7:8, :]

    h_ref[...] = lax.fori_loop(0, t_rows // 16, body, h_ref[...])


def _rglru(proj3, conv_w, conv_b, w_rg, b_rg, w_ig, b_ig, lam, *, t_rows=256):
    bsz, seq, _ = proj3.shape
    n_blocks, block, _ = w_rg.shape
    w = n_blocks * block
    cw = conv_w.shape[0]
    vec = lambda v: v.reshape(1, w).astype(F32)
    full2 = lambda r, c: pl.BlockSpec((r, c), lambda b, t: (0, 0))
    full3 = pl.BlockSpec((n_blocks, block, block), lambda b, t: (0, 0, 0))
    return pl.pallas_call(
        functools.partial(_lru_kernel, t_rows=t_rows, n_blocks=n_blocks, block=block, conv_width=cw),
        grid=(bsz, seq // t_rows),
        in_specs=[pl.BlockSpec((1, t_rows, w), lambda b, t: (b, t, 0)), full2(cw, w), full2(1, w),
                  full3, full2(1, w), full3, full2(1, w), full2(1, w)],
        out_specs=pl.BlockSpec((1, t_rows, w), lambda b, t: (b, t, 0)),
        out_shape=jax.ShapeDtypeStruct((bsz, seq, w), BF16),
        scratch_shapes=[pltpu.VMEM((8, w), F32), pltpu.VMEM((1, w), F32),
                        pltpu.VMEM((t_rows, w), F32), pltpu.VMEM((t_rows, w), F32)],
        compiler_params=_params("parallel", "arbitrary"),
        name="rglru",
    )(proj3, conv_w.astype(F32), vec(conv_b), w_rg.astype(BF16), vec(b_rg), w_ig.astype(BF16), vec(b_ig),
      vec(lam))


def _head_rms(x, gain):
    x = x.astype(F32)
    return x * lax.rsqrt(jnp.mean(x * x, axis=-1, keepdims=True) + EPS) * gain


def _attn_kernel(q_ref, kp_ref, kc_ref, vp_ref, vc_ref, bvec_ref, qg_ref, kg_ref, o_ref, bias_s, *, qb):
    first = pl.program_id(2) == 0

    @pl.when(first)
    def _():
        vec = jnp.broadcast_to(bvec_ref[0], (qb, 2 * qb))
        table = pltpu.roll(vec, 0, 1, stride=1, stride_axis=0)
        q_chunk = (lax.broadcasted_iota(jnp.int32, table.shape, 0) >> CHUNK_SHIFT) + (qb >> CHUNK_SHIFT)
        k_chunk = lax.broadcasted_iota(jnp.int32, table.shape, 1) >> CHUNK_SHIFT
        in_band = jnp.logical_and(k_chunk >= q_chunk - LEFT_CHUNKS, k_chunk <= q_chunk)
        bias_s[...] = jnp.where(in_band, table, NEG_INF)

    qn = (_head_rms(q_ref[0], qg_ref[...]) * (HEAD_DIM ** -0.5)).astype(BF16)
    kn = _head_rms(jnp.concatenate([kp_ref[0], kc_ref[0]], axis=0), kg_ref[...]).astype(BF16)
    s = lax.dot_general(qn, kn, (((1,), (1,)), ((), ())), preferred_element_type=F32)
    s = s + bias_s[...]
    col = lax.broadcasted_iota(jnp.int32, s.shape, 1)
    s = jnp.where(jnp.logical_and(first, col < qb), NEG_INF, s)
    p = jnp.exp(s - jnp.max(s, axis=-1, keepdims=True))
    denom = jnp.sum(p, axis=-1, keepdims=True)
    v = jnp.concatenate([vp_ref[0], vc_ref[0]], axis=0)
    o_ref[0] = (_dot(p.astype(BF16), v) / denom).astype(o_ref.dtype)


def _bias_vector(rel_bias, qb):
    mm = jnp.arange(2 * qb)
    j_minus_i = jnp.where(mm < 2 * qb - CHUNK, mm, mm - 2 * qb)
    dist = qb - j_minus_i
    return rel_bias.astype(F32)[:, None, jnp.clip(dist, -REL_CLIP, REL_CLIP) + REL_CLIP]


def _attention(proj3, q_off, k_off, v_off, n_heads, q_gain, k_gain, rel_bias, *, qb=ATTN_QBLOCK):
    assert qb % CHUNK == 0 and qb >= LEFT_CHUNKS * CHUNK and (LEFT_CHUNKS + 2) * CHUNK <= 2 * qb
    bsz, seq, _ = proj3.shape
    qo, ko, vo = q_off // HEAD_DIM, k_off // HEAD_DIM, v_off // HEAD_DIM
    cur = lambda off: pl.BlockSpec((1, qb, HEAD_DIM), lambda b, h, n: (b, n, off + h))
    prev = lambda off: pl.BlockSpec((1, qb, HEAD_DIM), lambda b, h, n: (b, jnp.maximum(n - 1, 0), off + h))
    gain = pl.BlockSpec((1, HEAD_DIM), lambda b, h, n: (0, 0))
    return pl.pallas_call(
        functools.partial(_attn_kernel, qb=qb),
        grid=(bsz, n_heads, seq // qb),
        in_specs=[cur(qo), prev(ko), cur(ko), prev(vo), cur(vo),
                  pl.BlockSpec((1, 1, 2 * qb), lambda b, h, n: (h, 0, 0)), gain, gain],
        out_specs=pl.BlockSpec((1, qb, HEAD_DIM), lambda b, h, n: (b, n, h)),
        out_shape=jax.ShapeDtypeStruct((bsz, seq, n_heads * HEAD_DIM), BF16),
        scratch_shapes=[pltpu.VMEM((qb, 2 * qb), F32)],
        compiler_params=_params("parallel", "parallel", "arbitrary"),
        name="band_attention",
    )(proj3, proj3, proj3, proj3, proj3, _bias_vector(rel_bias, qb),
      q_gain.reshape(1, HEAD_DIM).astype(F32), k_gain.reshape(1, HEAD_DIM).astype(F32))


def _s5_tables(a_re, a_im, log_dt, b_re, b_im, c_re, c_im, d_skip, t_chunk):
    hi = lax.Precision.HIGHEST
    g, p = a_re.shape
    c = b_re.shape[-1]
    gl = LANES // c
    nj = g // gl
    dt = jnp.exp(log_dt.astype(F32))[:, None]
    ar, ai = a_re.astype(F32), a_im.astype(F32)

    def apow(tau):
        tau = jnp.asarray(tau, F32)[:, None, None]
        mag = jnp.exp(tau * dt * ar)
        return mag * jnp.cos(tau * dt * ai), mag * jnp.sin(tau * dt * ai)

    pr, pi = apow(jnp.arange(t_chunk + 1))
    abar_re, abar_im = pr[1], pi[1]
    den = ar * ar + ai * ai
    nr, ni = abar_re - 1.0, abar_im
    coef_re = (nr * ar + ni * ai) / den
    coef_im = (ni * ar - nr * ai) / den
    br, bi = b_re.astype(F32), b_im.astype(F32)
    bbar_re = coef_re[..., None] * br - coef_im[..., None] * bi
    bbar_im = coef_re[..., None] * bi + coef_im[..., None] * br
    cr, ci = c_re.astype(F32), c_im.astype(F32)
    ca_re = cr[None] * pr[:, :, None, :] - ci[None] * pi[:, :, None, :]
    ca_im = cr[None] * pi[:, :, None, :] + ci[None] * pr[:, :, None, :]
    lane = jnp.arange(LANES)
    col = jnp.arange(gl * p)
    rep_c = (lane[None, :] % c == jnp.arange(c)[:, None]).astype(BF16)
    rep_p = (col[None, :] % p == jnp.arange(p)[:, None]).astype(BF16)
    lane_lane = lane[:, None] // c == lane[None, :] // c
    lane_col = lane[:, None] // c == col[None, :] // p
    col_lane = col[:, None] // p == lane[None, :] // c

    k_tap = (jnp.einsum("tgcp,gpd->tgcd", ca_re[:t_chunk], bbar_re, precision=hi)
             - jnp.einsum("tgcp,gpd->tgcd", ca_im[:t_chunk], bbar_im, precision=hi))
    lag = jnp.arange(t_chunk)[None, :] - jnp.arange(t_chunk)[:, None]
    k_st = jnp.where((lag >= 0)[:, :, None, None, None], k_tap[jnp.maximum(lag, 0)], 0.0)
    k_rows = jnp.transpose(k_st.reshape(t_chunk, t_chunk, nj, gl, c, c), (2, 0, 1, 3, 5, 4))
    k_rep = jnp.dot(k_rows.reshape(-1, c).astype(BF16), rep_c).reshape(nj, t_chunk, t_chunk, LANES, LANES)
    k_rep = jnp.where(lane_lane[None, None, None], k_rep, 0)
    k_mat = jnp.transpose(k_rep, (0, 1, 3, 2, 4)).reshape(nj, t_chunk * LANES, t_chunk * LANES)

    rev = t_chunk - 1 - jnp.arange(t_chunk)
    s_re = pr[rev][..., None] * bbar_re[None] - pi[rev][..., None] * bbar_im[None]
    s_im = pr[rev][..., None] * bbar_im[None] + pi[rev][..., None] * bbar_re[None]

    def inc_mat(x):
        rows = jnp.transpose(x.reshape(t_chunk, nj, gl, p, c), (1, 0, 2, 4, 3))
        rep = jnp.dot(rows.reshape(-1, p).astype(BF16), rep_p).reshape(nj, t_chunk, LANES, gl * p)
        return jnp.where(lane_col[None, None], rep, 0).reshape(nj, t_chunk * LANES, gl * p)

    w_in = jnp.concatenate([k_mat, inc_mat(s_re), inc_mat(s_im)], axis=-1)

    def out_mat(x):
        rows = jnp.transpose(x.reshape(t_chunk, nj, gl, c, p), (1, 2, 4, 0, 3))
        rep = jnp.dot(rows.reshape(-1, c).astype(BF16), rep_c).reshape(nj, gl * p, t_chunk, LANES)
        return jnp.where(col_lane[None, :, None, :], rep, 0).reshape(nj, gl * p, t_chunk * LANES)

    w_state = jnp.concatenate([out_mat(ca_re[1:]), out_mat(-ca_im[1:])], axis=1)

    qr, qi = apow(t_chunk * jnp.arange(1, 9))
    lanes = lambda x: jnp.transpose(x.reshape(8, nj, gl * p), (1, 0, 2))
    consts = jnp.concatenate([lanes(qr), lanes(qi)], axis=1)
    d_row = jnp.tile(d_skip.astype(F32).reshape(nj, 1, LANES), (1, 1, t_chunk))
    return w_in, w_state, consts, d_row


def _gelu_tanh(y):
    return 0.5 * y * (1.0 + jnp.tanh(math.sqrt(2.0 / math.pi) * (y + 0.044715 * (y * y * y))))


def _s5_kernel(x_ref, win_ref, wst_ref, cst_ref, d_ref, y_ref, stage, inc_re, inc_im, hp_re, hp_im,
               *, t_chunk, n_rows, tc, gp):
    stage[...] = x_ref[...].astype(F32)
    u = jnp.concatenate([stage[pl.ds(s, n_rows, stride=t_chunk), :] for s in range(t_chunk)], axis=1)
    z = _dot(u.astype(BF16), win_ref[0])
    inc_re[...] = z[:, tc:tc + gp]
    inc_im[...] = z[:, tc + gp:]
    cst = cst_ref[0]
    c_re, c_im = cst[0:8, :], cst[8:16, :]
    row = lax.broadcasted_iota(jnp.int32, (8, gp), 0)

    def body(i, carry):
        h_re, h_im = carry
        r0 = pl.multiple_of(i * 8, 8)
        x_re = inc_re[pl.ds(r0, 8), :]
        x_im = inc_im[pl.ds(r0, 8), :]
        for k in (1, 2, 4):
            keep = row >= k
            s_re = jnp.where(keep, pltpu.roll(x_re, k, 0), 0.0)
            s_im = jnp.where(keep, pltpu.roll(x_im, k, 0), 0.0)
            m_re, m_im = c_re[k - 1:k, :], c_im[k - 1:k, :]
            x_re, x_im = x_re + m_re * s_re - m_im * s_im, x_im + m_re * s_im + m_im * s_re
        o_re = x_re + c_re * h_re - c_im * h_im
        o_im = x_im + c_re * h_im + c_im * h_re
        first = row >= 1
        hp_re[pl.ds(r0, 8), :] = jnp.where(first, pltpu.roll(o_re, 1, 0), h_re)
        hp_im[pl.ds(r0, 8), :] = jnp.where(first, pltpu.roll(o_im, 1, 0), h_im)
        return o_re[7:8, :], o_im[7:8, :]

    zero = jnp.zeros((1, gp), F32)
    lax.fori_loop(0, n_rows // 8, body, (zero, zero))
    h_prev = jnp.concatenate([hp_re[...], hp_im[...]], axis=1).astype(BF16)
    y = _gelu_tanh(z[:, :tc] + _dot(h_prev, wst_ref[0]) + d_ref[0] * u)
    for t in range(t_chunk):
        stage[pl.ds(t, n_rows, stride=t_chunk), :] = y[:, t * LANES:(t + 1) * LANES]
    y_ref[...] = stage[...].astype(y_ref.dtype)


def _s5(proj, ssm_off, ssm_w, bsz, tables, *, t_chunk=S5_CHUNK):
    w_in, w_state, consts, d_row = tables
    m = proj.shape[0]
    seq = m // bsz
    nj = ssm_w // LANES
    n_rows = seq // t_chunk
    tc = t_chunk * LANES
    gp = w_state.shape[1] // 2
    jb0 = ssm_off // LANES
    return pl.pallas_call(
        functools.partial(_s5_kernel, t_chunk=t_chunk, n_rows=n_rows, tc=tc, gp=gp),
        grid=(nj, bsz),
        in_specs=[
            pl.BlockSpec((seq, LANES), lambda j, b: (b, jb0 + j)),
            pl.BlockSpec((1, tc, tc + 2 * gp), lambda j, b: (j, 0, 0)),
            pl.BlockSpec((1, 2 * gp, tc), lambda j, b: (j, 0, 0)),
            pl.BlockSpec((1, 16, gp), lambda j, b: (j, 0, 0)),
            pl.BlockSpec((1, 1, tc), lambda j, b: (j, 0, 0)),
        ],
        out_specs=pl.BlockSpec((seq, LANES), lambda j, b: (b, j)),
        out_shape=jax.ShapeDtypeStruct((m, ssm_w), BF16),
        scratch_shapes=[pltpu.VMEM((seq, LANES), F32)] + [pltpu.VMEM((n_rows, gp), F32) for _ in range(4)],
        compiler_params=_params("parallel", "parallel"),
        name="s5",
    )(proj, w_in, w_state, consts, d_row)


def _glu_kernel(x_ref, w_ref, b_ref, o_ref, *, n):
    z = _dot(x_ref[...], w_ref[...]) + b_ref[...]
    o_ref[...] = (z[:, :n] * jax.nn.sigmoid(z[:, n:])).astype(o_ref.dtype)


def _glu(x, w, layer, b, *, tm=1024):
    m, k = x.shape
    n2 = w.shape[2]
    n = n2 // 2
    return pl.pallas_call(
        functools.partial(_glu_kernel, n=n),
        grid=(m // tm,),
        in_specs=[
            pl.BlockSpec((tm, k), lambda i: (i, 0)),
            _layer_spec(layer, (k, n2), lambda i: (0, 0)),
            pl.BlockSpec((1, n2), lambda i: (0, 0)),
        ],
        out_specs=pl.BlockSpec((tm, n), lambda i: (i, 0)),
        out_shape=jax.ShapeDtypeStruct((m, n), BF16),
        compiler_params=_params("parallel"),
        name="glu",
    )(x, w, b.reshape(1, n2).astype(F32))


def _route_kernel(h_ref, g_ref, w_ref, b_ref, xp_ref, info_ref, *, n_groups, per_group):
    x = h_ref[...]
    xn = x * lax.rsqrt(jnp.mean(x * x, axis=-1, keepdims=True) + EPS) * g_ref[...]
    xp_ref[...] = _pack_halves(xn)
    n_exp = n_groups * per_group
    logits = _dot(xn.astype(BF16), w_ref[...]) + b_ref[...]
    lane = lax.broadcasted_iota(jnp.int32, logits.shape, 1).astype(F32)
    big = float(LANES)
    is_group = jnp.logical_and(lane >= n_exp, lane < n_exp + n_groups)
    gl = jnp.where(is_group, logits, -jnp.inf)
    gmax = jnp.max(gl, axis=-1, keepdims=True)
    gsel = jnp.min(jnp.where(gl == gmax, lane, big), axis=-1, keepdims=True) - n_exp
    gprob = 1.0 / jnp.sum(jnp.where(is_group, jnp.exp(logits - gmax), 0.0), axis=-1, keepdims=True)
    in_group = jnp.logical_and(lane >= gsel * per_group, lane < (gsel + 1.0) * per_group)
    el = jnp.where(in_group, logits, -jnp.inf)
    v1 = jnp.max(el, axis=-1, keepdims=True)
    i1 = jnp.min(jnp.where(el == v1, lane, big), axis=-1, keepdims=True)
    el2 = jnp.where(lane == i1, -jnp.inf, el)
    v2 = jnp.max(el2, axis=-1, keepdims=True)
    i2 = jnp.min(jnp.where(el2 == v2, lane, big), axis=-1, keepdims=True)
    e2 = jnp.exp(v2 - v1)
    w1 = gprob / (1.0 + e2)
    w2 = gprob * e2 / (1.0 + e2)
    info_ref[...] = (jnp.where(lane == 0.0, i1, 0.0) + jnp.where(lane == 1.0, i2, 0.0)
                     + jnp.where(lane == 2.0, w1, 0.0) + jnp.where(lane == 3.0, w2, 0.0))


def _route(h, gain, w_gr, b_gr, w_er, b_er, *, tm=512):
    m, d = h.shape
    n_groups, _, per_group = w_er.shape
    n_exp = n_groups * per_group
    assert n_exp + n_groups <= LANES
    w = jnp.concatenate([jnp.transpose(w_er, (1, 0, 2)).reshape(d, n_exp), w_gr], axis=1)
    w = jnp.pad(w, ((0, 0), (0, LANES - n_exp - n_groups))).astype(BF16)
    b = jnp.concatenate([b_er.reshape(n_exp), b_gr]).astype(F32)
    b = jnp.pad(b, (0, LANES - n_exp - n_groups)).reshape(1, LANES)
    return pl.pallas_call(
        functools.partial(_route_kernel, n_groups=n_groups, per_group=per_group),
        grid=(m // tm,),
        in_specs=[pl.BlockSpec((tm, d), lambda i: (i, 0)), pl.BlockSpec((1, d), lambda i: (0, 0)),
                  pl.BlockSpec((d, LANES), lambda i: (0, 0)), pl.BlockSpec((1, LANES), lambda i: (0, 0))],
        out_specs=[pl.BlockSpec((tm, d // 2), lambda i: (i, 0)), pl.BlockSpec((tm, LANES), lambda i: (i, 0))],
        out_shape=[jax.ShapeDtypeStruct((m, d // 2), U32), jax.ShapeDtypeStruct((m, LANES), F32)],
        compiler_params=_params("parallel"),
        name="moe_route",
    )(h, gain.reshape(1, d).astype(F32), w, b)


def _sorted_layout(expert_ids, n_exp, row_tile, n_tiles):
    e = expert_ids.reshape(-1)
    blk = LANES
    nb = e.shape[0] // blk
    onehot = (e[:, None] == jnp.arange(n_exp, dtype=jnp.int32)[None, :]).reshape(nb, blk, n_exp)
    tri = jnp.tril(jnp.ones((blk, blk), BF16))
    within = jnp.einsum("ij,bjk->bik", tri, onehot.astype(BF16), preferred_element_type=F32)
    totals = within[:, -1, :]
    before = jnp.dot(jnp.tril(jnp.ones((nb, nb), F32), -1), totals, precision=lax.Precision.HIGHEST)
    counts = (before[-1] + totals[-1]).astype(jnp.int32)
    rank = jnp.sum(jnp.where(onehot, within + before[:, None, :], 0.0), axis=-1).reshape(-1).astype(jnp.int32) - 1
    padded = ((counts + row_tile - 1) // row_tile) * row_tile
    ends = jnp.cumsum(padded)
    starts = ends - padded
    pos = starts[e] + rank
    tile_start = jnp.arange(n_tiles, dtype=jnp.int32) * row_tile
    tile_expert = jnp.minimum(jnp.sum((tile_start[:, None] >= ends[None, :]).astype(jnp.int32), axis=1), n_exp - 1)
    n_used = (ends[-1] // row_tile).astype(jnp.int32).reshape(1)
    return pos.astype(jnp.int32), tile_expert.astype(jnp.int32), n_used


def _row_copy(src_ref, src_row, dst_ref, dst_row, sem):
    return pltpu.make_async_copy(src_ref.at[pl.ds(src_row, 1)], dst_ref.at[pl.ds(dst_row, 1)], sem)


def _start_row_copies(n_rows, start_row):
    def body(t, c):
        start_row(t)
        return c

    lax.fori_loop(0, n_rows, body, 0, unroll=ROW_DMA_UNROLL)


def _wait_row_copies(n_copies, example_copy):
    for _ in range(n_copies):
        example_copy.wait()


def _dispatch_kernel(pos_ref, x_ref, init_ref, out_ref, sem, *, tt, n_slots):
    del init_ref

    def start_row(t):
        for k in range(n_slots):
            _row_copy(x_ref, t, out_ref, pos_ref[0, 0, n_slots * t + k], sem).start(priority=k % 2)

    _start_row_copies(tt, start_row)
    _wait_row_copies(tt * n_slots, _row_copy(x_ref, 0, out_ref, 0, sem))


def _dispatch(xp, pos, n_rows, *, tt=MOE_TOKEN_TILE, n_slots=TOP_K):
    m, d2 = xp.shape
    pos3 = pos.reshape(m // tt, 1, tt * n_slots)
    return pl.pallas_call(
        functools.partial(_dispatch_kernel, tt=tt, n_slots=n_slots),
        grid=(m // tt,),
        in_specs=[pl.BlockSpec((1, 1, tt * n_slots), lambda i: (i, 0, 0), memory_space=pltpu.SMEM),
                  pl.BlockSpec((tt, d2), lambda i: (i, 0)),
                  pl.BlockSpec(memory_space=pl.ANY)],
        out_specs=pl.BlockSpec(memory_space=pl.ANY),
        out_shape=jax.ShapeDtypeStruct((n_rows, d2), U32),
        scratch_shapes=[pltpu.SemaphoreType.DMA],
        input_output_aliases={2: 0},
        compiler_params=_params("arbitrary"),
        name="moe_dispatch",
    )(pos3, xp, jnp.zeros((n_rows, d2), U32))


def _expert_kernel(te_ref, first_ref, next_ref, nu_ref, x_ref, wg_hbm, wu_hbm, wd_hbm, o_ref,
                   stage_g, stage_u, stage_d, wg_s, wu_s, wd_s, sem, *, layer):
    t = pl.program_id(0)
    in_use = t < nu_ref[0]
    copies = ((wg_hbm, stage_g), (wu_hbm, stage_u), (wd_hbm, stage_d))

    def weight_copy(k, expert):
        src, dst = copies[k]
        return pltpu.make_async_copy(src.at[layer, expert], dst, sem.at[k])

    @pl.when(t == 0)
    def _():
        for k in range(len(copies)):
            weight_copy(k, te_ref[0]).start()

    @pl.when(jnp.logical_and(in_use, first_ref[t] == 1))
    def _():
        for k, dst in enumerate((wg_s, wu_s, wd_s)):
            weight_copy(k, 0).wait()
            dst[...] = copies[k][1][...].astype(BF16)

        @pl.when(next_ref[t] >= 0)
        def _():
            for k in range(len(copies)):
                weight_copy(k, next_ref[t]).start()

    @pl.when(in_use)
    def _():
        lo, hi = _unpack_halves(x_ref[...])
        x = jnp.concatenate([lo, hi], axis=1).astype(BF16)
        g = _dot(x, wg_s[...])
        u = _dot(x, wu_s[...])
        hid = (g * jax.nn.sigmoid(g) * u).astype(BF16)
        o_ref[...] = _pack_halves(_dot(hid, wd_s[...]))

    @pl.when(jnp.logical_not(in_use))
    def _():
        o_ref[...] = jnp.zeros_like(o_ref)


def _expert_ffn(xs, tile_expert, n_used, w_gate, w_up, w_down, layer, *, tm=MOE_ROW_TILE):
    rows, d2 = xs.shape
    _, _, d, ff = w_gate.shape
    n_tiles = rows // tm
    idx = jnp.arange(n_tiles, dtype=jnp.int32)
    used = idx < n_used[0]
    first = jnp.logical_and(used, jnp.logical_or(idx == 0, tile_expert != jnp.roll(tile_expert, 1)))
    first_pos = jnp.where(first, idx, n_tiles)
    later = jnp.concatenate([lax.cummin(first_pos[::-1])[::-1][1:], jnp.full((1,), n_tiles, jnp.int32)])
    next_expert = jnp.where(later < n_tiles, tile_expert[jnp.minimum(later, n_tiles - 1)], -1).astype(jnp.int32)
    any_spec = pl.BlockSpec(memory_space=pl.ANY)
    grid_spec = pltpu.PrefetchScalarGridSpec(
        num_scalar_prefetch=4,
        grid=(n_tiles,),
        in_specs=[pl.BlockSpec((tm, d2), lambda t, *_: (t, 0)), any_spec, any_spec, any_spec],
        out_specs=pl.BlockSpec((tm, d2), lambda t, *_: (t, 0)),
        scratch_shapes=[pltpu.VMEM((d, ff), F32), pltpu.VMEM((d, ff), F32), pltpu.VMEM((ff, d), F32),
                        pltpu.VMEM((d, ff), BF16), pltpu.VMEM((d, ff), BF16), pltpu.VMEM((ff, d), BF16),
                        pltpu.SemaphoreType.DMA((3,))],
    )
    return pl.pallas_call(
        functools.partial(_expert_kernel, layer=layer),
        grid_spec=grid_spec,
        out_shape=jax.ShapeDtypeStruct((rows, d2), U32),
        compiler_params=_params("arbitrary"),
        name="moe_expert_ffn",
    )(tile_expert, first.astype(jnp.int32), next_expert, n_used, xs, w_gate, w_up, w_down)


def _combine_kernel(pos_ref, pos_next_ref, info_ref, h_ref, g_ref, y_ref, o_ref, on_ref, buf, sem, *, tt, n_slots):
    i = pl.program_id(0)
    cur = i % 2

    def gather(p_ref, half):
        def start_row(t):
            for k in range(n_slots):
                _row_copy(y_ref, p_ref[0, 0, n_slots * t + k], buf.at[half, k], t, sem.at[half]).start(priority=k % 2)

        _start_row_copies(tt, start_row)

    @pl.when(i == 0)
    def _():
        gather(pos_ref, 0)

    @pl.when(i + 1 < pl.num_programs(0))
    def _():
        gather(pos_next_ref, 1 - cur)

    _wait_row_copies(tt * n_slots, _row_copy(y_ref, 0, buf.at[cur, 0], 0, sem.at[cur]))
    info = info_ref[...]
    lane = lax.broadcasted_iota(jnp.int32, info.shape, 1)
    lo_acc = hi_acc = None
    for k in range(n_slots):
        wk = jnp.sum(jnp.where(lane == n_slots + k, info, 0.0), axis=-1, keepdims=True)
        lo, hi = _unpack_halves(buf[cur, k])
        lo_acc = wk * lo if lo_acc is None else lo_acc + wk * lo
        hi_acc = wk * hi if hi_acc is None else hi_acc + wk * hi
    out = h_ref[...] + jnp.concatenate([lo_acc, hi_acc], axis=1)
    o_ref[...] = out
    ms = jnp.mean(out * out, axis=-1, keepdims=True)
    on_ref[...] = (out * lax.rsqrt(ms + EPS) * g_ref[...]).astype(on_ref.dtype)


def _combine(h, info, ys, pos, next_gain, *, tt=MOE_TOKEN_TILE, n_slots=TOP_K):
    m, d = h.shape
    d2 = ys.shape[1]
    n_steps = m // tt
    pos3 = pos.reshape(n_steps, 1, tt * n_slots)
    return pl.pallas_call(
        functools.partial(_combine_kernel, tt=tt, n_slots=n_slots),
        grid=(n_steps,),
        in_specs=[pl.BlockSpec((1, 1, tt * n_slots), lambda i: (i, 0, 0), memory_space=pltpu.SMEM),
                  pl.BlockSpec((1, 1, tt * n_slots), lambda i: (jnp.minimum(i + 1, n_steps - 1), 0, 0),
                               memory_space=pltpu.SMEM),
                  pl.BlockSpec((tt, LANES), lambda i: (i, 0)),
                  pl.BlockSpec((tt, d), lambda i: (i, 0)),
                  pl.BlockSpec((1, d), lambda i: (0, 0)),
                  pl.BlockSpec(memory_space=pl.ANY)],
        out_specs=[pl.BlockSpec((tt, d), lambda i: (i, 0)), pl.BlockSpec((tt, d), lambda i: (i, 0))],
        out_shape=[jax.ShapeDtypeStruct((m, d), F32), jax.ShapeDtypeStruct((m, d), BF16)],
        scratch_shapes=[pltpu.VMEM((2, n_slots, tt, d2), U32), pltpu.SemaphoreType.DMA((2,))],
        input_output_aliases={3: 0},
        compiler_params=_params("arbitrary"),
        name="moe_combine",
    )(pos3, pos3, info, h, next_gain.reshape(1, d).astype(F32), ys)


def _moe(h, gain, w_gr, b_gr, w_er, b_er, w_gate, w_up, w_down, layer, next_gain):
    m = h.shape[0]
    n_exp = w_gate.shape[1]
    n_tiles = -(-(m * TOP_K + n_exp * (MOE_ROW_TILE - 1)) // MOE_ROW_TILE)
    xp, info = _route(h, gain, w_gr, b_gr, w_er, b_er)
    expert_ids = info[:, :TOP_K].astype(jnp.int32)
    pos, tile_expert, n_used = _sorted_layout(expert_ids, n_exp, MOE_ROW_TILE, n_tiles)
    xs = _dispatch(xp, pos, n_tiles * MOE_ROW_TILE)
    ys = _expert_ffn(xs, tile_expert, n_used, w_gate, w_up, w_down, layer)
    return _combine(h, info, ys, pos, next_gain)


def kernel(x, p, mix_gain, w_in, conv_w, conv_b, w_rgate, b_rgate, w_igate, b_igate, lru_lambda, q_gain, k_gain, rel_bias, ssm_a_re, ssm_a_im, ssm_log_dt, ssm_b_re, ssm_b_im, ssm_c_re, ssm_c_im, ssm_d, w_glu, b_glu, w_proj_lru, w_proj_att, w_proj_ssm, w_out, ffn_gain, w_group_router, b_group_router, w_expert_router, b_expert_router, w_up, w_gate, w_down, ple_gain, w_ple, w_ple_gate):
    bsz, seq, d = x.shape
    depth = w_in.shape[0]
    m = bsz * seq
    lru_w = w_proj_lru.shape[1]
    att_w = w_proj_att.shape[1]
    ssm_w = w_proj_ssm.shape[1]
    n_heads = att_w // HEAD_DIM
    q_off, k_off, v_off = lru_w, lru_w + att_w, lru_w + 2 * att_w
    ssm_off = lru_w + 3 * att_w
    gate_off = ssm_off + ssm_w
    in_width = w_in.shape[2]

    bf = lambda w: w.astype(BF16)
    w_in, w_glu, w_proj_lru, w_proj_att, w_proj_ssm, w_out = map(
        bf, (w_in, w_glu, w_proj_lru, w_proj_att, w_proj_ssm, w_out))
    w_ple_gate, w_ple = bf(w_ple_gate), bf(w_ple)
    p = bf(p.reshape(depth, m, p.shape[-1]))

    h = x.reshape(m, d).astype(F32)
    for i in range(depth):
        xn = _rmsnorm(h, mix_gain[i])
        proj = _matmul(xn, w_in, i)
        proj3 = proj.reshape(bsz, seq, in_width)
        y_lru = _rglru(proj3, conv_w[i], conv_b[i], w_rgate[i], b_rgate[i], w_igate[i], b_igate[i], lru_lambda[i])
        y_att = _attention(proj3, q_off, k_off, v_off, n_heads, q_gain[i], k_gain[i], rel_bias[i])
        tables = _s5_tables(ssm_a_re[i], ssm_a_im[i], ssm_log_dt[i], ssm_b_re[i], ssm_b_im[i],
                            ssm_c_re[i], ssm_c_im[i], ssm_d[i], S5_CHUNK)
        y_ssm = _glu(_s5(proj, ssm_off, ssm_w, bsz, tables), w_glu, i, b_glu[i])
        merged = _gated_merge(y_lru.reshape(m, lru_w), y_att.reshape(m, att_w), y_ssm,
                              w_proj_lru, w_proj_att, w_proj_ssm, i, proj, gate_off)
        h = _matmul_residual(h, merged, w_out, i, in_place=i > 0)
        h, hn = _moe(h, ffn_gain[i], w_group_router[i], b_group_router[i], w_expert_router[i], b_expert_router[i],
                     w_gate, w_up, w_down, i, ple_gain[i])
        h = _ple(h, hn, w_ple_gate, p, w_ple, i)
    return h.reshape(bsz, seq, d)
```

```python
import functools
import math

import jax
import jax.numpy as jnp
from jax import lax
from jax.experimental import pallas as pl
from jax.experimental.pallas import tpu as pltpu

F32 = jnp.float32
BF16 = jnp.bfloat16
U32 = jnp.uint32

CHUNK = 64
CHUNK_SHIFT = CHUNK.bit_length() - 1
assert 1 << CHUNK_SHIFT == CHUNK
LEFT_CHUNKS = 8
REL_CLIP = 128
HEAD_DIM = 128
LRU_C = 8.0
TOP_K = 2
EPS = 1e-6
NEG_INF = -1e30

LANES = 128
S5_CHUNK = 8
ATTN_QBLOCK = LEFT_CHUNKS * CHUNK
ATTN_SPLIT = 4
MOE_ROW_TILE = 256
MOE_TOKEN_TILE = 256
ROW_DMA_UNROLL = 8
V7X_VMEM_LIMIT_BYTES = 56 * 1024 * 1024
HI16 = 0xFFFF0000


def _params(*semantics):
    return pltpu.CompilerParams(dimension_semantics=semantics, vmem_limit_bytes=V7X_VMEM_LIMIT_BYTES)


def _dot(a, b):
    return jnp.dot(a, b, preferred_element_type=F32)


def _pack_halves(x):
    n = x.shape[1] // 2
    lo = pltpu.bitcast(x[:, :n].astype(BF16).astype(F32), U32)
    hi = pltpu.bitcast(x[:, n:].astype(BF16).astype(F32), U32)
    return (lo >> 16) | (hi & jnp.uint32(HI16))


def _unpack_halves(u):
    return pltpu.bitcast(u << 16, F32), pltpu.bitcast(u & jnp.uint32(HI16), F32)


def _rmsnorm_kernel(x_ref, g_ref, o_ref):
    x = x_ref[...].astype(F32)
    ms = jnp.mean(x * x, axis=-1, keepdims=True)
    o_ref[...] = (x * lax.rsqrt(ms + EPS) * g_ref[...]).astype(o_ref.dtype)


def _rmsnorm(x, gain, *, tm=512):
    m, d = x.shape
    return pl.pallas_call(
        _rmsnorm_kernel,
        grid=(m // tm,),
        in_specs=[pl.BlockSpec((tm, d), lambda i: (i, 0)), pl.BlockSpec((1, d), lambda i: (0, 0))],
        out_specs=pl.BlockSpec((tm, d), lambda i: (i, 0)),
        out_shape=jax.ShapeDtypeStruct((m, d), BF16),
        compiler_params=_params("parallel"),
        name="rmsnorm",
    )(x, gain.reshape(1, d).astype(F32))


def _mm_kernel(x_ref, w_hbm, o_ref, stage, w_s, sem, *, layer, tn):
    j, i = pl.program_id(0), pl.program_id(1)

    def weight_copy(col_tile):
        return pltpu.make_async_copy(w_hbm.at[layer, :, pl.ds(col_tile * tn, tn)], stage, sem)

    @pl.when(jnp.logical_and(j == 0, i == 0))
    def _():
        weight_copy(0).start()

    @pl.when(i == 0)
    def _():
        weight_copy(0).wait()
        w_s[...] = stage[...].astype(BF16)

        @pl.when(j + 1 < pl.num_programs(0))
        def _():
            weight_copy(j + 1).start()

    o_ref[...] = _dot(x_ref[...], w_s[...]).astype(o_ref.dtype)


def _layer_spec(layer, block, index_map):
    return pl.BlockSpec((None,) + tuple(block), lambda *a: (layer,) + tuple(index_map(*a)))


def _matmul(x, w, layer, *, tm=1024, tn=1024, out_dtype=BF16):
    m, k = x.shape
    n = w.shape[2]
    return pl.pallas_call(
        functools.partial(_mm_kernel, layer=layer, tn=tn),
        grid=(n // tn, m // tm),
        in_specs=[pl.BlockSpec((tm, k), lambda j, i: (i, 0)), pl.BlockSpec(memory_space=pl.ANY)],
        out_specs=pl.BlockSpec((tm, tn), lambda j, i: (i, j)),
        out_shape=jax.ShapeDtypeStruct((m, n), out_dtype),
        scratch_shapes=[pltpu.VMEM((k, tn), F32), pltpu.VMEM((k, tn), BF16), pltpu.SemaphoreType.DMA],
        compiler_params=_params("arbitrary", "arbitrary"),
        name="matmul",
    )(x, w)


def _mm_residual_kernel(h_ref, x_ref, w_ref, o_ref):
    o_ref[...] = h_ref[...] + _dot(x_ref[...], w_ref[...])


def _matmul_residual(h, x, w, layer, *, in_place, tm=1024, tn=512):
    m, k = x.shape
    n = w.shape[2]
    return pl.pallas_call(
        _mm_residual_kernel,
        grid=(m // tm, n // tn),
        in_specs=[
            pl.BlockSpec((tm, tn), lambda i, j: (i, j)),
            pl.BlockSpec((tm, k), lambda i, j: (i, 0)),
            _layer_spec(layer, (k, tn), lambda i, j: (0, j)),
        ],
        out_specs=pl.BlockSpec((tm, tn), lambda i, j: (i, j)),
        out_shape=jax.ShapeDtypeStruct((m, n), F32),
        input_output_aliases={0: 0} if in_place else {},
        compiler_params=_params("parallel", "arbitrary"),
        name="matmul_residual",
    )(h, x, w)


def _ple_kernel(h_ref, x_ref, wg_ref, p_ref, we_ref, o_ref):
    gate = jax.nn.sigmoid(_dot(x_ref[...], wg_ref[...]))
    emb = _dot(p_ref[...], we_ref[...])
    o_ref[...] = h_ref[...] + gate * emb


def _ple(h, xn, w_gate, p, w_ple, layer, *, tm=1024, tn=512):
    m, k = xn.shape
    n = w_gate.shape[2]
    kp = p.shape[2]
    return pl.pallas_call(
        _ple_kernel,
        grid=(m // tm, n // tn),
        in_specs=[
            pl.BlockSpec((tm, tn), lambda i, j: (i, j)),
            pl.BlockSpec((tm, k), lambda i, j: (i, 0)),
            _layer_spec(layer, (k, tn), lambda i, j: (0, j)),
            _layer_spec(layer, (tm, kp), lambda i, j: (i, 0)),
            _layer_spec(layer, (kp, tn), lambda i, j: (0, j)),
        ],
        out_specs=pl.BlockSpec((tm, tn), lambda i, j: (i, j)),
        out_shape=jax.ShapeDtypeStruct((m, n), F32),
        input_output_aliases={0: 0},
        compiler_params=_params("parallel", "arbitrary"),
        name="ple",
    )(h, xn, w_gate, p, w_ple)


def _merge_kernel(yl_ref, ya_ref, ys_ref, pl_ref, pa_ref, ps_ref, gl_ref, ga_ref, gs_ref, o_ref):
    acc = jax.nn.sigmoid(gl_ref[...].astype(F32)) * _dot(yl_ref[...], pl_ref[...])
    acc += jax.nn.sigmoid(ga_ref[...].astype(F32)) * _dot(ya_ref[...], pa_ref[...])
    acc += jax.nn.sigmoid(gs_ref[...].astype(F32)) * _dot(ys_ref[...], ps_ref[...])
    o_ref[...] = acc.astype(o_ref.dtype)


def _gated_merge(y_lru, y_att, y_ssm, p_lru, p_att, p_ssm, layer, proj, gate_off, *, tm=1024, tn=512):
    m = y_lru.shape[0]
    d = p_lru.shape[2]
    goff = gate_off // tn
    nd = d // tn

    def y_spec(y):
        return pl.BlockSpec((tm, y.shape[1]), lambda i, j: (i, 0))

    def p_spec(p):
        return _layer_spec(layer, (p.shape[1], tn), lambda i, j: (0, j))

    def g_spec(b):
        return pl.BlockSpec((tm, tn), lambda i, j: (i, goff + b * nd + j))

    return pl.pallas_call(
        _merge_kernel,
        grid=(m // tm, nd),
        in_specs=[y_spec(y_lru), y_spec(y_att), y_spec(y_ssm), p_spec(p_lru), p_spec(p_att), p_spec(p_ssm),
                  g_spec(0), g_spec(1), g_spec(2)],
        out_specs=pl.BlockSpec((tm, tn), lambda i, j: (i, j)),
        out_shape=jax.ShapeDtypeStruct((m, d), BF16),
        compiler_params=_params("parallel", "arbitrary"),
        name="gated_merge",
    )(y_lru, y_att, y_ssm, p_lru, p_att, p_ssm, proj, proj, proj)


def _scan_rows8(a8, x8, hprev, row):
    for k in (1, 2, 4):
        keep = row >= k
        a_sh = jnp.where(keep, pltpu.roll(a8, k, 0), 1.0)
        x_sh = jnp.where(keep, pltpu.roll(x8, k, 0), 0.0)
        x8 = a8 * x_sh + x8
        a8 = a8 * a_sh
    return a8 * hprev + x8


def _lru_kernel(u_ref, cw_ref, cb_ref, wr_ref, br_ref, wi_ref, bi_ref, lam_ref, y_ref,
                tail_ref, h_ref, a_s, x_s, *, t_rows, n_blocks, block, conv_width):
    @pl.when(pl.program_id(1) == 0)
    def _():
        tail_ref[...] = jnp.zeros_like(tail_ref)
        h_ref[...] = jnp.zeros_like(h_ref)

    u = u_ref[0].astype(F32)
    ue = jnp.concatenate([tail_ref[...], u], axis=0)
    xc = cb_ref[...]
    for j in range(conv_width):
        off = 8 - (conv_width - 1) + j
        xc = xc + cw_ref[j:j + 1, :] * ue[off:off + t_rows, :]
    tail_ref[...] = u[t_rows - 8:, :]

    xcb = xc.astype(BF16)
    rs, igs = [], []
    for hb in range(n_blocks):
        xb = xcb[:, hb * block:(hb + 1) * block]
        rs.append(_dot(xb, wr_ref[hb]))
        igs.append(_dot(xb, wi_ref[hb]))
    r = jax.nn.sigmoid(jnp.concatenate(rs, axis=1) + br_ref[...])
    ig = jax.nn.sigmoid(jnp.concatenate(igs, axis=1) + bi_ref[...])
    lam = lam_ref[...]
    softplus_neg_lam = jnp.maximum(-lam, 0.0) + jnp.log1p(jnp.exp(-jnp.abs(lam)))
    log_a = (-LRU_C) * r * softplus_neg_lam
    a = jnp.exp(log_a)
    a_s[...] = a
    x_s[...] = jnp.sqrt(-jnp.tanh(log_a) * (a * a + 1.0)) * (ig * xc)

    w = a_s.shape[1]
    row = lax.broadcasted_iota(jnp.int32, (8, w), 0)

    def body(i, hprev):
        r0 = pl.multiple_of(i * 16, 16)
        h_a = _scan_rows8(a_s[pl.ds(r0, 8), :], x_s[pl.ds(r0, 8), :], hprev, row)
        h_b = _scan_rows8(a_s[pl.ds(r0 + 8, 8), :], x_s[pl.ds(r0 + 8, 8), :], h_a[7:8, :], row)
        y_ref[0, pl.ds(r0, 16), :] = jnp.concatenate([h_a, h_b], axis=0).astype(y_ref.dtype)
        return h_b[7:8, :]

    h_ref[...] = lax.fori_loop(0, t_rows // 16, body, h_ref[...])


def _rglru(proj3, conv_w, conv_b, w_rg, b_rg, w_ig, b_ig, lam, *, t_rows=256):
    bsz, seq, _ = proj3.shape
    n_blocks, block, _ = w_rg.shape
    w = n_blocks * block
    cw = conv_w.shape[0]
    vec = lambda v: v.reshape(1, w).astype(F32)
    full2 = lambda r, c: pl.BlockSpec((r, c), lambda b, t: (0, 0))
    full3 = pl.BlockSpec((n_blocks, block, block), lambda b, t: (0, 0, 0))
    return pl.pallas_call(
        functools.partial(_lru_kernel, t_rows=t_rows, n_blocks=n_blocks, block=block, conv_width=cw),
        grid=(bsz, seq // t_rows),
        in_specs=[pl.BlockSpec((1, t_rows, w), lambda b, t: (b, t, 0)), full2(cw, w), full2(1, w),
                  full3, full2(1, w), full3, full2(1, w), full2(1, w)],
        out_specs=pl.BlockSpec((1, t_rows, w), lambda b, t: (b, t, 0)),
        out_shape=jax.ShapeDtypeStruct((bsz, seq, w), BF16),
        scratch_shapes=[pltpu.VMEM((8, w), F32), pltpu.VMEM((1, w), F32),
                        pltpu.VMEM((t_rows, w), F32), pltpu.VMEM((t_rows, w), F32)],
        compiler_params=_params("parallel", "arbitrary"),
        name="rglru",
    )(proj3, conv_w.astype(F32), vec(conv_b), w_rg.astype(BF16), vec(b_rg), w_ig.astype(BF16), vec(b_ig),
      vec(lam))


def _head_rms(x, gain):
    x = x.astype(F32)
    return x * lax.rsqrt(jnp.mean(x * x, axis=-1, keepdims=True) + EPS) * gain


def _attn_kernel(q_ref, kp_ref, kc_ref, vp_ref, vc_ref, bvec_ref, qg_ref, kg_ref, o_ref, bias_s, *, qb):
    first = pl.program_id(2) == 0

    @pl.when(first)
    def _():
        vec = jnp.broadcast_to(bvec_ref[0], (qb, 2 * qb))
        table = pltpu.roll(vec, 0, 1, stride=1, stride_axis=0)
        q_chunk = (lax.broadcasted_iota(jnp.int32, table.shape, 0) >> CHUNK_SHIFT) + (qb >> CHUNK_SHIFT)
        k_chunk = lax.broadcasted_iota(jnp.int32, table.shape, 1) >> CHUNK_SHIFT
        in_band = jnp.logical_and(k_chunk >= q_chunk - LEFT_CHUNKS, k_chunk <= q_chunk)
        bias_s[...] = jnp.where(in_band, table, NEG_INF)

    qn = (_head_rms(q_ref[0], qg_ref[...]) * (HEAD_DIM ** -0.5)).astype(BF16)
    kn = _head_rms(jnp.concatenate([kp_ref[0], kc_ref[0]], axis=0), kg_ref[...]).astype(BF16)
    v = jnp.concatenate([vp_ref[0], vc_ref[0]], axis=0)
    sub = qb // ATTN_SPLIT
    span = sub + LEFT_CHUNKS * CHUNK
    outs = []
    for part in range(ATTN_SPLIT):
        r0 = part * sub
        c0 = r0 + qb - LEFT_CHUNKS * CHUNK
        s = lax.dot_general(qn[r0:r0 + sub], kn[c0:c0 + span], (((1,), (1,)), ((), ())),
                            preferred_element_type=F32)
        s = s + bias_s[r0:r0 + sub, c0:c0 + span]
        col = c0 + lax.broadcasted_iota(jnp.int32, s.shape, 1)
        s = jnp.where(jnp.logical_and(first, col < qb), NEG_INF, s)
        p = jnp.exp(s - jnp.max(s, axis=-1, keepdims=True))
        denom = jnp.sum(p, axis=-1, keepdims=True)
        outs.append(_dot(p.astype(BF16), v[c0:c0 + span]) / denom)
    o_ref[0] = jnp.concatenate(outs, axis=0).astype(o_ref.dtype)


def _bias_vector(rel_bias, qb):
    mm = jnp.arange(2 * qb)
    j_minus_i = jnp.where(mm < 2 * qb - CHUNK, mm, mm - 2 * qb)
    dist = qb - j_minus_i
    return rel_bias.astype(F32)[:, None, jnp.clip(dist, -REL_CLIP, REL_CLIP) + REL_CLIP]


def _attention(proj3, q_off, k_off, v_off, n_heads, q_gain, k_gain, rel_bias, *, qb=ATTN_QBLOCK):
    assert qb % CHUNK == 0 and qb >= LEFT_CHUNKS * CHUNK and (LEFT_CHUNKS + 2) * CHUNK <= 2 * qb
    assert (qb // ATTN_SPLIT) % LANES == 0 and (qb - LEFT_CHUNKS * CHUNK) % LANES == 0 and LANES % CHUNK == 0
    bsz, seq, _ = proj3.shape
    qo, ko, vo = q_off // HEAD_DIM, k_off // HEAD_DIM, v_off // HEAD_DIM
    cur = lambda off: pl.BlockSpec((1, qb, HEAD_DIM), lambda b, h, n: (b, n, off + h))
    prev = lambda off: pl.BlockSpec((1, qb, HEAD_DIM), lambda b, h, n: (b, jnp.maximum(n - 1, 0), off + h))
    gain = pl.BlockSpec((1, HEAD_DIM), lambda b, h, n: (0, 0))
    return pl.pallas_call(
        functools.partial(_attn_kernel, qb=qb),
        grid=(bsz, n_heads, seq // qb),
        in_specs=[cur(qo), prev(ko), cur(ko), prev(vo), cur(vo),
                  pl.BlockSpec((1, 1, 2 * qb), lambda b, h, n: (h, 0, 0)), gain, gain],
        out_specs=pl.BlockSpec((1, qb, HEAD_DIM), lambda b, h, n: (b, n, h)),
        out_shape=jax.ShapeDtypeStruct((bsz, seq, n_heads * HEAD_DIM), BF16),
        scratch_shapes=[pltpu.VMEM((qb, 2 * qb), F32)],
        compiler_params=_params("parallel", "parallel", "arbitrary"),
        name="band_attention",
    )(proj3, proj3, proj3, proj3, proj3, _bias_vector(rel_bias, qb),
      q_gain.reshape(1, HEAD_DIM).astype(F32), k_gain.reshape(1, HEAD_DIM).astype(F32))


def _s5_tables(a_re, a_im, log_dt, b_re, b_im, c_re, c_im, d_skip, t_chunk):
    hi = lax.Precision.HIGHEST
    g, p = a_re.shape
    c = b_re.shape[-1]
    gl = LANES // c
    nj = g // gl
    dt = jnp.exp(log_dt.astype(F32))[:, None]
    ar, ai = a_re.astype(F32), a_im.astype(F32)

    def apow(tau):
        tau = jnp.asarray(tau, F32)[:, None, None]
        mag = jnp.exp(tau * dt * ar)
        return mag * jnp.cos(tau * dt * ai), mag * jnp.sin(tau * dt * ai)

    pr, pi = apow(jnp.arange(t_chunk + 1))
    abar_re, abar_im = pr[1], pi[1]
    den = ar * ar + ai * ai
    nr, ni = abar_re - 1.0, abar_im
    coef_re = (nr * ar + ni * ai) / den
    coef_im = (ni * ar - nr * ai) / den
    br, bi = b_re.astype(F32), b_im.astype(F32)
    bbar_re = coef_re[..., None] * br - coef_im[..., None] * bi
    bbar_im = coef_re[..., None] * bi + coef_im[..., None] * br
    cr, ci = c_re.astype(F32), c_im.astype(F32)
    ca_re = cr[None] * pr[:, :, None, :] - ci[None] * pi[:, :, None, :]
    ca_im = cr[None] * pi[:, :, None, :] + ci[None] * pr[:, :, None, :]
    lane = jnp.arange(LANES)
    col = jnp.arange(gl * p)
    rep_c = (lane[None, :] % c == jnp.arange(c)[:, None]).astype(BF16)
    rep_p = (col[None, :] % p == jnp.arange(p)[:, None]).astype(BF16)
    lane_lane = lane[:, None] // c == lane[None, :] // c
    lane_col = lane[:, None] // c == col[None, :] // p
    col_lane = col[:, None] // p == lane[None, :] // c

    k_tap = (jnp.einsum("tgcp,gpd->tgcd", ca_re[:t_chunk], bbar_re, precision=hi)
             - jnp.einsum("tgcp,gpd->tgcd", ca_im[:t_chunk], bbar_im, precision=hi))
    lag = jnp.arange(t_chunk)[None, :] - jnp.arange(t_chunk)[:, None]
    k_st = jnp.where((lag >= 0)[:, :, None, None, None], k_tap[jnp.maximum(lag, 0)], 0.0)
    k_rows = jnp.transpose(k_st.reshape(t_chunk, t_chunk, nj, gl, c, c), (2, 0, 1, 3, 5, 4))
    k_rep = jnp.dot(k_rows.reshape(-1, c).astype(BF16), rep_c).reshape(nj, t_chunk, t_chunk, LANES, LANES)
    k_rep = jnp.where(lane_lane[None, None, None], k_rep, 0)
    k_mat = jnp.transpose(k_rep, (0, 1, 3, 2, 4)).reshape(nj, t_chunk * LANES, t_chunk * LANES)

    rev = t_chunk - 1 - jnp.arange(t_chunk)
    s_re = pr[rev][..., None] * bbar_re[None] - pi[rev][..., None] * bbar_im[None]
    s_im = pr[rev][..., None] * bbar_im[None] + pi[rev][..., None] * bbar_re[None]

    def inc_mat(x):
        rows = jnp.transpose(x.reshape(t_chunk, nj, gl, p, c), (1, 0, 2, 4, 3))
        rep = jnp.dot(rows.reshape(-1, p).astype(BF16), rep_p).reshape(nj, t_chunk, LANES, gl * p)
        return jnp.where(lane_col[None, None], rep, 0).reshape(nj, t_chunk * LANES, gl * p)

    w_in = jnp.concatenate([k_mat, inc_mat(s_re), inc_mat(s_im)], axis=-1)

    def out_mat(x):
        rows = jnp.transpose(x.reshape(t_chunk, nj, gl, c, p), (1, 2, 4, 0, 3))
        rep = jnp.dot(rows.reshape(-1, c).astype(BF16), rep_c).reshape(nj, gl * p, t_chunk, LANES)
        return jnp.where(col_lane[None, :, None, :], rep, 0).reshape(nj, gl * p, t_chunk * LANES)

    w_state = jnp.concatenate([out_mat(ca_re[1:]), out_mat(-ca_im[1:])], axis=1)

    qr, qi = apow(t_chunk * jnp.arange(1, 9))
    lanes = lambda x: jnp.transpose(x.reshape(8, nj, gl * p), (1, 0, 2))
    consts = jnp.concatenate([lanes(qr), lanes(qi)], axis=1)
    d_row = jnp.tile(d_skip.astype(F32).reshape(nj, 1, LANES), (1, 1, t_chunk))
    return w_in, w_state, consts, d_row


def _gelu_tanh(y):
    return 0.5 * y * (1.0 + jnp.tanh(math.sqrt(2.0 / math.pi) * (y + 0.044715 * (y * y * y))))


def _s5_kernel(x_ref, win_ref, wst_ref, cst_ref, d_ref, y_ref, stage, inc_re, inc_im, hp_re, hp_im,
               *, t_chunk, n_rows, tc, gp):
    stage[...] = x_ref[...].astype(F32)
    u = jnp.concatenate([stage[pl.ds(s, n_rows, stride=t_chunk), :] for s in range(t_chunk)], axis=1)
    z = _dot(u.astype(BF16), win_ref[0])
    inc_re[...] = z[:, tc:tc + gp]
    inc_im[...] = z[:, tc + gp:]
    cst = cst_ref[0]
    c_re, c_im = cst[0:8, :], cst[8:16, :]
    row = lax.broadcasted_iota(jnp.int32, (8, gp), 0)

    def body(i, carry):
        h_re, h_im = carry
        r0 = pl.multiple_of(i * 8, 8)
        x_re = inc_re[pl.ds(r0, 8), :]
        x_im = inc_im[pl.ds(r0, 8), :]
        for k in (1, 2, 4):
            keep = row >= k
            s_re = jnp.where(keep, pltpu.roll(x_re, k, 0), 0.0)
            s_im = jnp.where(keep, pltpu.roll(x_im, k, 0), 0.0)
            m_re, m_im = c_re[k - 1:k, :], c_im[k - 1:k, :]
            x_re, x_im = x_re + m_re * s_re - m_im * s_im, x_im + m_re * s_im + m_im * s_re
        o_re = x_re + c_re * h_re - c_im * h_im
        o_im = x_im + c_re * h_im + c_im * h_re
        first = row >= 1
        hp_re[pl.ds(r0, 8), :] = jnp.where(first, pltpu.roll(o_re, 1, 0), h_re)
        hp_im[pl.ds(r0, 8), :] = jnp.where(first, pltpu.roll(o_im, 1, 0), h_im)
        return o_re[7:8, :], o_im[7:8, :]

    zero = jnp.zeros((1, gp), F32)
    lax.fori_loop(0, n_rows // 8, body, (zero, zero))
    h_prev = jnp.concatenate([hp_re[...], hp_im[...]], axis=1).astype(BF16)
    y = _gelu_tanh(z[:, :tc] + _dot(h_prev, wst_ref[0]) + d_ref[0] * u)
    for t in range(t_chunk):
        stage[pl.ds(t, n_rows, stride=t_chunk), :] = y[:, t * LANES:(t + 1) * LANES]
    y_ref[...] = stage[...].astype(y_ref.dtype)


def _s5(proj, ssm_off, ssm_w, bsz, tables, *, t_chunk=S5_CHUNK):
    w_in, w_state, consts, d_row = tables
    m = proj.shape[0]
    seq = m // bsz
    nj = ssm_w // LANES
    n_rows = seq // t_chunk
    tc = t_chunk * LANES
    gp = w_state.shape[1] // 2
    jb0 = ssm_off // LANES
    return pl.pallas_call(
        functools.partial(_s5_kernel, t_chunk=t_chunk, n_rows=n_rows, tc=tc, gp=gp),
        grid=(nj, bsz),
        in_specs=[
            pl.BlockSpec((seq, LANES), lambda j, b: (b, jb0 + j)),
            pl.BlockSpec((1, tc, tc + 2 * gp), lambda j, b: (j, 0, 0)),
            pl.BlockSpec((1, 2 * gp, tc), lambda j, b: (j, 0, 0)),
            pl.BlockSpec((1, 16, gp), lambda j, b: (j, 0, 0)),
            pl.BlockSpec((1, 1, tc), lambda j, b: (j, 0, 0)),
        ],
        out_specs=pl.BlockSpec((seq, LANES), lambda j, b: (b, j)),
        out_shape=jax.ShapeDtypeStruct((m, ssm_w), BF16),
        scratch_shapes=[pltpu.VMEM((seq, LANES), F32)] + [pltpu.VMEM((n_rows, gp), F32) for _ in range(4)],
        compiler_params=_params("parallel", "parallel"),
        name="s5",
    )(proj, w_in, w_state, consts, d_row)


def _glu_kernel(x_ref, w_ref, b_ref, o_ref, *, n):
    z = _dot(x_ref[...], w_ref[...]) + b_ref[...]
    o_ref[...] = (z[:, :n] * jax.nn.sigmoid(z[:, n:])).astype(o_ref.dtype)


def _glu(x, w, layer, b, *, tm=1024):
    m, k = x.shape
    n2 = w.shape[2]
    n = n2 // 2
    return pl.pallas_call(
        functools.partial(_glu_kernel, n=n),
        grid=(m // tm,),
        in_specs=[
            pl.BlockSpec((tm, k), lambda i: (i, 0)),
            _layer_spec(layer, (k, n2), lambda i: (0, 0)),
            pl.BlockSpec((1, n2), lambda i: (0, 0)),
        ],
        out_specs=pl.BlockSpec((tm, n), lambda i: (i, 0)),
        out_shape=jax.ShapeDtypeStruct((m, n), BF16),
        compiler_params=_params("parallel"),
        name="glu",
    )(x, w, b.reshape(1, n2).astype(F32))


def _route_kernel(h_ref, g_ref, w_ref, b_ref, xp_ref, info_ref, *, n_groups, per_group):
    x = h_ref[...]
    xn = x * lax.rsqrt(jnp.mean(x * x, axis=-1, keepdims=True) + EPS) * g_ref[...]
    xp_ref[...] = _pack_halves(xn)
    n_exp = n_groups * per_group
    logits = _dot(xn.astype(BF16), w_ref[...]) + b_ref[...]
    lane = lax.broadcasted_iota(jnp.int32, logits.shape, 1).astype(F32)
    big = float(LANES)
    is_group = jnp.logical_and(lane >= n_exp, lane < n_exp + n_groups)
    gl = jnp.where(is_group, logits, -jnp.inf)
    gmax = jnp.max(gl, axis=-1, keepdims=True)
    gsel = jnp.min(jnp.where(gl == gmax, lane, big), axis=-1, keepdims=True) - n_exp
    gprob = 1.0 / jnp.sum(jnp.where(is_group, jnp.exp(logits - gmax), 0.0), axis=-1, keepdims=True)
    in_group = jnp.logical_and(lane >= gsel * per_group, lane < (gsel + 1.0) * per_group)
    el = jnp.where(in_group, logits, -jnp.inf)
    v1 = jnp.max(el, axis=-1, keepdims=True)
    i1 = jnp.min(jnp.where(el == v1, lane, big), axis=-1, keepdims=True)
    el2 = jnp.where(lane == i1, -jnp.inf, el)
    v2 = jnp.max(el2, axis=-1, keepdims=True)
    i2 = jnp.min(jnp.where(el2 == v2, lane, big), axis=-1, keepdims=True)
    e2 = jnp.exp(v2 - v1)
    w1 = gprob / (1.0 + e2)
    w2 = gprob * e2 / (1.0 + e2)
    info_ref[...] = (jnp.where(lane == 0.0, i1, 0.0) + jnp.where(lane == 1.0, i2, 0.0)
                     + jnp.where(lane == 2.0, w1, 0.0) + jnp.where(lane == 3.0, w2, 0.0))


def _route(h, gain, w_gr, b_gr, w_er, b_er, *, tm=512):
    m, d = h.shape
    n_groups, _, per_group = w_er.shape
    n_exp = n_groups * per_group
    assert n_exp + n_groups <= LANES
    w = jnp.concatenate([jnp.transpose(w_er, (1, 0, 2)).reshape(d, n_exp), w_gr], axis=1)
    w = jnp.pad(w, ((0, 0), (0, LANES - n_exp - n_groups))).astype(BF16)
    b = jnp.concatenate([b_er.reshape(n_exp), b_gr]).astype(F32)
    b = jnp.pad(b, (0, LANES - n_exp - n_groups)).reshape(1, LANES)
    return pl.pallas_call(
        functools.partial(_route_kernel, n_groups=n_groups, per_group=per_group),
        grid=(m // tm,),
        in_specs=[pl.BlockSpec((tm, d), lambda i: (i, 0)), pl.BlockSpec((1, d), lambda i: (0, 0)),
                  pl.BlockSpec((d, LANES), lambda i: (0, 0)), pl.BlockSpec((1, LANES), lambda i: (0, 0))],
        out_specs=[pl.BlockSpec((tm, d // 2), lambda i: (i, 0)), pl.BlockSpec((tm, LANES), lambda i: (i, 0))],
        out_shape=[jax.ShapeDtypeStruct((m, d // 2), U32), jax.ShapeDtypeStruct((m, LANES), F32)],
        compiler_params=_params("parallel"),
        name="moe_route",
    )(h, gain.reshape(1, d).astype(F32), w, b)


def _sorted_layout(expert_ids, n_exp, row_tile, n_tiles):
    e = expert_ids.reshape(-1)
    blk = LANES
    nb = e.shape[0] // blk
    onehot = (e[:, None] == jnp.arange(n_exp, dtype=jnp.int32)[None, :]).reshape(nb, blk, n_exp)
    tri = jnp.tril(jnp.ones((blk, blk), BF16))
    within = jnp.einsum("ij,bjk->bik", tri, onehot.astype(BF16), preferred_element_type=F32)
    totals = within[:, -1, :]
    before = jnp.dot(jnp.tril(jnp.ones((nb, nb), F32), -1), totals, precision=lax.Precision.HIGHEST)
    counts = (before[-1] + totals[-1]).astype(jnp.int32)
    rank = jnp.sum(jnp.where(onehot, within + before[:, None, :], 0.0), axis=-1).reshape(-1).astype(jnp.int32) - 1
    padded = ((counts + row_tile - 1) // row_tile) * row_tile
    ends = jnp.cumsum(padded)
    starts = ends - padded
    pos = starts[e] + rank
    tile_start = jnp.arange(n_tiles, dtype=jnp.int32) * row_tile
    tile_expert = jnp.minimum(jnp.sum((tile_start[:, None] >= ends[None, :]).astype(jnp.int32), axis=1), n_exp - 1)
    n_used = (ends[-1] // row_tile).astype(jnp.int32).reshape(1)
    return pos.astype(jnp.int32), tile_expert.astype(jnp.int32), n_used


def _row_copy(src_ref, src_row, dst_ref, dst_row, sem):
    return pltpu.make_async_copy(src_ref.at[pl.ds(src_row, 1)], dst_ref.at[pl.ds(dst_row, 1)], sem)


def _start_row_copies(n_rows, start_row):
    def body(t, c):
        start_row(t)
        return c

    lax.fori_loop(0, n_rows, body, 0, unroll=ROW_DMA_UNROLL)


def _wait_row_copies(n_copies, example_copy):
    for _ in range(n_copies):
        example_copy.wait()


def _dispatch_kernel(pos_ref, x_ref, init_ref, out_ref, sem, *, tt, n_slots):
    del init_ref

    def start_row(t):
        for k in range(n_slots):
            _row_copy(x_ref, t, out_ref, pos_ref[0, 0, n_slots * t + k], sem).start(priority=k % 2)

    _start_row_copies(tt, start_row)
    _wait_row_copies(tt * n_slots, _row_copy(x_ref, 0, out_ref, 0, sem))


def _dispatch(xp, pos, n_rows, *, tt=MOE_TOKEN_TILE, n_slots=TOP_K):
    m, d2 = xp.shape
    pos3 = pos.reshape(m // tt, 1, tt * n_slots)
    return pl.pallas_call(
        functools.partial(_dispatch_kernel, tt=tt, n_slots=n_slots),
        grid=(m // tt,),
        in_specs=[pl.BlockSpec((1, 1, tt * n_slots), lambda i: (i, 0, 0), memory_space=pltpu.SMEM),
                  pl.BlockSpec((tt, d2), lambda i: (i, 0)),
                  pl.BlockSpec(memory_space=pl.ANY)],
        out_specs=pl.BlockSpec(memory_space=pl.ANY),
        out_shape=jax.ShapeDtypeStruct((n_rows, d2), U32),
        scratch_shapes=[pltpu.SemaphoreType.DMA],
        input_output_aliases={2: 0},
        compiler_params=_params("arbitrary"),
        name="moe_dispatch",
    )(pos3, xp, jnp.zeros((n_rows, d2), U32))


def _expert_kernel(te_ref, first_ref, next_ref, nu_ref, x_ref, wg_hbm, wu_hbm, wd_hbm, o_ref,
                   stage_g, stage_u, stage_d, wg_s, wu_s, wd_s, sem, *, layer):
    t = pl.program_id(0)
    in_use = t < nu_ref[0]
    copies = ((wg_hbm, stage_g), (wu_hbm, stage_u), (wd_hbm, stage_d))

    def weight_copy(k, expert):
        src, dst = copies[k]
        return pltpu.make_async_copy(src.at[layer, expert], dst, sem.at[k])

    @pl.when(t == 0)
    def _():
        for k in range(len(copies)):
            weight_copy(k, te_ref[0]).start()

    @pl.when(jnp.logical_and(in_use, first_ref[t] == 1))
    def _():
        for k, dst in enumerate((wg_s, wu_s, wd_s)):
            weight_copy(k, 0).wait()
            dst[...] = copies[k][1][...].astype(BF16)

        @pl.when(next_ref[t] >= 0)
        def _():
            for k in range(len(copies)):
                weight_copy(k, next_ref[t]).start()

    @pl.when(in_use)
    def _():
        lo, hi = _unpack_halves(x_ref[...])
        x = jnp.concatenate([lo, hi], axis=1).astype(BF16)
        g = _dot(x, wg_s[...])
        u = _dot(x, wu_s[...])
        hid = (g * jax.nn.sigmoid(g) * u).astype(BF16)
        o_ref[...] = _pack_halves(_dot(hid, wd_s[...]))

    @pl.when(jnp.logical_not(in_use))
    def _():
        o_ref[...] = jnp.zeros_like(o_ref)


def _expert_ffn(xs, tile_expert, n_used, w_gate, w_up, w_down, layer, *, tm=MOE_ROW_TILE):
    rows, d2 = xs.shape
    _, _, d, ff = w_gate.shape
    n_tiles = rows // tm
    idx = jnp.arange(n_tiles, dtype=jnp.int32)
    used = idx < n_used[0]
    first = jnp.logical_and(used, jnp.logical_or(idx == 0, tile_expert != jnp.roll(tile_expert, 1)))
    first_pos = jnp.where(first, idx, n_tiles)
    later = jnp.concatenate([lax.cummin(first_pos[::-1])[::-1][1:], jnp.full((1,), n_tiles, jnp.int32)])
    next_expert = jnp.where(later < n_tiles, tile_expert[jnp.minimum(later, n_tiles - 1)], -1).astype(jnp.int32)
    any_spec = pl.BlockSpec(memory_space=pl.ANY)
    grid_spec = pltpu.PrefetchScalarGridSpec(
        num_scalar_prefetch=4,
        grid=(n_tiles,),
        in_specs=[pl.BlockSpec((tm, d2), lambda t, *_: (t, 0)), any_spec, any_spec, any_spec],
        out_specs=pl.BlockSpec((tm, d2), lambda t, *_: (t, 0)),
        scratch_shapes=[pltpu.VMEM((d, ff), F32), pltpu.VMEM((d, ff), F32), pltpu.VMEM((ff, d), F32),
                        pltpu.VMEM((d, ff), BF16), pltpu.VMEM((d, ff), BF16), pltpu.VMEM((ff, d), BF16),
                        pltpu.SemaphoreType.DMA((3,))],
    )
    return pl.pallas_call(
        functools.partial(_expert_kernel, layer=layer),
        grid_spec=grid_spec,
        out_shape=jax.ShapeDtypeStruct((rows, d2), U32),
        compiler_params=_params("arbitrary"),
        name="moe_expert_ffn",
    )(tile_expert, first.astype(jnp.int32), next_expert, n_used, xs, w_gate, w_up, w_down)


def _combine_kernel(pos_ref, pos_next_ref, info_ref, h_ref, g_ref, y_ref, o_ref, on_ref, buf, sem, *, tt, n_slots):
    i = pl.program_id(0)
    cur = i % 2

    def gather(p_ref, half):
        def start_row(t):
            for k in range(n_slots):
                _row_copy(y_ref, p_ref[0, 0, n_slots * t + k], buf.at[half, k], t, sem.at[half]).start(priority=k % 2)

        _start_row_copies(tt, start_row)

    @pl.when(i == 0)
    def _():
        gather(pos_ref, 0)

    @pl.when(i + 1 < pl.num_programs(0))
    def _():
        gather(pos_next_ref, 1 - cur)

    _wait_row_copies(tt * n_slots, _row_copy(y_ref, 0, buf.at[cur, 0], 0, sem.at[cur]))
    info = info_ref[...]
    lane = lax.broadcasted_iota(jnp.int32, info.shape, 1)
    lo_acc = hi_acc = None
    for k in range(n_slots):
        wk = jnp.sum(jnp.where(lane == n_slots + k, info, 0.0), axis=-1, keepdims=True)
        lo, hi = _unpack_halves(buf[cur, k])
        lo_acc = wk * lo if lo_acc is None else lo_acc + wk * lo
        hi_acc = wk * hi if hi_acc is None else hi_acc + wk * hi
    out = h_ref[...] + jnp.concatenate([lo_acc, hi_acc], axis=1)
    o_ref[...] = out
    ms = jnp.mean(out * out, axis=-1, keepdims=True)
    on_ref[...] = (out * lax.rsqrt(ms + EPS) * g_ref[...]).astype(on_ref.dtype)


def _combine(h, info, ys, pos, next_gain, *, tt=MOE_TOKEN_TILE, n_slots=TOP_K):
    m, d = h.shape
    d2 = ys.shape[1]
    n_steps = m // tt
    pos3 = pos.reshape(n_steps, 1, tt * n_slots)
    return pl.pallas_call(
        functools.partial(_combine_kernel, tt=tt, n_slots=n_slots),
        grid=(n_steps,),
        in_specs=[pl.BlockSpec((1, 1, tt * n_slots), lambda i: (i, 0, 0), memory_space=pltpu.SMEM),
                  pl.BlockSpec((1, 1, tt * n_slots), lambda i: (jnp.minimum(i + 1, n_steps - 1), 0, 0),
                               memory_space=pltpu.SMEM),
                  pl.BlockSpec((tt, LANES), lambda i: (i, 0)),
                  pl.BlockSpec((tt, d), lambda i: (i, 0)),
                  pl.BlockSpec((1, d), lambda i: (0, 0)),
                  pl.BlockSpec(memory_space=pl.ANY)],
        out_specs=[pl.BlockSpec((tt, d), lambda i: (i, 0)), pl.BlockSpec((tt, d), lambda i: (i, 0))],
        out_shape=[jax.ShapeDtypeStruct((m, d), F32), jax.ShapeDtypeStruct((m, d), BF16)],
        scratch_shapes=[pltpu.VMEM((2, n_slots, tt, d2), U32), pltpu.SemaphoreType.DMA((2,))],
        input_output_aliases={3: 0},
        compiler_params=_params("arbitrary"),
        name="moe_combine",
    )(pos3, pos3, info, h, next_gain.reshape(1, d).astype(F32), ys)


def _moe(h, gain, w_gr, b_gr, w_er, b_er, w_gate, w_up, w_down, layer, next_gain):
    m = h.shape[0]
    n_exp = w_gate.shape[1]
    n_tiles = -(-(m * TOP_K + n_exp * (MOE_ROW_TILE - 1)) // MOE_ROW_TILE)
    xp, info = _route(h, gain, w_gr, b_gr, w_er, b_er)
    expert_ids = info[:, :TOP_K].astype(jnp.int32)
    pos, tile_expert, n_used = _sorted_layout(expert_ids, n_exp, MOE_ROW_TILE, n_tiles)
    xs = _dispatch(xp, pos, n_tiles * MOE_ROW_TILE)
    ys = _expert_ffn(xs, tile_expert, n_used, w_gate, w_up, w_down, layer)
    return _combine(h, info, ys, pos, next_gain)


def kernel(x, p, mix_gain, w_in, conv_w, conv_b, w_rgate, b_rgate, w_igate, b_igate, lru_lambda, q_gain, k_gain, rel_bias, ssm_a_re, ssm_a_im, ssm_log_dt, ssm_b_re, ssm_b_im, ssm_c_re, ssm_c_im, ssm_d, w_glu, b_glu, w_proj_lru, w_proj_att, w_proj_ssm, w_out, ffn_gain, w_group_router, b_group_router, w_expert_router, b_expert_router, w_up, w_gate, w_down, ple_gain, w_ple, w_ple_gate):
    bsz, seq, d = x.shape
    depth = w_in.shape[0]
    m = bsz * seq
    lru_w = w_proj_lru.shape[1]
    att_w = w_proj_att.shape[1]
    ssm_w = w_proj_ssm.shape[1]
    n_heads = att_w // HEAD_DIM
    q_off, k_off, v_off = lru_w, lru_w + att_w, lru_w + 2 * att_w
    ssm_off = lru_w + 3 * att_w
    gate_off = ssm_off + ssm_w
    in_width = w_in.shape[2]

    bf = lambda w: w.astype(BF16)
    w_glu, w_proj_lru, w_proj_att, w_proj_ssm, w_out = map(bf, (w_glu, w_proj_lru, w_proj_att, w_proj_ssm, w_out))
    w_ple_gate, w_ple = bf(w_ple_gate), bf(w_ple)
    p = bf(p.reshape(depth, m, p.shape[-1]))

    h = x.reshape(m, d).astype(F32)
    for i in range(depth):
        xn = _rmsnorm(h, mix_gain[i])
        proj = _matmul(xn, w_in, i)
        proj3 = proj.reshape(bsz, seq, in_width)
        y_lru = _rglru(proj3, conv_w[i], conv_b[i], w_rgate[i], b_rgate[i], w_igate[i], b_igate[i], lru_lambda[i])
        y_att = _attention(proj3, q_off, k_off, v_off, n_heads, q_gain[i], k_gain[i], rel_bias[i])
        tables = _s5_tables(ssm_a_re[i], ssm_a_im[i], ssm_log_dt[i], ssm_b_re[i], ssm_b_im[i],
                            ssm_c_re[i], ssm_c_im[i], ssm_d[i], S5_CHUNK)
        y_ssm = _glu(_s5(proj, ssm_off, ssm_w, bsz, tables), w_glu, i, b_glu[i])
        merged = _gated_merge(y_lru.reshape(m, lru_w), y_att.reshape(m, att_w), y_ssm,
                              w_proj_lru, w_proj_att, w_proj_ssm, i, proj, gate_off)
        h = _matmul_residual(h, merged, w_out, i, in_place=i > 0)
        h, hn = _moe(h, ffn_gain[i], w_group_router[i], b_group_router[i], w_expert_router[i], b_expert_router[i],
                     w_gate, w_up, w_down, i, ple_gain[i])
        h = _ple(h, hn, w_ple_gate, p, w_ple, i)
    return h.reshape(bsz, seq, d)
```

```python
import functools
import math

import jax
import jax.numpy as jnp
from jax import lax
from jax.experimental import pallas as pl
from jax.experimental.pallas import tpu as pltpu

F32 = jnp.float32
BF16 = jnp.bfloat16
U32 = jnp.uint32

CHUNK = 64
CHUNK_SHIFT = CHUNK.bit_length() - 1
assert 1 << CHUNK_SHIFT == CHUNK
LEFT_CHUNKS = 8
REL_CLIP = 128
HEAD_DIM = 128
LRU_C = 8.0
TOP_K = 2
EPS = 1e-6
NEG_INF = -1e30

LANES = 128
S5_CHUNK = 8
ATTN_QBLOCK = LEFT_CHUNKS * CHUNK
ATTN_SPLIT = 4
MOE_ROW_TILE = 256
MOE_TOKEN_TILE = 256
ROW_DMA_UNROLL = 8
V7X_VMEM_LIMIT_BYTES = 56 * 1024 * 1024
HI16 = 0xFFFF0000


def _params(*semantics):
    return pltpu.CompilerParams(dimension_semantics=semantics, vmem_limit_bytes=V7X_VMEM_LIMIT_BYTES)


def _dot(a, b):
    return jnp.dot(a, b, preferred_element_type=F32)


def _pack_halves(x):
    n = x.shape[1] // 2
    lo = pltpu.bitcast(x[:, :n].astype(BF16).astype(F32), U32)
    hi = pltpu.bitcast(x[:, n:].astype(BF16).astype(F32), U32)
    return (lo >> 16) | (hi & jnp.uint32(HI16))


def _unpack_halves(u):
    return pltpu.bitcast(u << 16, F32), pltpu.bitcast(u & jnp.uint32(HI16), F32)


def _rmsnorm_kernel(x_ref, g_ref, o_ref):
    x = x_ref[...].astype(F32)
    ms = jnp.mean(x * x, axis=-1, keepdims=True)
    o_ref[...] = (x * lax.rsqrt(ms + EPS) * g_ref[...]).astype(o_ref.dtype)


def _rmsnorm(x, gain, *, tm=512):
    m, d = x.shape
    return pl.pallas_call(
        _rmsnorm_kernel,
        grid=(m // tm,),
        in_specs=[pl.BlockSpec((tm, d), lambda i: (i, 0)), pl.BlockSpec((1, d), lambda i: (0, 0))],
        out_specs=pl.BlockSpec((tm, d), lambda i: (i, 0)),
        out_shape=jax.ShapeDtypeStruct((m, d), BF16),
        compiler_params=_params("parallel"),
        name="rmsnorm",
    )(x, gain.reshape(1, d).astype(F32))


def _mm_kernel(x_ref, w_hbm, o_ref, stage, w_s, sem, *, layer, tn, col_tile, lane_tile_major):
    j, i = pl.program_id(0), pl.program_id(1)

    def weight_copy(jj):
        return pltpu.make_async_copy(w_hbm.at[layer, :, pl.ds(col_tile(jj) * tn, tn)], stage, sem)

    @pl.when(jnp.logical_and(j == 0, i == 0))
    def _():
        weight_copy(0).start()

    @pl.when(i == 0)
    def _():
        weight_copy(0).wait()
        w_s[...] = stage[...].astype(BF16)

        @pl.when(j + 1 < pl.num_programs(0))
        def _():
            weight_copy(j + 1).start()

    res = _dot(x_ref[...], w_s[...]).astype(o_ref.dtype)
    if lane_tile_major:
        for t in range(tn // LANES):
            o_ref[t] = res[:, t * LANES:(t + 1) * LANES]
    else:
        o_ref[...] = res


def _layer_spec(layer, block, index_map):
    return pl.BlockSpec((None,) + tuple(block), lambda *a: (layer,) + tuple(index_map(*a)))


def _matmul(x, w, layer, *, first_tile, n_tiles, skip_from=None, skip=0, lane_tile_major=False,
            tm=1024, tn=1024, out_dtype=BF16):
    m, k = x.shape

    def col_tile(j):
        return first_tile + j if skip_from is None else first_tile + j + jnp.where(j >= skip_from, skip, 0)

    if lane_tile_major:
        out_spec = pl.BlockSpec((None, tn // LANES, tm, LANES), lambda j, i: (j, 0, i, 0))
        out_shape = jax.ShapeDtypeStruct((n_tiles, tn // LANES, m, LANES), out_dtype)
    else:
        out_spec = pl.BlockSpec((tm, tn), lambda j, i: (i, j))
        out_shape = jax.ShapeDtypeStruct((m, n_tiles * tn), out_dtype)
    return pl.pallas_call(
        functools.partial(_mm_kernel, layer=layer, tn=tn, col_tile=col_tile, lane_tile_major=lane_tile_major),
        grid=(n_tiles, m // tm),
        in_specs=[pl.BlockSpec((tm, k), lambda j, i: (i, 0)), pl.BlockSpec(memory_space=pl.ANY)],
        out_specs=out_spec,
        out_shape=out_shape,
        scratch_shapes=[pltpu.VMEM((k, tn), F32), pltpu.VMEM((k, tn), BF16), pltpu.SemaphoreType.DMA],
        compiler_params=_params("arbitrary", "arbitrary"),
        name="matmul_tiles" if lane_tile_major else "matmul",
    )(x, w)


def _mm_residual_kernel(h_ref, x_ref, w_ref, o_ref):
    o_ref[...] = h_ref[...] + _dot(x_ref[...], w_ref[...])


def _matmul_residual(h, x, w, layer, *, in_place, tm=1024, tn=512):
    m, k = x.shape
    n = w.shape[2]
    return pl.pallas_call(
        _mm_residual_kernel,
        grid=(m // tm, n // tn),
        in_specs=[
            pl.BlockSpec((tm, tn), lambda i, j: (i, j)),
            pl.BlockSpec((tm, k), lambda i, j: (i, 0)),
            _layer_spec(layer, (k, tn), lambda i, j: (0, j)),
        ],
        out_specs=pl.BlockSpec((tm, tn), lambda i, j: (i, j)),
        out_shape=jax.ShapeDtypeStruct((m, n), F32),
        input_output_aliases={0: 0} if in_place else {},
        compiler_params=_params("parallel", "arbitrary"),
        name="matmul_residual",
    )(h, x, w)


def _ple_kernel(h_ref, x_ref, wg_ref, p_ref, we_ref, o_ref):
    gate = jax.nn.sigmoid(_dot(x_ref[...], wg_ref[...]))
    emb = _dot(p_ref[...], we_ref[...])
    o_ref[...] = h_ref[...] + gate * emb


def _ple(h, xn, w_gate, p, w_ple, layer, *, tm=1024, tn=512):
    m, k = xn.shape
    n = w_gate.shape[2]
    kp = p.shape[2]
    return pl.pallas_call(
        _ple_kernel,
        grid=(m // tm, n // tn),
        in_specs=[
            pl.BlockSpec((tm, tn), lambda i, j: (i, j)),
            pl.BlockSpec((tm, k), lambda i, j: (i, 0)),
            _layer_spec(layer, (k, tn), lambda i, j: (0, j)),
            _layer_spec(layer, (tm, kp), lambda i, j: (i, 0)),
            _layer_spec(layer, (kp, tn), lambda i, j: (0, j)),
        ],
        out_specs=pl.BlockSpec((tm, tn), lambda i, j: (i, j)),
        out_shape=jax.ShapeDtypeStruct((m, n), F32),
        input_output_aliases={0: 0},
        compiler_params=_params("parallel", "arbitrary"),
        name="ple",
    )(h, xn, w_gate, p, w_ple)


def _merge_kernel(yl_ref, ya_ref, ys_ref, pl_ref, pa_ref, ps_ref, gl_ref, ga_ref, gs_ref, o_ref):
    acc = jax.nn.sigmoid(gl_ref[...].astype(F32)) * _dot(yl_ref[...], pl_ref[...])
    acc += jax.nn.sigmoid(ga_ref[...].astype(F32)) * _dot(ya_ref[...], pa_ref[...])
    acc += jax.nn.sigmoid(gs_ref[...].astype(F32)) * _dot(ys_ref[...], ps_ref[...])
    o_ref[...] = acc.astype(o_ref.dtype)


def _gated_merge(y_lru, y_att, y_ssm, p_lru, p_att, p_ssm, layer, proj, gate_off, *, tm=1024, tn=512):
    m = y_lru.shape[0]
    d = p_lru.shape[2]
    goff = gate_off // tn
    nd = d // tn

    def y_spec(y):
        return pl.BlockSpec((tm, y.shape[1]), lambda i, j: (i, 0))

    def p_spec(p):
        return _layer_spec(layer, (p.shape[1], tn), lambda i, j: (0, j))

    def g_spec(b):
        return pl.BlockSpec((tm, tn), lambda i, j: (i, goff + b * nd + j))

    return pl.pallas_call(
        _merge_kernel,
        grid=(m // tm, nd),
        in_specs=[y_spec(y_lru), y_spec(y_att), y_spec(y_ssm), p_spec(p_lru), p_spec(p_att), p_spec(p_ssm),
                  g_spec(0), g_spec(1), g_spec(2)],
        out_specs=pl.BlockSpec((tm, tn), lambda i, j: (i, j)),
        out_shape=jax.ShapeDtypeStruct((m, d), BF16),
        compiler_params=_params("parallel", "arbitrary"),
        name="gated_merge",
    )(y_lru, y_att, y_ssm, p_lru, p_att, p_ssm, proj, proj, proj)


def _scan_rows8(a8, x8, hprev, row):
    for k in (1, 2, 4):
        keep = row >= k
        a_sh = jnp.where(keep, pltpu.roll(a8, k, 0), 1.0)
        x_sh = jnp.where(keep, pltpu.roll(x8, k, 0), 0.0)
        x8 = a8 * x_sh + x8
        a8 = a8 * a_sh
    return a8 * hprev + x8


def _lru_kernel(u_ref, cw_ref, cb_ref, wr_ref, br_ref, wi_ref, bi_ref, lam_ref, y_ref,
                tail_ref, h_ref, a_s, x_s, *, t_rows, n_blocks, block, conv_width):
    @pl.when(pl.program_id(1) == 0)
    def _():
        tail_ref[...] = jnp.zeros_like(tail_ref)
        h_ref[...] = jnp.zeros_like(h_ref)

    u = u_ref[0].astype(F32)
    ue = jnp.concatenate([tail_ref[...], u], axis=0)
    xc = cb_ref[...]
    for j in range(conv_width):
        off = 8 - (conv_width - 1) + j
        xc = xc + cw_ref[j:j + 1, :] * ue[off:off + t_rows, :]
    tail_ref[...] = u[t_rows - 8:, :]

    xcb = xc.astype(BF16)
    rs, igs = [], []
    for hb in range(n_blocks):
        xb = xcb[:, hb * block:(hb + 1) * block]
        rs.append(_dot(xb, wr_ref[hb]))
        igs.append(_dot(xb, wi_ref[hb]))
    r = jax.nn.sigmoid(jnp.concatenate(rs, axis=1) + br_ref[...])
    ig = jax.nn.sigmoid(jnp.concatenate(igs, axis=1) + bi_ref[...])
    lam = lam_ref[...]
    softplus_neg_lam = jnp.maximum(-lam, 0.0) + jnp.log1p(jnp.exp(-jnp.abs(lam)))
    log_a = (-LRU_C) * r * softplus_neg_lam
    a = jnp.exp(log_a)
    a_s[...] = a
    x_s[...] = jnp.sqrt(-jnp.tanh(log_a) * (a * a + 1.0)) * (ig * xc)

    w = a_s.shape[1]
    row = lax.broadcasted_iota(jnp.int32, (8, w), 0)

    def body(i, hprev):
        r0 = pl.multiple_of(i * 16, 16)
        h_a = _scan_rows8(a_s[pl.ds(r0, 8), :], x_s[pl.ds(r0, 8), :], hprev, row)
        h_b = _scan_rows8(a_s[pl.ds(r0 + 8, 8), :], x_s[pl.ds(r0 + 8, 8), :], h_a[7:8, :], row)
        y_ref[0, pl.ds(r0, 16), :] = jnp.concatenate([h_a, h_b], axis=0).astype(y_ref.dtype)
        return h_b[7:8, :]

    h_ref[...] = lax.fori_loop(0, t_rows // 16, body, h_ref[...])


def _rglru(proj3, conv_w, conv_b, w_rg, b_rg, w_ig, b_ig, lam, *, t_rows=256):
    bsz, seq, _ = proj3.shape
    n_blocks, block, _ = w_rg.shape
    w = n_blocks * block
    cw = conv_w.shape[0]
    vec = lambda v: v.reshape(1, w).astype(F32)
    full2 = lambda r, c: pl.BlockSpec((r, c), lambda b, t: (0, 0))
    full3 = pl.BlockSpec((n_blocks, block, block), lambda b, t: (0, 0, 0))
    return pl.pallas_call(
        functools.partial(_lru_kernel, t_rows=t_rows, n_blocks=n_blocks, block=block, conv_width=cw),
        grid=(bsz, seq // t_rows),
        in_specs=[pl.BlockSpec((1, t_rows, w), lambda b, t: (b, t, 0)), full2(cw, w), full2(1, w),
                  full3, full2(1, w), full3, full2(1, w), full2(1, w)],
        out_specs=pl.BlockSpec((1, t_rows, w), lambda b, t: (b, t, 0)),
        out_shape=jax.ShapeDtypeStruct((bsz, seq, w), BF16),
        scratch_shapes=[pltpu.VMEM((8, w), F32), pltpu.VMEM((1, w), F32),
                        pltpu.VMEM((t_rows, w), F32), pltpu.VMEM((t_rows, w), F32)],
        compiler_params=_params("parallel", "arbitrary"),
        name="rglru",
    )(proj3, conv_w.astype(F32), vec(conv_b), w_rg.astype(BF16), vec(b_rg), w_ig.astype(BF16), vec(b_ig),
      vec(lam))


def _head_rms(x, gain):
    x = x.astype(F32)
    return x * lax.rsqrt(jnp.mean(x * x, axis=-1, keepdims=True) + EPS) * gain


def _attn_kernel(q_ref, kp_ref, kc_ref, vp_ref, vc_ref, bvec_ref, qg_ref, kg_ref, o_ref, bias_s, *, qb):
    first = pl.program_id(2) == 0

    @pl.when(first)
    def _():
        vec = jnp.broadcast_to(bvec_ref[0], (qb, 2 * qb))
        table = pltpu.roll(vec, 0, 1, stride=1, stride_axis=0)
        q_chunk = (lax.broadcasted_iota(jnp.int32, table.shape, 0) >> CHUNK_SHIFT) + (qb >> CHUNK_SHIFT)
        k_chunk = lax.broadcasted_iota(jnp.int32, table.shape, 1) >> CHUNK_SHIFT
        in_band = jnp.logical_and(k_chunk >= q_chunk - LEFT_CHUNKS, k_chunk <= q_chunk)
        bias_s[...] = jnp.where(in_band, table, NEG_INF)

    qn = (_head_rms(q_ref[...], qg_ref[...]) * (HEAD_DIM ** -0.5)).astype(BF16)
    kn = _head_rms(jnp.concatenate([kp_ref[...], kc_ref[...]], axis=0), kg_ref[...]).astype(BF16)
    v = jnp.concatenate([vp_ref[...], vc_ref[...]], axis=0)
    sub = qb // ATTN_SPLIT
    span = sub + LEFT_CHUNKS * CHUNK
    outs = []
    for part in range(ATTN_SPLIT):
        r0 = part * sub
        c0 = r0 + qb - LEFT_CHUNKS * CHUNK
        s = lax.dot_general(qn[r0:r0 + sub], kn[c0:c0 + span], (((1,), (1,)), ((), ())),
                            preferred_element_type=F32)
        s = s + bias_s[r0:r0 + sub, c0:c0 + span]
        col = c0 + lax.broadcasted_iota(jnp.int32, s.shape, 1)
        s = jnp.where(jnp.logical_and(first, col < qb), NEG_INF, s)
        p = jnp.exp(s - jnp.max(s, axis=-1, keepdims=True))
        denom = jnp.sum(p, axis=-1, keepdims=True)
        outs.append(_dot(p.astype(BF16), v[c0:c0 + span]) / denom)
    o_ref[...] = jnp.concatenate(outs, axis=0).astype(o_ref.dtype)


def _bias_vector(rel_bias, qb):
    mm = jnp.arange(2 * qb)
    j_minus_i = jnp.where(mm < 2 * qb - CHUNK, mm, mm - 2 * qb)
    dist = qb - j_minus_i
    return rel_bias.astype(F32)[:, None, jnp.clip(dist, -REL_CLIP, REL_CLIP) + REL_CLIP]


def _attention(qkv, bsz, q_gain, k_gain, rel_bias, *, qb=ATTN_QBLOCK):
    assert qb % CHUNK == 0 and qb >= LEFT_CHUNKS * CHUNK and (LEFT_CHUNKS + 2) * CHUNK <= 2 * qb
    assert (qb // ATTN_SPLIT) % LANES == 0 and (qb - LEFT_CHUNKS * CHUNK) % LANES == 0 and LANES % CHUNK == 0
    _, n_heads, m, _ = qkv.shape
    nblk = m // bsz // qb
    cur = lambda sec: pl.BlockSpec((None, None, qb, HEAD_DIM), lambda b, h, n: (sec, h, b * nblk + n, 0))
    prev = lambda sec: pl.BlockSpec((None, None, qb, HEAD_DIM),
                                    lambda b, h, n: (sec, h, b * nblk + jnp.maximum(n - 1, 0), 0))
    gain = pl.BlockSpec((1, HEAD_DIM), lambda b, h, n: (0, 0))
    return pl.pallas_call(
        functools.partial(_attn_kernel, qb=qb),
        grid=(bsz, n_heads, nblk),
        in_specs=[cur(0), prev(1), cur(1), prev(2), cur(2),
                  pl.BlockSpec((1, 1, 2 * qb), lambda b, h, n: (h, 0, 0)), gain, gain],
        out_specs=pl.BlockSpec((qb, HEAD_DIM), lambda b, h, n: (b * nblk + n, h)),
        out_shape=jax.ShapeDtypeStruct((m, n_heads * HEAD_DIM), BF16),
        scratch_shapes=[pltpu.VMEM((qb, 2 * qb), F32)],
        compiler_params=_params("parallel", "parallel", "arbitrary"),
        name="band_attention",
    )(qkv, qkv, qkv, qkv, qkv, _bias_vector(rel_bias, qb),
      q_gain.reshape(1, HEAD_DIM).astype(F32), k_gain.reshape(1, HEAD_DIM).astype(F32))


def _s5_tables(a_re, a_im, log_dt, b_re, b_im, c_re, c_im, d_skip, t_chunk):
    hi = lax.Precision.HIGHEST
    g, p = a_re.shape
    c = b_re.shape[-1]
    gl = LANES // c
    nj = g // gl
    dt = jnp.exp(log_dt.astype(F32))[:, None]
    ar, ai = a_re.astype(F32), a_im.astype(F32)

    def apow(tau):
        tau = jnp.asarray(tau, F32)[:, None, None]
        mag = jnp.exp(tau * dt * ar)
        return mag * jnp.cos(tau * dt * ai), mag * jnp.sin(tau * dt * ai)

    pr, pi = apow(jnp.arange(t_chunk + 1))
    abar_re, abar_im = pr[1], pi[1]
    den = ar * ar + ai * ai
    nr, ni = abar_re - 1.0, abar_im
    coef_re = (nr * ar + ni * ai) / den
    coef_im = (ni * ar - nr * ai) / den
    br, bi = b_re.astype(F32), b_im.astype(F32)
    bbar_re = coef_re[..., None] * br - coef_im[..., None] * bi
    bbar_im = coef_re[..., None] * bi + coef_im[..., None] * br
    cr, ci = c_re.astype(F32), c_im.astype(F32)
    ca_re = cr[None] * pr[:, :, None, :] - ci[None] * pi[:, :, None, :]
    ca_im = cr[None] * pi[:, :, None, :] + ci[None] * pr[:, :, None, :]
    lane = jnp.arange(LANES)
    col = jnp.arange(gl * p)
    rep_c = (lane[None, :] % c == jnp.arange(c)[:, None]).astype(BF16)
    rep_p = (col[None, :] % p == jnp.arange(p)[:, None]).astype(BF16)
    lane_lane = lane[:, None] // c == lane[None, :] // c
    lane_col = lane[:, None] // c == col[None, :] // p
    col_lane = col[:, None] // p == lane[None, :] // c

    k_tap = (jnp.einsum("tgcp,gpd->tgcd", ca_re[:t_chunk], bbar_re, precision=hi)
             - jnp.einsum("tgcp,gpd->tgcd", ca_im[:t_chunk], bbar_im, precision=hi))
    lag = jnp.arange(t_chunk)[None, :] - jnp.arange(t_chunk)[:, None]
    k_st = jnp.where((lag >= 0)[:, :, None, None, None], k_tap[jnp.maximum(lag, 0)], 0.0)
    k_rows = jnp.transpose(k_st.reshape(t_chunk, t_chunk, nj, gl, c, c), (2, 0, 1, 3, 5, 4))
    k_rep = jnp.dot(k_rows.reshape(-1, c).astype(BF16), rep_c).reshape(nj, t_chunk, t_chunk, LANES, LANES)
    k_rep = jnp.where(lane_lane[None, None, None], k_rep, 0)
    k_mat = jnp.transpose(k_rep, (0, 1, 3, 2, 4)).reshape(nj, t_chunk * LANES, t_chunk * LANES)

    rev = t_chunk - 1 - jnp.arange(t_chunk)
    s_re = pr[rev][..., None] * bbar_re[None] - pi[rev][..., None] * bbar_im[None]
    s_im = pr[rev][..., None] * bbar_im[None] + pi[rev][..., None] * bbar_re[None]

    def inc_mat(x):
        rows = jnp.transpose(x.reshape(t_chunk, nj, gl, p, c), (1, 0, 2, 4, 3))
        rep = jnp.dot(rows.reshape(-1, p).astype(BF16), rep_p).reshape(nj, t_chunk, LANES, gl * p)
        return jnp.where(lane_col[None, None], rep, 0).reshape(nj, t_chunk * LANES, gl * p)

    w_in = jnp.concatenate([k_mat, inc_mat(s_re), inc_mat(s_im)], axis=-1)

    def out_mat(x):
        rows = jnp.transpose(x.reshape(t_chunk, nj, gl, c, p), (1, 2, 4, 0, 3))
        rep = jnp.dot(rows.reshape(-1, c).astype(BF16), rep_c).reshape(nj, gl * p, t_chunk, LANES)
        return jnp.where(col_lane[None, :, None, :], rep, 0).reshape(nj, gl * p, t_chunk * LANES)

    w_state = jnp.concatenate([out_mat(ca_re[1:]), out_mat(-ca_im[1:])], axis=1)

    qr, qi = apow(t_chunk * jnp.arange(1, 9))
    lanes = lambda x: jnp.transpose(x.reshape(8, nj, gl * p), (1, 0, 2))
    consts = jnp.concatenate([lanes(qr), lanes(qi)], axis=1)
    d_row = jnp.tile(d_skip.astype(F32).reshape(nj, 1, LANES), (1, 1, t_chunk))
    return w_in, w_state, consts, d_row


def _gelu_tanh(y):
    return 0.5 * y * (1.0 + jnp.tanh(math.sqrt(2.0 / math.pi) * (y + 0.044715 * (y * y * y))))


def _s5_kernel(x_ref, win_ref, wst_ref, cst_ref, d_ref, y_ref, stage, inc_re, inc_im, hp_re, hp_im,
               *, t_chunk, n_rows, tc, gp):
    stage[...] = x_ref[...].astype(F32)
    u = jnp.concatenate([stage[pl.ds(s, n_rows, stride=t_chunk), :] for s in range(t_chunk)], axis=1)
    z = _dot(u.astype(BF16), win_ref[0])
    inc_re[...] = z[:, tc:tc + gp]
    inc_im[...] = z[:, tc + gp:]
    cst = cst_ref[0]
    c_re, c_im = cst[0:8, :], cst[8:16, :]
    row = lax.broadcasted_iota(jnp.int32, (8, gp), 0)

    def body(i, carry):
        h_re, h_im = carry
        r0 = pl.multiple_of(i * 8, 8)
        x_re = inc_re[pl.ds(r0, 8), :]
        x_im = inc_im[pl.ds(r0, 8), :]
        for k in (1, 2, 4):
            keep = row >= k
            s_re = jnp.where(keep, pltpu.roll(x_re, k, 0), 0.0)
            s_im = jnp.where(keep, pltpu.roll(x_im, k, 0), 0.0)
            m_re, m_im = c_re[k - 1:k, :], c_im[k - 1:k, :]
            x_re, x_im = x_re + m_re * s_re - m_im * s_im, x_im + m_re * s_im + m_im * s_re
        o_re = x_re + c_re * h_re - c_im * h_im
        o_im = x_im + c_re * h_im + c_im * h_re
        first = row >= 1
        hp_re[pl.ds(r0, 8), :] = jnp.where(first, pltpu.roll(o_re, 1, 0), h_re)
        hp_im[pl.ds(r0, 8), :] = jnp.where(first, pltpu.roll(o_im, 1, 0), h_im)
        return o_re[7:8, :], o_im[7:8, :]

    zero = jnp.zeros((1, gp), F32)
    lax.fori_loop(0, n_rows // 8, body, (zero, zero))
    h_prev = jnp.concatenate([hp_re[...], hp_im[...]], axis=1).astype(BF16)
    y = _gelu_tanh(z[:, :tc] + _dot(h_prev, wst_ref[0]) + d_ref[0] * u)
    for t in range(t_chunk):
        stage[pl.ds(t, n_rows, stride=t_chunk), :] = y[:, t * LANES:(t + 1) * LANES]
    y_ref[...] = stage[...].astype(y_ref.dtype)


def _s5(tiles, section, bsz, tables, *, t_chunk=S5_CHUNK):
    w_in, w_state, consts, d_row = tables
    _, nj, m, _ = tiles.shape
    seq = m // bsz
    n_rows = seq // t_chunk
    tc = t_chunk * LANES
    gp = w_state.shape[1] // 2
    return pl.pallas_call(
        functools.partial(_s5_kernel, t_chunk=t_chunk, n_rows=n_rows, tc=tc, gp=gp),
        grid=(nj, bsz),
        in_specs=[
            pl.BlockSpec((None, None, seq, LANES), lambda j, b: (section, j, b, 0)),
            pl.BlockSpec((1, tc, tc + 2 * gp), lambda j, b: (j, 0, 0)),
            pl.BlockSpec((1, 2 * gp, tc), lambda j, b: (j, 0, 0)),
            pl.BlockSpec((1, 16, gp), lambda j, b: (j, 0, 0)),
            pl.BlockSpec((1, 1, tc), lambda j, b: (j, 0, 0)),
        ],
        out_specs=pl.BlockSpec((None, seq, LANES), lambda j, b: (j, b, 0)),
        out_shape=jax.ShapeDtypeStruct((nj, m, LANES), BF16),
        scratch_shapes=[pltpu.VMEM((seq, LANES), F32)] + [pltpu.VMEM((n_rows, gp), F32) for _ in range(4)],
        compiler_params=_params("parallel", "parallel"),
        name="s5",
    )(tiles, w_in, w_state, consts, d_row)


def _glu_kernel(*refs, nj, n):
    x = jnp.concatenate([r[...] for r in refs[:nj]], axis=1)
    w_ref, b_ref, o_ref = refs[nj:]
    z = _dot(x, w_ref[...]) + b_ref[...]
    o_ref[...] = (z[:, :n] * jax.nn.sigmoid(z[:, n:])).astype(o_ref.dtype)


def _glu(x_tiles, w, layer, b, *, tm=1024):
    nj, m, _ = x_tiles.shape
    k, n2 = w.shape[1], w.shape[2]
    n = n2 // 2
    return pl.pallas_call(
        functools.partial(_glu_kernel, nj=nj, n=n),
        grid=(m // tm,),
        in_specs=[pl.BlockSpec((None, tm, LANES), functools.partial(lambda i, j: (j, i, 0), j=j))
                  for j in range(nj)] + [
            _layer_spec(layer, (k, n2), lambda i: (0, 0)),
            pl.BlockSpec((1, n2), lambda i: (0, 0)),
        ],
        out_specs=pl.BlockSpec((tm, n), lambda i: (i, 0)),
        out_shape=jax.ShapeDtypeStruct((m, n), BF16),
        compiler_params=_params("parallel"),
        name="glu",
    )(*([x_tiles] * nj), w, b.reshape(1, n2).astype(F32))


def _route_kernel(h_ref, g_ref, w_ref, b_ref, xp_ref, info_ref, *, n_groups, per_group):
    x = h_ref[...]
    xn = x * lax.rsqrt(jnp.mean(x * x, axis=-1, keepdims=True) + EPS) * g_ref[...]
    xp_ref[...] = _pack_halves(xn)
    n_exp = n_groups * per_group
    logits = _dot(xn.astype(BF16), w_ref[...]) + b_ref[...]
    lane = lax.broadcasted_iota(jnp.int32, logits.shape, 1).astype(F32)
    big = float(LANES)
    is_group = jnp.logical_and(lane >= n_exp, lane < n_exp + n_groups)
    gl = jnp.where(is_group, logits, -jnp.inf)
    gmax = jnp.max(gl, axis=-1, keepdims=True)
    gsel = jnp.min(jnp.where(gl == gmax, lane, big), axis=-1, keepdims=True) - n_exp
    gprob = 1.0 / jnp.sum(jnp.where(is_group, jnp.exp(logits - gmax), 0.0), axis=-1, keepdims=True)
    in_group = jnp.logical_and(lane >= gsel * per_group, lane < (gsel + 1.0) * per_group)
    el = jnp.where(in_group, logits, -jnp.inf)
    v1 = jnp.max(el, axis=-1, keepdims=True)
    i1 = jnp.min(jnp.where(el == v1, lane, big), axis=-1, keepdims=True)
    el2 = jnp.where(lane == i1, -jnp.inf, el)
    v2 = jnp.max(el2, axis=-1, keepdims=True)
    i2 = jnp.min(jnp.where(el2 == v2, lane, big), axis=-1, keepdims=True)
    e2 = jnp.exp(v2 - v1)
    w1 = gprob / (1.0 + e2)
    w2 = gprob * e2 / (1.0 + e2)
    info_ref[...] = (jnp.where(lane == 0.0, i1, 0.0) + jnp.where(lane == 1.0, i2, 0.0)
                     + jnp.where(lane == 2.0, w1, 0.0) + jnp.where(lane == 3.0, w2, 0.0))


def _route(h, gain, w_gr, b_gr, w_er, b_er, *, tm=512):
    m, d = h.shape
    n_groups, _, per_group = w_er.shape
    n_exp = n_groups * per_group
    assert n_exp + n_groups <= LANES
    w = jnp.concatenate([jnp.transpose(w_er, (1, 0, 2)).reshape(d, n_exp), w_gr], axis=1)
    w = jnp.pad(w, ((0, 0), (0, LANES - n_exp - n_groups))).astype(BF16)
    b = jnp.concatenate([b_er.reshape(n_exp), b_gr]).astype(F32)
    b = jnp.pad(b, (0, LANES - n_exp - n_groups)).reshape(1, LANES)
    return pl.pallas_call(
        functools.partial(_route_kernel, n_groups=n_groups, per_group=per_group),
        grid=(m // tm,),
        in_specs=[pl.BlockSpec((tm, d), lambda i: (i, 0)), pl.BlockSpec((1, d), lambda i: (0, 0)),
                  pl.BlockSpec((d, LANES), lambda i: (0, 0)), pl.BlockSpec((1, LANES), lambda i: (0, 0))],
        out_specs=[pl.BlockSpec((tm, d // 2), lambda i: (i, 0)), pl.BlockSpec((tm, LANES), lambda i: (i, 0))],
        out_shape=[jax.ShapeDtypeStruct((m, d // 2), U32), jax.ShapeDtypeStruct((m, LANES), F32)],
        compiler_params=_params("parallel"),
        name="moe_route",
    )(h, gain.reshape(1, d).astype(F32), w, b)


def _sorted_layout(expert_ids, n_exp, row_tile, n_tiles):
    e = expert_ids.reshape(-1)
    blk = LANES
    nb = e.shape[0] // blk
    onehot = (e[:, None] == jnp.arange(n_exp, dtype=jnp.int32)[None, :]).reshape(nb, blk, n_exp)
    tri = jnp.tril(jnp.ones((blk, blk), BF16))
    within = jnp.einsum("ij,bjk->bik", tri, onehot.astype(BF16), preferred_element_type=F32)
    totals = within[:, -1, :]
    before = jnp.dot(jnp.tril(jnp.ones((nb, nb), F32), -1), totals, precision=lax.Precision.HIGHEST)
    counts = (before[-1] + totals[-1]).astype(jnp.int32)
    rank = jnp.sum(jnp.where(onehot, within + before[:, None, :], 0.0), axis=-1).reshape(-1).astype(jnp.int32) - 1
    padded = ((counts + row_tile - 1) // row_tile) * row_tile
    ends = jnp.cumsum(padded)
    starts = ends - padded
    pos = starts[e] + rank
    tile_start = jnp.arange(n_tiles, dtype=jnp.int32) * row_tile
    tile_expert = jnp.minimum(jnp.sum((tile_start[:, None] >= ends[None, :]).astype(jnp.int32), axis=1), n_exp - 1)
    n_used = (ends[-1] // row_tile).astype(jnp.int32).reshape(1)
    return pos.astype(jnp.int32), tile_expert.astype(jnp.int32), n_used


def _row_copy(src_ref, src_row, dst_ref, dst_row, sem):
    return pltpu.make_async_copy(src_ref.at[pl.ds(src_row, 1)], dst_ref.at[pl.ds(dst_row, 1)], sem)


def _start_row_copies(n_rows, start_row):
    def body(t, c):
        start_row(t)
        return c

    lax.fori_loop(0, n_rows, body, 0, unroll=ROW_DMA_UNROLL)


def _wait_row_copies(n_copies, example_copy):
    for _ in range(n_copies):
        example_copy.wait()


def _dispatch_kernel(pos_ref, x_ref, init_ref, out_ref, sem, *, tt, n_slots):
    del init_ref

    def start_row(t):
        for k in range(n_slots):
            _row_copy(x_ref, t, out_ref, pos_ref[0, 0, n_slots * t + k], sem).start(priority=k % 2)

    _start_row_copies(tt, start_row)
    _wait_row_copies(tt * n_slots, _row_copy(x_ref, 0, out_ref, 0, sem))


def _dispatch(xp, pos, n_rows, *, tt=MOE_TOKEN_TILE, n_slots=TOP_K):
    m, d2 = xp.shape
    pos3 = pos.reshape(m // tt, 1, tt * n_slots)
    return pl.pallas_call(
        functools.partial(_dispatch_kernel, tt=tt, n_slots=n_slots),
        grid=(m // tt,),
        in_specs=[pl.BlockSpec((1, 1, tt * n_slots), lambda i: (i, 0, 0), memory_space=pltpu.SMEM),
                  pl.BlockSpec((tt, d2), lambda i: (i, 0)),
                  pl.BlockSpec(memory_space=pl.ANY)],
        out_specs=pl.BlockSpec(memory_space=pl.ANY),
        out_shape=jax.ShapeDtypeStruct((n_rows, d2), U32),
        scratch_shapes=[pltpu.SemaphoreType.DMA],
        input_output_aliases={2: 0},
        compiler_params=_params("arbitrary"),
        name="moe_dispatch",
    )(pos3, xp, jnp.zeros((n_rows, d2), U32))


def _expert_kernel(te_ref, first_ref, next_ref, nu_ref, x_ref, wg_hbm, wu_hbm, wd_hbm, o_ref,
                   stage_g, stage_u, stage_d, wg_s, wu_s, wd_s, sem, *, layer):
    t = pl.program_id(0)
    in_use = t < nu_ref[0]
    copies = ((wg_hbm, stage_g), (wu_hbm, stage_u), (wd_hbm, stage_d))

    def weight_copy(k, expert):
        src, dst = copies[k]
        return pltpu.make_async_copy(src.at[layer, expert], dst, sem.at[k])

    @pl.when(t == 0)
    def _():
        for k in range(len(copies)):
            weight_copy(k, te_ref[0]).start()

    @pl.when(jnp.logical_and(in_use, first_ref[t] == 1))
    def _():
        for k, dst in enumerate((wg_s, wu_s, wd_s)):
            weight_copy(k, 0).wait()
            dst[...] = copies[k][1][...].astype(BF16)

        @pl.when(next_ref[t] >= 0)
        def _():
            for k in range(len(copies)):
                weight_copy(k, next_ref[t]).start()

    @pl.when(in_use)
    def _():
        lo, hi = _unpack_halves(x_ref[...])
        x = jnp.concatenate([lo, hi], axis=1).astype(BF16)
        g = _dot(x, wg_s[...])
        u = _dot(x, wu_s[...])
        hid = (g * jax.nn.sigmoid(g) * u).astype(BF16)
        o_ref[...] = _pack_halves(_dot(hid, wd_s[...]))

    @pl.when(jnp.logical_not(in_use))
    def _():
        o_ref[...] = jnp.zeros_like(o_ref)


def _expert_ffn(xs, tile_expert, n_used, w_gate, w_up, w_down, layer, *, tm=MOE_ROW_TILE):
    rows, d2 = xs.shape
    _, _, d, ff = w_gate.shape
    n_tiles = rows // tm
    idx = jnp.arange(n_tiles, dtype=jnp.int32)
    used = idx < n_used[0]
    first = jnp.logical_and(used, jnp.logical_or(idx == 0, tile_expert != jnp.roll(tile_expert, 1)))
    first_pos = jnp.where(first, idx, n_tiles)
    later = jnp.concatenate([lax.cummin(first_pos[::-1])[::-1][1:], jnp.full((1,), n_tiles, jnp.int32)])
    next_expert = jnp.where(later < n_tiles, tile_expert[jnp.minimum(later, n_tiles - 1)], -1).astype(jnp.int32)
    any_spec = pl.BlockSpec(memory_space=pl.ANY)
    grid_spec = pltpu.PrefetchScalarGridSpec(
        num_scalar_prefetch=4,
        grid=(n_tiles,),
        in_specs=[pl.BlockSpec((tm, d2), lambda t, *_: (t, 0)), any_spec, any_spec, any_spec],
        out_specs=pl.BlockSpec((tm, d2), lambda t, *_: (t, 0)),
        scratch_shapes=[pltpu.VMEM((d, ff), F32), pltpu.VMEM((d, ff), F32), pltpu.VMEM((ff, d), F32),
                        pltpu.VMEM((d, ff), BF16), pltpu.VMEM((d, ff), BF16), pltpu.VMEM((ff, d), BF16),
                        pltpu.SemaphoreType.DMA((3,))],
    )
    return pl.pallas_call(
        functools.partial(_expert_kernel, layer=layer),
        grid_spec=grid_spec,
        out_shape=jax.ShapeDtypeStruct((rows, d2), U32),
        compiler_params=_params("arbitrary"),
        name="moe_expert_ffn",
    )(tile_expert, first.astype(jnp.int32), next_expert, n_used, xs, w_gate, w_up, w_down)


def _combine_kernel(pos_ref, pos_next_ref, info_ref, h_ref, g_ref, y_ref, o_ref, on_ref, buf, sem, *, tt, n_slots):
    i = pl.program_id(0)
    cur = i % 2

    def gather(p_ref, half):
        def start_row(t):
            for k in range(n_slots):
                _row_copy(y_ref, p_ref[0, 0, n_slots * t + k], buf.at[half, k], t, sem.at[half]).start(priority=k % 2)

        _start_row_copies(tt, start_row)

    @pl.when(i == 0)
    def _():
        gather(pos_ref, 0)

    @pl.when(i + 1 < pl.num_programs(0))
    def _():
        gather(pos_next_ref, 1 - cur)

    _wait_row_copies(tt * n_slots, _row_copy(y_ref, 0, buf.at[cur, 0], 0, sem.at[cur]))
    info = info_ref[...]
    lane = lax.broadcasted_iota(jnp.int32, info.shape, 1)
    lo_acc = hi_acc = None
    for k in range(n_slots):
        wk = jnp.sum(jnp.where(lane == n_slots + k, info, 0.0), axis=-1, keepdims=True)
        lo, hi = _unpack_halves(buf[cur, k])
        lo_acc = wk * lo if lo_acc is None else lo_acc + wk * lo
        hi_acc = wk * hi if hi_acc is None else hi_acc + wk * hi
    out = h_ref[...] + jnp.concatenate([lo_acc, hi_acc], axis=1)
    o_ref[...] = out
    ms = jnp.mean(out * out, axis=-1, keepdims=True)
    on_ref[...] = (out * lax.rsqrt(ms + EPS) * g_ref[...]).astype(on_ref.dtype)


def _combine(h, info, ys, pos, next_gain, *, tt=MOE_TOKEN_TILE, n_slots=TOP_K):
    m, d = h.shape
    d2 = ys.shape[1]
    n_steps = m // tt
    pos3 = pos.reshape(n_steps, 1, tt * n_slots)
    return pl.pallas_call(
        functools.partial(_combine_kernel, tt=tt, n_slots=n_slots),
        grid=(n_steps,),
        in_specs=[pl.BlockSpec((1, 1, tt * n_slots), lambda i: (i, 0, 0), memory_space=pltpu.SMEM),
                  pl.BlockSpec((1, 1, tt * n_slots), lambda i: (jnp.minimum(i + 1, n_steps - 1), 0, 0),
                               memory_space=pltpu.SMEM),
                  pl.BlockSpec((tt, LANES), lambda i: (i, 0)),
                  pl.BlockSpec((tt, d), lambda i: (i, 0)),
                  pl.BlockSpec((1, d), lambda i: (0, 0)),
                  pl.BlockSpec(memory_space=pl.ANY)],
        out_specs=[pl.BlockSpec((tt, d), lambda i: (i, 0)), pl.BlockSpec((tt, d), lambda i: (i, 0))],
        out_shape=[jax.ShapeDtypeStruct((m, d), F32), jax.ShapeDtypeStruct((m, d), BF16)],
        scratch_shapes=[pltpu.VMEM((2, n_slots, tt, d2), U32), pltpu.SemaphoreType.DMA((2,))],
        input_output_aliases={3: 0},
        compiler_params=_params("arbitrary"),
        name="moe_combine",
    )(pos3, pos3, info, h, next_gain.reshape(1, d).astype(F32), ys)


def _moe(h, gain, w_gr, b_gr, w_er, b_er, w_gate, w_up, w_down, layer, next_gain):
    m = h.shape[0]
    n_exp = w_gate.shape[1]
    n_tiles = -(-(m * TOP_K + n_exp * (MOE_ROW_TILE - 1)) // MOE_ROW_TILE)
    xp, info = _route(h, gain, w_gr, b_gr, w_er, b_er)
    expert_ids = info[:, :TOP_K].astype(jnp.int32)
    pos, tile_expert, n_used = _sorted_layout(expert_ids, n_exp, MOE_ROW_TILE, n_tiles)
    xs = _dispatch(xp, pos, n_tiles * MOE_ROW_TILE)
    ys = _expert_ffn(xs, tile_expert, n_used, w_gate, w_up, w_down, layer)
    return _combine(h, info, ys, pos, next_gain)


def kernel(x, p, mix_gain, w_in, conv_w, conv_b, w_rgate, b_rgate, w_igate, b_igate, lru_lambda, q_gain, k_gain, rel_bias, ssm_a_re, ssm_a_im, ssm_log_dt, ssm_b_re, ssm_b_im, ssm_c_re, ssm_c_im, ssm_d, w_glu, b_glu, w_proj_lru, w_proj_att, w_proj_ssm, w_out, ffn_gain, w_group_router, b_group_router, w_expert_router, b_expert_router, w_up, w_gate, w_down, ple_gain, w_ple, w_ple_gate):
    bsz, seq, d = x.shape
    depth = w_in.shape[0]
    m = bsz * seq
    lru_w = w_proj_lru.shape[1]
    att_w = w_proj_att.shape[1]
    ssm_w = w_proj_ssm.shape[1]
    tn = att_w
    assert ssm_w == tn and lru_w % tn == 0 and d % tn == 0 and tn // LANES == att_w // HEAD_DIM
    side_tiles = (3 * att_w + ssm_w) // tn
    in_width = w_in.shape[2]
    main_tiles = in_width // tn - side_tiles

    bf = lambda w: w.astype(BF16)
    w_glu, w_proj_lru, w_proj_att, w_proj_ssm, w_out = map(bf, (w_glu, w_proj_lru, w_proj_att, w_proj_ssm, w_out))
    w_ple_gate, w_ple = bf(w_ple_gate), bf(w_ple)
    p = bf(p.reshape(depth, m, p.shape[-1]))

    h = x.reshape(m, d).astype(F32)
    for i in range(depth):
        xn = _rmsnorm(h, mix_gain[i])
        proj = _matmul(xn, w_in, i, first_tile=0, n_tiles=main_tiles, skip_from=lru_w // tn, skip=side_tiles, tn=tn)
        side = _matmul(xn, w_in, i, first_tile=lru_w // tn, n_tiles=side_tiles, lane_tile_major=True, tn=tn)
        y_lru = _rglru(proj.reshape(bsz, seq, main_tiles * tn), conv_w[i], conv_b[i], w_rgate[i], b_rgate[i],
                       w_igate[i], b_igate[i], lru_lambda[i])
        y_att = _attention(side, bsz, q_gain[i], k_gain[i], rel_bias[i])
        tables = _s5_tables(ssm_a_re[i], ssm_a_im[i], ssm_log_dt[i], ssm_b_re[i], ssm_b_im[i],
                            ssm_c_re[i], ssm_c_im[i], ssm_d[i], S5_CHUNK)
        y_ssm = _glu(_s5(side, 3, bsz, tables), w_glu, i, b_glu[i])
        merged = _gated_merge(y_lru.reshape(m, lru_w), y_att, y_ssm,
                              w_proj_lru, w_proj_att, w_proj_ssm, i, proj, lru_w)
        h = _matmul_residual(h, merged, w_out, i, in_place=i > 0)
        h, hn = _moe(h, ffn_gain[i], w_group_router[i], b_group_router[i], w_expert_router[i], b_expert_router[i],
                     w_gate, w_up, w_down, i, ple_gain[i])
        h = _ple(h, hn, w_ple_gate, p, w_ple, i)
    return h.reshape(bsz, seq, d)
```

```python
import functools
import math

import jax
import jax.numpy as jnp
from jax import lax
from jax.experimental import pallas as pl
from jax.experimental.pallas import tpu as pltpu

F32 = jnp.float32
BF16 = jnp.bfloat16
U32 = jnp.uint32

CHUNK = 64
CHUNK_SHIFT = CHUNK.bit_length() - 1
assert 1 << CHUNK_SHIFT == CHUNK
LEFT_CHUNKS = 8
REL_CLIP = 128
HEAD_DIM = 128
LRU_C = 8.0
TOP_K = 2
EPS = 1e-6
NEG_INF = -1e30

LANES = 128
S5_CHUNK = 8
ATTN_PREV = LEFT_CHUNKS * CHUNK
ATTN_QBLOCK = 2 * ATTN_PREV
ATTN_SUB = 128
ATTN_ROLL_WIDTH = 1024
MOE_ROW_TILE = 256
MOE_TOKEN_TILE = 256
ROW_DMA_UNROLL = 8
V7X_VMEM_LIMIT_BYTES = 56 * 1024 * 1024
HI16 = 0xFFFF0000


def _params(*semantics):
    return pltpu.CompilerParams(dimension_semantics=semantics, vmem_limit_bytes=V7X_VMEM_LIMIT_BYTES)


def _dot(a, b):
    return jnp.dot(a, b, preferred_element_type=F32)


def _sigmoid(x):
    return 0.5 * jnp.tanh(0.5 * x) + 0.5


def _pack_halves(x):
    n = x.shape[1] // 2
    lo = pltpu.bitcast(x[:, :n].astype(BF16).astype(F32), U32)
    hi = pltpu.bitcast(x[:, n:].astype(BF16).astype(F32), U32)
    return (lo >> 16) | (hi & jnp.uint32(HI16))


def _unpack_halves(u):
    return pltpu.bitcast(u << 16, F32), pltpu.bitcast(u & jnp.uint32(HI16), F32)


def _rmsnorm_kernel(x_ref, g_ref, o_ref):
    x = x_ref[...].astype(F32)
    ms = jnp.mean(x * x, axis=-1, keepdims=True)
    o_ref[...] = (x * lax.rsqrt(ms + EPS) * g_ref[...]).astype(o_ref.dtype)


def _rmsnorm(x, gain, *, tm=512):
    m, d = x.shape
    return pl.pallas_call(
        _rmsnorm_kernel,
        grid=(m // tm,),
        in_specs=[pl.BlockSpec((tm, d), lambda i: (i, 0)), pl.BlockSpec((1, d), lambda i: (0, 0))],
        out_specs=pl.BlockSpec((tm, d), lambda i: (i, 0)),
        out_shape=jax.ShapeDtypeStruct((m, d), BF16),
        compiler_params=_params("parallel"),
        name="rmsnorm",
    )(x, gain.reshape(1, d).astype(F32))


def _mm_kernel(x_ref, w_hbm, o_ref, stage, w_s, sem, *, layer, tn):
    j, i = pl.program_id(0), pl.program_id(1)

    def weight_copy(col_tile):
        return pltpu.make_async_copy(w_hbm.at[layer, :, pl.ds(col_tile * tn, tn)], stage, sem)

    @pl.when(jnp.logical_and(j == 0, i == 0))
    def _():
        weight_copy(0).start()

    @pl.when(i == 0)
    def _():
        weight_copy(0).wait()
        w_s[...] = stage[...].astype(BF16)

        @pl.when(j + 1 < pl.num_programs(0))
        def _():
            weight_copy(j + 1).start()

    o_ref[...] = _dot(x_ref[...], w_s[...]).astype(o_ref.dtype)


def _layer_spec(layer, block, index_map):
    return pl.BlockSpec((None,) + tuple(block), lambda *a: (layer,) + tuple(index_map(*a)))


def _matmul(x, w, layer, *, tm=1024, tn=1024, out_dtype=BF16):
    m, k = x.shape
    n = w.shape[2]
    return pl.pallas_call(
        functools.partial(_mm_kernel, layer=layer, tn=tn),
        grid=(n // tn, m // tm),
        in_specs=[pl.BlockSpec((tm, k), lambda j, i: (i, 0)), pl.BlockSpec(memory_space=pl.ANY)],
        out_specs=pl.BlockSpec((tm, tn), lambda j, i: (i, j)),
        out_shape=jax.ShapeDtypeStruct((m, n), out_dtype),
        scratch_shapes=[pltpu.VMEM((k, tn), F32), pltpu.VMEM((k, tn), BF16), pltpu.SemaphoreType.DMA],
        compiler_params=_params("arbitrary", "arbitrary"),
        name="matmul",
    )(x, w)


def _mm_residual_kernel(h_ref, x_ref, w_ref, o_ref):
    o_ref[...] = h_ref[...] + _dot(x_ref[...], w_ref[...])


def _matmul_residual(h, x, w, layer, *, in_place, tm=1024, tn=512):
    m, k = x.shape
    n = w.shape[2]
    return pl.pallas_call(
        _mm_residual_kernel,
        grid=(m // tm, n // tn),
        in_specs=[
            pl.BlockSpec((tm, tn), lambda i, j: (i, j)),
            pl.BlockSpec((tm, k), lambda i, j: (i, 0)),
            _layer_spec(layer, (k, tn), lambda i, j: (0, j)),
        ],
        out_specs=pl.BlockSpec((tm, tn), lambda i, j: (i, j)),
        out_shape=jax.ShapeDtypeStruct((m, n), F32),
        input_output_aliases={0: 0} if in_place else {},
        compiler_params=_params("parallel", "arbitrary"),
        name="matmul_residual",
    )(h, x, w)


def _ple_kernel(h_ref, x_ref, wg_ref, p_ref, we_ref, o_ref):
    gate = _sigmoid(_dot(x_ref[...], wg_ref[...]))
    emb = _dot(p_ref[...], we_ref[...])
    o_ref[...] = h_ref[...] + gate * emb


def _ple(h, xn, w_gate, p, w_ple, layer, *, tm=1024, tn=512):
    m, k = xn.shape
    n = w_gate.shape[2]
    kp = p.shape[2]
    return pl.pallas_call(
        _ple_kernel,
        grid=(m // tm, n // tn),
        in_specs=[
            pl.BlockSpec((tm, tn), lambda i, j: (i, j)),
            pl.BlockSpec((tm, k), lambda i, j: (i, 0)),
            _layer_spec(layer, (k, tn), lambda i, j: (0, j)),
            _layer_spec(layer, (tm, kp), lambda i, j: (i, 0)),
            _layer_spec(layer, (kp, tn), lambda i, j: (0, j)),
        ],
        out_specs=pl.BlockSpec((tm, tn), lambda i, j: (i, j)),
        out_shape=jax.ShapeDtypeStruct((m, n), F32),
        input_output_aliases={0: 0},
        compiler_params=_params("parallel", "arbitrary"),
        name="ple",
    )(h, xn, w_gate, p, w_ple)


def _merge_kernel(yl_ref, ya_ref, ys_ref, pl_ref, pa_ref, ps_ref, gl_ref, ga_ref, gs_ref, o_ref):
    acc = _sigmoid(gl_ref[...].astype(F32)) * _dot(yl_ref[...], pl_ref[...])
    acc += _sigmoid(ga_ref[...].astype(F32)) * _dot(ya_ref[...], pa_ref[...])
    acc += _sigmoid(gs_ref[...].astype(F32)) * _dot(ys_ref[...], ps_ref[...])
    o_ref[...] = acc.astype(o_ref.dtype)


def _gated_merge(y_lru, y_att, y_ssm, p_lru, p_att, p_ssm, layer, proj, gate_off, *, tm=1024, tn=512):
    m = y_lru.shape[0]
    d = p_lru.shape[2]
    goff = gate_off // tn
    nd = d // tn

    def y_spec(y):
        return pl.BlockSpec((tm, y.shape[1]), lambda i, j: (i, 0))

    def p_spec(p):
        return _layer_spec(layer, (p.shape[1], tn), lambda i, j: (0, j))

    def g_spec(b):
        return pl.BlockSpec((tm, tn), lambda i, j: (i, goff + b * nd + j))

    return pl.pallas_call(
        _merge_kernel,
        grid=(m // tm, nd),
        in_specs=[y_spec(y_lru), y_spec(y_att), y_spec(y_ssm), p_spec(p_lru), p_spec(p_att), p_spec(p_ssm),
                  g_spec(0), g_spec(1), g_spec(2)],
        out_specs=pl.BlockSpec((tm, tn), lambda i, j: (i, j)),
        out_shape=jax.ShapeDtypeStruct((m, d), BF16),
        compiler_params=_params("parallel", "arbitrary"),
        name="gated_merge",
    )(y_lru, y_att, y_ssm, p_lru, p_att, p_ssm, proj, proj, proj)


def _scan_rows8(a8, x8, hprev, row):
    for k in (1, 2, 4):
        keep = row >= k
        a_sh = jnp.where(keep, pltpu.roll(a8, k, 0), 1.0)
        x_sh = jnp.where(keep, pltpu.roll(x8, k, 0), 0.0)
        x8 = a8 * x_sh + x8
        a8 = a8 * a_sh
    return a8 * hprev + x8


def _lru_kernel(u_ref, cw_ref, cb_ref, wr_ref, br_ref, wi_ref, bi_ref, lam_ref, y_ref,
                tail_ref, h_ref, a_s, x_s, *, t_rows, n_blocks, block, conv_width):
    @pl.when(pl.program_id(1) == 0)
    def _():
        tail_ref[...] = jnp.zeros_like(tail_ref)
        h_ref[...] = jnp.zeros_like(h_ref)

    u = u_ref[0].astype(F32)
    ue = jnp.concatenate([tail_ref[...], u], axis=0)
    xc = cb_ref[...]
    for j in range(conv_width):
        off = 8 - (conv_width - 1) + j
        xc = xc + cw_ref[j:j + 1, :] * ue[off:off + t_rows, :]
    tail_ref[...] = u[t_rows - 8:, :]

    xcb = xc.astype(BF16)
    rs, igs = [], []
    for hb in range(n_blocks):
        xb = xcb[:, hb * block:(hb + 1) * block]
        rs.append(_dot(xb, wr_ref[hb]))
        igs.append(_dot(xb, wi_ref[hb]))
    r = _sigmoid(jnp.concatenate(rs, axis=1) + br_ref[...])
    ig = _sigmoid(jnp.concatenate(igs, axis=1) + bi_ref[...])
    lam = lam_ref[...]
    softplus_neg_lam = jnp.maximum(-lam, 0.0) + jnp.log1p(jnp.exp(-jnp.abs(lam)))
    log_a = (-LRU_C) * r * softplus_neg_lam
    a = jnp.exp(log_a)
    a_s[...] = a
    x_s[...] = jnp.sqrt(-jnp.tanh(log_a) * (a * a + 1.0)) * (ig * xc)

    w = a_s.shape[1]
    row = lax.broadcasted_iota(jnp.int32, (8, w), 0)

    def body(i, hprev):
        r0 = pl.multiple_of(i * 16, 16)
        h_a = _scan_rows8(a_s[pl.ds(r0, 8), :], x_s[pl.ds(r0, 8), :], hprev, row)
        h_b = _scan_rows8(a_s[pl.ds(r0 + 8, 8), :], x_s[pl.ds(r0 + 8, 8), :], h_a[7:8, :], row)
        y_ref[0, pl.ds(r0, 16), :] = jnp.concatenate([h_a, h_b], axis=0).astype(y_ref.dtype)
        return h_b[7:8, :]

    h_ref[...] = lax.fori_loop(0, t_rows // 16, body, h_ref[...])


def _rglru(proj3, conv_w, conv_b, w_rg, b_rg, w_ig, b_ig, lam, *, t_rows=256):
    bsz, seq, _ = proj3.shape
    n_blocks, block, _ = w_rg.shape
    w = n_blocks * block
    cw = conv_w.shape[0]
    vec = lambda v: v.reshape(1, w).astype(F32)
    full2 = lambda r, c: pl.BlockSpec((r, c), lambda b, t: (0, 0))
    full3 = pl.BlockSpec((n_blocks, block, block), lambda b, t: (0, 0, 0))
    return pl.pallas_call(
        functools.partial(_lru_kernel, t_rows=t_rows, n_blocks=n_blocks, block=block, conv_width=cw),
        grid=(bsz, seq // t_rows),
        in_specs=[pl.BlockSpec((1, t_rows, w), lambda b, t: (b, t, 0)), full2(cw, w), full2(1, w),
                  full3, full2(1, w), full3, full2(1, w), full2(1, w)],
        out_specs=pl.BlockSpec((1, t_rows, w), lambda b, t: (b, t, 0)),
        out_shape=jax.ShapeDtypeStruct((bsz, seq, w), BF16),
        scratch_shapes=[pltpu.VMEM((8, w), F32), pltpu.VMEM((1, w), F32),
                        pltpu.VMEM((t_rows, w), F32), pltpu.VMEM((t_rows, w), F32)],
        compiler_params=_params("parallel", "arbitrary"),
        name="rglru",
    )(proj3, conv_w.astype(F32), vec(conv_b), w_rg.astype(BF16), vec(b_rg), w_ig.astype(BF16), vec(b_ig),
      vec(lam))


def _head_rms(x, gain):
    x = x.astype(F32)
    return x * lax.rsqrt(jnp.mean(x * x, axis=-1, keepdims=True) + EPS) * gain


def _attn_kernel(q_ref, kp_ref, kc_ref, vp_ref, vc_ref, bvec_ref, qg_ref, kg_ref, o_ref, bias_s, *, qb, sub, span):
    first = pl.program_id(2) == 0

    @pl.when(first)
    def _():
        vec = jnp.broadcast_to(bvec_ref[0], (sub, ATTN_ROLL_WIDTH))
        table = pltpu.roll(vec, 0, 1, stride=1, stride_axis=0)[:, :span]
        q_chunk = (lax.broadcasted_iota(jnp.int32, table.shape, 0) >> CHUNK_SHIFT) + LEFT_CHUNKS
        k_chunk = lax.broadcasted_iota(jnp.int32, table.shape, 1) >> CHUNK_SHIFT
        in_band = jnp.logical_and(k_chunk >= q_chunk - LEFT_CHUNKS, k_chunk <= q_chunk)
        bias_s[...] = jnp.where(in_band, table, NEG_INF)

    qn = (_head_rms(q_ref[0], qg_ref[...]) * (HEAD_DIM ** -0.5)).astype(BF16)
    kn = _head_rms(jnp.concatenate([kp_ref[0], kc_ref[0]], axis=0), kg_ref[...]).astype(BF16)
    v = jnp.concatenate([vp_ref[0], vc_ref[0]], axis=0)
    bias = bias_s[...]
    outs = []
    for part in range(qb // sub):
        c0 = part * sub
        s = lax.dot_general(qn[c0:c0 + sub], kn[c0:c0 + span], (((1,), (1,)), ((), ())),
                            preferred_element_type=F32)
        s = s + bias
        col = c0 + lax.broadcasted_iota(jnp.int32, s.shape, 1)
        s = jnp.where(jnp.logical_and(first, col < ATTN_PREV), NEG_INF, s)
        p = jnp.exp(s - jnp.max(s, axis=-1, keepdims=True))
        denom = jnp.sum(p, axis=-1, keepdims=True)
        outs.append(_dot(p.astype(BF16), v[c0:c0 + span]) / denom)
    o_ref[0] = jnp.concatenate(outs, axis=0).astype(o_ref.dtype)


def _bias_vector(rel_bias, sub):
    mm = jnp.arange(ATTN_ROLL_WIDTH)
    offset = jnp.where(mm < ATTN_ROLL_WIDTH - sub, mm, mm - ATTN_ROLL_WIDTH)
    dist = ATTN_PREV - offset
    return rel_bias.astype(F32)[:, None, jnp.clip(dist, -REL_CLIP, REL_CLIP) + REL_CLIP]


def _attention(proj3, q_off, k_off, v_off, n_heads, q_gain, k_gain, rel_bias, *, qb=ATTN_QBLOCK, sub=ATTN_SUB):
    span = sub + ATTN_PREV
    assert qb % ATTN_PREV == 0 and qb % sub == 0 and sub % CHUNK == 0 and sub % LANES == 0
    assert span + sub <= ATTN_ROLL_WIDTH
    bsz, seq, _ = proj3.shape
    qo, ko, vo = q_off // HEAD_DIM, k_off // HEAD_DIM, v_off // HEAD_DIM
    ratio = qb // ATTN_PREV
    cur = lambda off: pl.BlockSpec((1, qb, HEAD_DIM), lambda b, h, n: (b, n, off + h))
    prev = lambda off: pl.BlockSpec((1, ATTN_PREV, HEAD_DIM),
                                    lambda b, h, n: (b, jnp.maximum(n * ratio - 1, 0), off + h))
    gain = pl.BlockSpec((1, HEAD_DIM), lambda b, h, n: (0, 0))
    return pl.pallas_call(
        functools.partial(_attn_kernel, qb=qb, sub=sub, span=span),
        grid=(bsz, n_heads, seq // qb),
        in_specs=[cur(qo), prev(ko), cur(ko), prev(vo), cur(vo),
                  pl.BlockSpec((1, 1, ATTN_ROLL_WIDTH), lambda b, h, n: (h, 0, 0)), gain, gain],
        out_specs=pl.BlockSpec((1, qb, HEAD_DIM), lambda b, h, n: (b, n, h)),
        out_shape=jax.ShapeDtypeStruct((bsz, seq, n_heads * HEAD_DIM), BF16),
        scratch_shapes=[pltpu.VMEM((sub, span), F32)],
        compiler_params=_params("parallel", "parallel", "arbitrary"),
        name="band_attention",
    )(proj3, proj3, proj3, proj3, proj3, _bias_vector(rel_bias, sub),
      q_gain.reshape(1, HEAD_DIM).astype(F32), k_gain.reshape(1, HEAD_DIM).astype(F32))


def _s5_tables(a_re, a_im, log_dt, b_re, b_im, c_re, c_im, d_skip, t_chunk):
    hi = lax.Precision.HIGHEST
    g, p = a_re.shape
    c = b_re.shape[-1]
    gl = LANES // c
    nj = g // gl
    dt = jnp.exp(log_dt.astype(F32))[:, None]
    ar, ai = a_re.astype(F32), a_im.astype(F32)

    def apow(tau):
        tau = jnp.asarray(tau, F32)[:, None, None]
        mag = jnp.exp(tau * dt * ar)
        return mag * jnp.cos(tau * dt * ai), mag * jnp.sin(tau * dt * ai)

    pr, pi = apow(jnp.arange(t_chunk + 1))
    abar_re, abar_im = pr[1], pi[1]
    den = ar * ar + ai * ai
    nr, ni = abar_re - 1.0, abar_im
    coef_re = (nr * ar + ni * ai) / den
    coef_im = (ni * ar - nr * ai) / den
    br, bi = b_re.astype(F32), b_im.astype(F32)
    bbar_re = coef_re[..., None] * br - coef_im[..., None] * bi
    bbar_im = coef_re[..., None] * bi + coef_im[..., None] * br
    cr, ci = c_re.astype(F32), c_im.astype(F32)
    ca_re = cr[None] * pr[:, :, None, :] - ci[None] * pi[:, :, None, :]
    ca_im = cr[None] * pi[:, :, None, :] + ci[None] * pr[:, :, None, :]
    lane = jnp.arange(LANES)
    col = jnp.arange(gl * p)
    rep_c = (lane[None, :] % c == jnp.arange(c)[:, None]).astype(BF16)
    rep_p = (col[None, :] % p == jnp.arange(p)[:, None]).astype(BF16)
    lane_lane = lane[:, None] // c == lane[None, :] // c
    lane_col = lane[:, None] // c == col[None, :] // p
    col_lane = col[:, None] // p == lane[None, :] // c

    k_tap = (jnp.einsum("tgcp,gpd->tgcd", ca_re[:t_chunk], bbar_re, precision=hi)
             - jnp.einsum("tgcp,gpd->tgcd", ca_im[:t_chunk], bbar_im, precision=hi))
    lag = jnp.arange(t_chunk)[None, :] - jnp.arange(t_chunk)[:, None]
    k_st = jnp.where((lag >= 0)[:, :, None, None, None], k_tap[jnp.maximum(lag, 0)], 0.0)
    k_rows = jnp.transpose(k_st.reshape(t_chunk, t_chunk, nj, gl, c, c), (2, 0, 1, 3, 5, 4))
    k_rep = jnp.dot(k_rows.reshape(-1, c).astype(BF16), rep_c).reshape(nj, t_chunk, t_chunk, LANES, LANES)
    k_rep = jnp.where(lane_lane[None, None, None], k_rep, 0)
    k_mat = jnp.transpose(k_rep, (0, 1, 3, 2, 4)).reshape(nj, t_chunk * LANES, t_chunk * LANES)

    rev = t_chunk - 1 - jnp.arange(t_chunk)
    s_re = pr[rev][..., None] * bbar_re[None] - pi[rev][..., None] * bbar_im[None]
    s_im = pr[rev][..., None] * bbar_im[None] + pi[rev][..., None] * bbar_re[None]

    def inc_mat(x):
        rows = jnp.transpose(x.reshape(t_chunk, nj, gl, p, c), (1, 0, 2, 4, 3))
        rep = jnp.dot(rows.reshape(-1, p).astype(BF16), rep_p).reshape(nj, t_chunk, LANES, gl * p)
        return jnp.where(lane_col[None, None], rep, 0).reshape(nj, t_chunk * LANES, gl * p)

    w_in = jnp.concatenate([k_mat, inc_mat(s_re), inc_mat(s_im)], axis=-1)

    def out_mat(x):
        rows = jnp.transpose(x.reshape(t_chunk, nj, gl, c, p), (1, 2, 4, 0, 3))
        rep = jnp.dot(rows.reshape(-1, c).astype(BF16), rep_c).reshape(nj, gl * p, t_chunk, LANES)
        return jnp.where(col_lane[None, :, None, :], rep, 0).reshape(nj, gl * p, t_chunk * LANES)

    w_state = jnp.concatenate([out_mat(ca_re[1:]), out_mat(-ca_im[1:])], axis=1)

    qr, qi = apow(t_chunk * jnp.arange(1, 9))
    lanes = lambda x: jnp.transpose(x.reshape(8, nj, gl * p), (1, 0, 2))
    consts = jnp.concatenate([lanes(qr), lanes(qi)], axis=1)
    d_row = jnp.tile(d_skip.astype(F32).reshape(nj, 1, LANES), (1, 1, t_chunk))
    return w_in, w_state, consts, d_row


def _gelu_tanh(y):
    return 0.5 * y * (1.0 + jnp.tanh(math.sqrt(2.0 / math.pi) * (y + 0.044715 * (y * y * y))))


def _s5_kernel(x_ref, win_ref, wst_ref, cst_ref, d_ref, y_ref, stage, inc_re, inc_im, hp_re, hp_im,
               *, t_chunk, n_rows, tc, gp):
    stage[...] = x_ref[...].astype(F32)
    u = jnp.concatenate([stage[pl.ds(s, n_rows, stride=t_chunk), :] for s in range(t_chunk)], axis=1)
    z = _dot(u.astype(BF16), win_ref[0])
    inc_re[...] = z[:, tc:tc + gp]
    inc_im[...] = z[:, tc + gp:]
    cst = cst_ref[0]
    c_re, c_im = cst[0:8, :], cst[8:16, :]
    row = lax.broadcasted_iota(jnp.int32, (8, gp), 0)

    def body(i, carry):
        h_re, h_im = carry
        r0 = pl.multiple_of(i * 8, 8)
        x_re = inc_re[pl.ds(r0, 8), :]
        x_im = inc_im[pl.ds(r0, 8), :]
        for k in (1, 2, 4):
            keep = row >= k
            s_re = jnp.where(keep, pltpu.roll(x_re, k, 0), 0.0)
            s_im = jnp.where(keep, pltpu.roll(x_im, k, 0), 0.0)
            m_re, m_im = c_re[k - 1:k, :], c_im[k - 1:k, :]
            x_re, x_im = x_re + m_re * s_re - m_im * s_im, x_im + m_re * s_im + m_im * s_re
        o_re = x_re + c_re * h_re - c_im * h_im
        o_im = x_im + c_re * h_im + c_im * h_re
        first = row >= 1
        hp_re[pl.ds(r0, 8), :] = jnp.where(first, pltpu.roll(o_re, 1, 0), h_re)
        hp_im[pl.ds(r0, 8), :] = jnp.where(first, pltpu.roll(o_im, 1, 0), h_im)
        return o_re[7:8, :], o_im[7:8, :]

    zero = jnp.zeros((1, gp), F32)
    lax.fori_loop(0, n_rows // 8, body, (zero, zero))
    h_prev = jnp.concatenate([hp_re[...], hp_im[...]], axis=1).astype(BF16)
    y = _gelu_tanh(z[:, :tc] + _dot(h_prev, wst_ref[0]) + d_ref[0] * u)
    for t in range(t_chunk):
        stage[pl.ds(t, n_rows, stride=t_chunk), :] = y[:, t * LANES:(t + 1) * LANES]
    y_ref[...] = stage[...].astype(y_ref.dtype)


def _s5(proj, ssm_off, ssm_w, bsz, tables, *, t_chunk=S5_CHUNK):
    w_in, w_state, consts, d_row = tables
    m = proj.shape[0]
    seq = m // bsz
    nj = ssm_w // LANES
    n_rows = seq // t_chunk
    tc = t_chunk * LANES
    gp = w_state.shape[1] // 2
    jb0 = ssm_off // LANES
    return pl.pallas_call(
        functools.partial(_s5_kernel, t_chunk=t_chunk, n_rows=n_rows, tc=tc, gp=gp),
        grid=(nj, bsz),
        in_specs=[
            pl.BlockSpec((seq, LANES), lambda j, b: (b, jb0 + j)),
            pl.BlockSpec((1, tc, tc + 2 * gp), lambda j, b: (j, 0, 0)),
            pl.BlockSpec((1, 2 * gp, tc), lambda j, b: (j, 0, 0)),
            pl.BlockSpec((1, 16, gp), lambda j, b: (j, 0, 0)),
            pl.BlockSpec((1, 1, tc), lambda j, b: (j, 0, 0)),
        ],
        out_specs=pl.BlockSpec((seq, LANES), lambda j, b: (b, j)),
        out_shape=jax.ShapeDtypeStruct((m, ssm_w), BF16),
        scratch_shapes=[pltpu.VMEM((seq, LANES), F32)] + [pltpu.VMEM((n_rows, gp), F32) for _ in range(4)],
        compiler_params=_params("parallel", "parallel"),
        name="s5",
    )(proj, w_in, w_state, consts, d_row)


def _glu_kernel(x_ref, w_ref, b_ref, o_ref, *, n):
    z = _dot(x_ref[...], w_ref[...]) + b_ref[...]
    o_ref[...] = (z[:, :n] * _sigmoid(z[:, n:])).astype(o_ref.dtype)


def _glu(x, w, layer, b, *, tm=1024):
    m, k = x.shape
    n2 = w.shape[2]
    n = n2 // 2
    return pl.pallas_call(
        functools.partial(_glu_kernel, n=n),
        grid=(m // tm,),
        in_specs=[
            pl.BlockSpec((tm, k), lambda i: (i, 0)),
            _layer_spec(layer, (k, n2), lambda i: (0, 0)),
            pl.BlockSpec((1, n2), lambda i: (0, 0)),
        ],
        out_specs=pl.BlockSpec((tm, n), lambda i: (i, 0)),
        out_shape=jax.ShapeDtypeStruct((m, n), BF16),
        compiler_params=_params("parallel"),
        name="glu",
    )(x, w, b.reshape(1, n2).astype(F32))


def _route_kernel(h_ref, g_ref, w_ref, b_ref, xp_ref, info_ref, *, n_groups, per_group):
    x = h_ref[...]
    xn = x * lax.rsqrt(jnp.mean(x * x, axis=-1, keepdims=True) + EPS) * g_ref[...]
    xp_ref[...] = _pack_halves(xn)
    n_exp = n_groups * per_group
    logits = _dot(xn.astype(BF16), w_ref[...]) + b_ref[...]
    lane = lax.broadcasted_iota(jnp.int32, logits.shape, 1).astype(F32)
    big = float(LANES)
    is_group = jnp.logical_and(lane >= n_exp, lane < n_exp + n_groups)
    gl = jnp.where(is_group, logits, -jnp.inf)
    gmax = jnp.max(gl, axis=-1, keepdims=True)
    gsel = jnp.min(jnp.where(gl == gmax, lane, big), axis=-1, keepdims=True) - n_exp
    gprob = 1.0 / jnp.sum(jnp.where(is_group, jnp.exp(logits - gmax), 0.0), axis=-1, keepdims=True)
    in_group = jnp.logical_and(lane >= gsel * per_group, lane < (gsel + 1.0) * per_group)
    el = jnp.where(in_group, logits, -jnp.inf)
    v1 = jnp.max(el, axis=-1, keepdims=True)
    i1 = jnp.min(jnp.where(el == v1, lane, big), axis=-1, keepdims=True)
    el2 = jnp.where(lane == i1, -jnp.inf, el)
    v2 = jnp.max(el2, axis=-1, keepdims=True)
    i2 = jnp.min(jnp.where(el2 == v2, lane, big), axis=-1, keepdims=True)
    e2 = jnp.exp(v2 - v1)
    w1 = gprob / (1.0 + e2)
    w2 = gprob * e2 / (1.0 + e2)
    info_ref[...] = (jnp.where(lane == 0.0, i1, 0.0) + jnp.where(lane == 1.0, i2, 0.0)
                     + jnp.where(lane == 2.0, w1, 0.0) + jnp.where(lane == 3.0, w2, 0.0))


def _route(h, gain, w_gr, b_gr, w_er, b_er, *, tm=512):
    m, d = h.shape
    n_groups, _, per_group = w_er.shape
    n_exp = n_groups * per_group
    assert n_exp + n_groups <= LANES
    w = jnp.concatenate([jnp.transpose(w_er, (1, 0, 2)).reshape(d, n_exp), w_gr], axis=1)
    w = jnp.pad(w, ((0, 0), (0, LANES - n_exp - n_groups))).astype(BF16)
    b = jnp.concatenate([b_er.reshape(n_exp), b_gr]).astype(F32)
    b = jnp.pad(b, (0, LANES - n_exp - n_groups)).reshape(1, LANES)
    return pl.pallas_call(
        functools.partial(_route_kernel, n_groups=n_groups, per_group=per_group),
        grid=(m // tm,),
        in_specs=[pl.BlockSpec((tm, d), lambda i: (i, 0)), pl.BlockSpec((1, d), lambda i: (0, 0)),
                  pl.BlockSpec((d, LANES), lambda i: (0, 0)), pl.BlockSpec((1, LANES), lambda i: (0, 0))],
        out_specs=[pl.BlockSpec((tm, d // 2), lambda i: (i, 0)), pl.BlockSpec((tm, LANES), lambda i: (i, 0))],
        out_shape=[jax.ShapeDtypeStruct((m, d // 2), U32), jax.ShapeDtypeStruct((m, LANES), F32)],
        compiler_params=_params("parallel"),
        name="moe_route",
    )(h, gain.reshape(1, d).astype(F32), w, b)


def _sorted_layout(expert_ids, n_exp, row_tile, n_tiles):
    e = expert_ids.reshape(-1)
    blk = LANES
    nb = e.shape[0] // blk
    onehot = (e[:, None] == jnp.arange(n_exp, dtype=jnp.int32)[None, :]).reshape(nb, blk, n_exp)
    tri = jnp.tril(jnp.ones((blk, blk), BF16))
    within = jnp.einsum("ij,bjk->bik", tri, onehot.astype(BF16), preferred_element_type=F32)
    totals = within[:, -1, :]
    before = jnp.dot(jnp.tril(jnp.ones((nb, nb), F32), -1), totals, precision=lax.Precision.HIGHEST)
    counts = (before[-1] + totals[-1]).astype(jnp.int32)
    rank = jnp.sum(jnp.where(onehot, within + before[:, None, :], 0.0), axis=-1).reshape(-1).astype(jnp.int32) - 1
    padded = ((counts + row_tile - 1) // row_tile) * row_tile
    ends = jnp.cumsum(padded)
    starts = ends - padded
    pos = starts[e] + rank
    tile_start = jnp.arange(n_tiles, dtype=jnp.int32) * row_tile
    tile_expert = jnp.minimum(jnp.sum((tile_start[:, None] >= ends[None, :]).astype(jnp.int32), axis=1), n_exp - 1)
    n_used = (ends[-1] // row_tile).astype(jnp.int32).reshape(1)
    return pos.astype(jnp.int32), tile_expert.astype(jnp.int32), n_used


def _row_copy(src_ref, src_row, dst_ref, dst_row, sem):
    return pltpu.make_async_copy(src_ref.at[pl.ds(src_row, 1)], dst_ref.at[pl.ds(dst_row, 1)], sem)


def _start_row_copies(n_rows, start_row):
    def body(t, c):
        start_row(t)
        return c

    lax.fori_loop(0, n_rows, body, 0, unroll=ROW_DMA_UNROLL)


def _wait_row_copies(n_copies, example_copy):
    for _ in range(n_copies):
        example_copy.wait()


def _dispatch_kernel(pos_ref, x_ref, init_ref, out_ref, sem, *, tt, n_slots):
    del init_ref

    def start_row(t):
        for k in range(n_slots):
            _row_copy(x_ref, t, out_ref, pos_ref[0, 0, n_slots * t + k], sem).start(priority=k % 2)

    _start_row_copies(tt, start_row)
    _wait_row_copies(tt * n_slots, _row_copy(x_ref, 0, out_ref, 0, sem))


def _dispatch(xp, pos, n_rows, *, tt=MOE_TOKEN_TILE, n_slots=TOP_K):
    m, d2 = xp.shape
    pos3 = pos.reshape(m // tt, 1, tt * n_slots)
    return pl.pallas_call(
        functools.partial(_dispatch_kernel, tt=tt, n_slots=n_slots),
        grid=(m // tt,),
        in_specs=[pl.BlockSpec((1, 1, tt * n_slots), lambda i: (i, 0, 0), memory_space=pltpu.SMEM),
                  pl.BlockSpec((tt, d2), lambda i: (i, 0)),
                  pl.BlockSpec(memory_space=pl.ANY)],
        out_specs=pl.BlockSpec(memory_space=pl.ANY),
        out_shape=jax.ShapeDtypeStruct((n_rows, d2), U32),
        scratch_shapes=[pltpu.SemaphoreType.DMA],
        input_output_aliases={2: 0},
        compiler_params=_params("arbitrary"),
        name="moe_dispatch",
    )(pos3, xp, jnp.zeros((n_rows, d2), U32))


def _expert_kernel(te_ref, first_ref, next_ref, nu_ref, x_ref, wg_hbm, wu_hbm, wd_hbm, o_ref,
                   stage_g, stage_u, stage_d, wg_s, wu_s, wd_s, sem, *, layer):
    t = pl.program_id(0)
    in_use = t < nu_ref[0]
    copies = ((wg_hbm, stage_g), (wu_hbm, stage_u), (wd_hbm, stage_d))

    def weight_copy(k, expert):
        src, dst = copies[k]
        return pltpu.make_async_copy(src.at[layer, expert], dst, sem.at[k])

    @pl.when(t == 0)
    def _():
        for k in range(len(copies)):
            weight_copy(k, te_ref[0]).start()

    @pl.when(jnp.logical_and(in_use, first_ref[t] == 1))
    def _():
        for k, dst in enumerate((wg_s, wu_s, wd_s)):
            weight_copy(k, 0).wait()
            dst[...] = copies[k][1][...].astype(BF16)

        @pl.when(next_ref[t] >= 0)
        def _():
            for k in range(len(copies)):
                weight_copy(k, next_ref[t]).start()

    @pl.when(in_use)
    def _():
        lo, hi = _unpack_halves(x_ref[...])
        x = jnp.concatenate([lo, hi], axis=1).astype(BF16)
        g = _dot(x, wg_s[...])
        u = _dot(x, wu_s[...])
        hid = (g * _sigmoid(g) * u).astype(BF16)
        o_ref[...] = _pack_halves(_dot(hid, wd_s[...]))

    @pl.when(jnp.logical_not(in_use))
    def _():
        o_ref[...] = jnp.zeros_like(o_ref)


def _expert_ffn(xs, tile_expert, n_used, w_gate, w_up, w_down, layer, *, tm=MOE_ROW_TILE):
    rows, d2 = xs.shape
    _, _, d, ff = w_gate.shape
    n_tiles = rows // tm
    idx = jnp.arange(n_tiles, dtype=jnp.int32)
    used = idx < n_used[0]
    first = jnp.logical_and(used, jnp.logical_or(idx == 0, tile_expert != jnp.roll(tile_expert, 1)))
    first_pos = jnp.where(first, idx, n_tiles)
    later = jnp.concatenate([lax.cummin(first_pos[::-1])[::-1][1:], jnp.full((1,), n_tiles, jnp.int32)])
    next_expert = jnp.where(later < n_tiles, tile_expert[jnp.minimum(later, n_tiles - 1)], -1).astype(jnp.int32)
    any_spec = pl.BlockSpec(memory_space=pl.ANY)
    grid_spec = pltpu.PrefetchScalarGridSpec(
        num_scalar_prefetch=4,
        grid=(n_tiles,),
        in_specs=[pl.BlockSpec((tm, d2), lambda t, *_: (t, 0)), any_spec, any_spec, any_spec],
        out_specs=pl.BlockSpec((tm, d2), lambda t, *_: (t, 0)),
        scratch_shapes=[pltpu.VMEM((d, ff), F32), pltpu.VMEM((d, ff), F32), pltpu.VMEM((ff, d), F32),
                        pltpu.VMEM((d, ff), BF16), pltpu.VMEM((d, ff), BF16), pltpu.VMEM((ff, d), BF16),
                        pltpu.SemaphoreType.DMA((3,))],
    )
    return pl.pallas_call(
        functools.partial(_expert_kernel, layer=layer),
        grid_spec=grid_spec,
        out_shape=jax.ShapeDtypeStruct((rows, d2), U32),
        compiler_params=_params("arbitrary"),
        name="moe_expert_ffn",
    )(tile_expert, first.astype(jnp.int32), next_expert, n_used, xs, w_gate, w_up, w_down)


def _combine_kernel(pos_ref, pos_next_ref, info_ref, h_ref, g_ref, y_ref, o_ref, on_ref, buf, sem, *, tt, n_slots):
    i = pl.program_id(0)
    cur = i % 2

    def gather(p_ref, half):
        def start_row(t):
            for k in range(n_slots):
                _row_copy(y_ref, p_ref[0, 0, n_slots * t + k], buf.at[half, k], t, sem.at[half]).start(priority=k % 2)

        _start_row_copies(tt, start_row)

    @pl.when(i == 0)
    def _():
        gather(pos_ref, 0)

    @pl.when(i + 1 < pl.num_programs(0))
    def _():
        gather(pos_next_ref, 1 - cur)

    _wait_row_copies(tt * n_slots, _row_copy(y_ref, 0, buf.at[cur, 0], 0, sem.at[cur]))
    info = info_ref[...]
    lane = lax.broadcasted_iota(jnp.int32, info.shape, 1)
    lo_acc = hi_acc = None
    for k in range(n_slots):
        wk = jnp.sum(jnp.where(lane == n_slots + k, info, 0.0), axis=-1, keepdims=True)
        lo, hi = _unpack_halves(buf[cur, k])
        lo_acc = wk * lo if lo_acc is None else lo_acc + wk * lo
        hi_acc = wk * hi if hi_acc is None else hi_acc + wk * hi
    out = h_ref[...] + jnp.concatenate([lo_acc, hi_acc], axis=1)
    o_ref[...] = out
    ms = jnp.mean(out * out, axis=-1, keepdims=True)
    on_ref[...] = (out * lax.rsqrt(ms + EPS) * g_ref[...]).astype(on_ref.dtype)


def _combine(h, info, ys, pos, next_gain, *, tt=MOE_TOKEN_TILE, n_slots=TOP_K):
    m, d = h.shape
    d2 = ys.shape[1]
    n_steps = m // tt
    pos3 = pos.reshape(n_steps, 1, tt * n_slots)
    return pl.pallas_call(
        functools.partial(_combine_kernel, tt=tt, n_slots=n_slots),
        grid=(n_steps,),
        in_specs=[pl.BlockSpec((1, 1, tt * n_slots), lambda i: (i, 0, 0), memory_space=pltpu.SMEM),
                  pl.BlockSpec((1, 1, tt * n_slots), lambda i: (jnp.minimum(i + 1, n_steps - 1), 0, 0),
                               memory_space=pltpu.SMEM),
                  pl.BlockSpec((tt, LANES), lambda i: (i, 0)),
                  pl.BlockSpec((tt, d), lambda i: (i, 0)),
                  pl.BlockSpec((1, d), lambda i: (0, 0)),
                  pl.BlockSpec(memory_space=pl.ANY)],
        out_specs=[pl.BlockSpec((tt, d), lambda i: (i, 0)), pl.BlockSpec((tt, d), lambda i: (i, 0))],
        out_shape=[jax.ShapeDtypeStruct((m, d), F32), jax.ShapeDtypeStruct((m, d), BF16)],
        scratch_shapes=[pltpu.VMEM((2, n_slots, tt, d2), U32), pltpu.SemaphoreType.DMA((2,))],
        input_output_aliases={3: 0},
        compiler_params=_params("arbitrary"),
        name="moe_combine",
    )(pos3, pos3, info, h, next_gain.reshape(1, d).astype(F32), ys)


def _moe(h, gain, w_gr, b_gr, w_er, b_er, w_gate, w_up, w_down, layer, next_gain):
    m = h.shape[0]
    n_exp = w_gate.shape[1]
    n_tiles = -(-(m * TOP_K + n_exp * (MOE_ROW_TILE - 1)) // MOE_ROW_TILE)
    xp, info = _route(h, gain, w_gr, b_gr, w_er, b_er)
    expert_ids = info[:, :TOP_K].astype(jnp.int32)
    pos, tile_expert, n_used = _sorted_layout(expert_ids, n_exp, MOE_ROW_TILE, n_tiles)
    xs = _dispatch(xp, pos, n_tiles * MOE_ROW_TILE)
    ys = _expert_ffn(xs, tile_expert, n_used, w_gate, w_up, w_down, layer)
    return _combine(h, info, ys, pos, next_gain)


def kernel(x, p, mix_gain, w_in, conv_w, conv_b, w_rgate, b_rgate, w_igate, b_igate, lru_lambda, q_gain, k_gain, rel_bias, ssm_a_re, ssm_a_im, ssm_log_dt, ssm_b_re, ssm_b_im, ssm_c_re, ssm_c_im, ssm_d, w_glu, b_glu, w_proj_lru, w_proj_att, w_proj_ssm, w_out, ffn_gain, w_group_router, b_group_router, w_expert_router, b_expert_router, w_up, w_gate, w_down, ple_gain, w_ple, w_ple_gate):
    bsz, seq, d = x.shape
    depth = w_in.shape[0]
    m = bsz * seq
    lru_w = w_proj_lru.shape[1]
    att_w = w_proj_att.shape[1]
    ssm_w = w_proj_ssm.shape[1]
    n_heads = att_w // HEAD_DIM
    q_off, k_off, v_off = lru_w, lru_w + att_w, lru_w + 2 * att_w
    ssm_off = lru_w + 3 * att_w
    gate_off = ssm_off + ssm_w
    in_width = w_in.shape[2]

    bf = lambda w: w.astype(BF16)
    w_glu, w_proj_lru, w_proj_att, w_proj_ssm, w_out = map(bf, (w_glu, w_proj_lru, w_proj_att, w_proj_ssm, w_out))
    w_ple_gate, w_ple = bf(w_ple_gate), bf(w_ple)
    p = bf(p.reshape(depth, m, p.shape[-1]))

    h = x.reshape(m, d).astype(F32)
    for i in range(depth):
        xn = _rmsnorm(h, mix_gain[i])
        proj = _matmul(xn, w_in, i)
        proj3 = proj.reshape(bsz, seq, in_width)
        y_lru = _rglru(proj3, conv_w[i], conv_b[i], w_rgate[i], b_rgate[i], w_igate[i], b_igate[i], lru_lambda[i])
        y_att = _attention(proj3, q_off, k_off, v_off, n_heads, q_gain[i], k_gain[i], rel_bias[i])
        tables = _s5_tables(ssm_a_re[i], ssm_a_im[i], ssm_log_dt[i], ssm_b_re[i], ssm_b_im[i],
                            ssm_c_re[i], ssm_c_im[i], ssm_d[i], S5_CHUNK)
        y_ssm = _glu(_s5(proj, ssm_off, ssm_w, bsz, tables), w_glu, i, b_glu[i])
        merged = _gated_merge(y_lru.reshape(m, lru_w), y_att.reshape(m, att_w), y_ssm,
                              w_proj_lru, w_proj_att, w_proj_ssm, i, proj, gate_off)
        h = _matmul_residual(h, merged, w_out, i, in_place=i > 0)
        h, hn = _moe(h, ffn_gain[i], w_group_router[i], b_group_router[i], w_expert_router[i], b_expert_router[i],
                     w_gate, w_up, w_down, i, ple_gain[i])
        h = _ple(h, hn, w_ple_gate, p, w_ple, i)
    return h.reshape(bsz, seq, d)
```

```python
import functools
import math

import jax
import jax.numpy as jnp
from jax import lax
from jax.experimental import pallas as pl
from jax.experimental.pallas import tpu as pltpu

F32 = jnp.float32
BF16 = jnp.bfloat16
U32 = jnp.uint32

CHUNK = 64
CHUNK_SHIFT = CHUNK.bit_length() - 1
assert 1 << CHUNK_SHIFT == CHUNK
LEFT_CHUNKS = 8
REL_CLIP = 128
HEAD_DIM = 128
LRU_C = 8.0
TOP_K = 2
EPS = 1e-6
NEG_INF = -1e30

LANES = 128
S5_CHUNK = 8
ATTN_PREV = LEFT_CHUNKS * CHUNK
ATTN_QBLOCK = 4 * ATTN_PREV
ATTN_SUB = 128
ATTN_ROLL_WIDTH = 1024
MOE_ROW_TILE = 256
MOE_TOKEN_TILE = 256
ROW_DMA_UNROLL = 8
V7X_VMEM_LIMIT_BYTES = 56 * 1024 * 1024
HI16 = 0xFFFF0000


def _params(*semantics):
    return pltpu.CompilerParams(dimension_semantics=semantics, vmem_limit_bytes=V7X_VMEM_LIMIT_BYTES)


def _dot(a, b):
    return jnp.dot(a, b, preferred_element_type=F32)


def _sigmoid(x):
    return 0.5 * jnp.tanh(0.5 * x) + 0.5


def _pack_halves(x):
    n = x.shape[1] // 2
    lo = pltpu.bitcast(x[:, :n].astype(BF16).astype(F32), U32)
    hi = pltpu.bitcast(x[:, n:].astype(BF16).astype(F32), U32)
    return (lo >> 16) | (hi & jnp.uint32(HI16))


def _unpack_halves(u):
    return pltpu.bitcast(u << 16, F32), pltpu.bitcast(u & jnp.uint32(HI16), F32)


def _rmsnorm_kernel(x_ref, g_ref, o_ref):
    x = x_ref[...].astype(F32)
    ms = jnp.mean(x * x, axis=-1, keepdims=True)
    o_ref[...] = (x * lax.rsqrt(ms + EPS) * g_ref[...]).astype(o_ref.dtype)


def _rmsnorm(x, gain, *, tm=512):
    m, d = x.shape
    return pl.pallas_call(
        _rmsnorm_kernel,
        grid=(m // tm,),
        in_specs=[pl.BlockSpec((tm, d), lambda i: (i, 0)), pl.BlockSpec((1, d), lambda i: (0, 0))],
        out_specs=pl.BlockSpec((tm, d), lambda i: (i, 0)),
        out_shape=jax.ShapeDtypeStruct((m, d), BF16),
        compiler_params=_params("parallel"),
        name="rmsnorm",
    )(x, gain.reshape(1, d).astype(F32))


def _mm_kernel(x_ref, w_hbm, o_ref, stage, w_s, sem, *, layer, tn):
    j, i = pl.program_id(0), pl.program_id(1)

    def weight_copy(col_tile):
        return pltpu.make_async_copy(w_hbm.at[layer, :, pl.ds(col_tile * tn, tn)], stage, sem)

    @pl.when(jnp.logical_and(j == 0, i == 0))
    def _():
        weight_copy(0).start()

    @pl.when(i == 0)
    def _():
        weight_copy(0).wait()
        w_s[...] = stage[...].astype(BF16)

        @pl.when(j + 1 < pl.num_programs(0))
        def _():
            weight_copy(j + 1).start()

    o_ref[...] = _dot(x_ref[...], w_s[...]).astype(o_ref.dtype)


def _layer_spec(layer, block, index_map):
    return pl.BlockSpec((None,) + tuple(block), lambda *a: (layer,) + tuple(index_map(*a)))


def _matmul(x, w, layer, *, tm=1024, tn=1024, out_dtype=BF16):
    m, k = x.shape
    n = w.shape[2]
    return pl.pallas_call(
        functools.partial(_mm_kernel, layer=layer, tn=tn),
        grid=(n // tn, m // tm),
        in_specs=[pl.BlockSpec((tm, k), lambda j, i: (i, 0)), pl.BlockSpec(memory_space=pl.ANY)],
        out_specs=pl.BlockSpec((tm, tn), lambda j, i: (i, j)),
        out_shape=jax.ShapeDtypeStruct((m, n), out_dtype),
        scratch_shapes=[pltpu.VMEM((k, tn), F32), pltpu.VMEM((k, tn), BF16), pltpu.SemaphoreType.DMA],
        compiler_params=_params("arbitrary", "arbitrary"),
        name="matmul",
    )(x, w)


def _mm_residual_kernel(h_ref, x_ref, w_ref, o_ref):
    o_ref[...] = h_ref[...] + _dot(x_ref[...], w_ref[...])


def _matmul_residual(h, x, w, layer, *, in_place, tm=1024, tn=1024):
    m, k = x.shape
    n = w.shape[2]
    return pl.pallas_call(
        _mm_residual_kernel,
        grid=(m // tm, n // tn),
        in_specs=[
            pl.BlockSpec((tm, tn), lambda i, j: (i, j)),
            pl.BlockSpec((tm, k), lambda i, j: (i, 0)),
            _layer_spec(layer, (k, tn), lambda i, j: (0, j)),
        ],
        out_specs=pl.BlockSpec((tm, tn), lambda i, j: (i, j)),
        out_shape=jax.ShapeDtypeStruct((m, n), F32),
        input_output_aliases={0: 0} if in_place else {},
        compiler_params=_params("parallel", "arbitrary"),
        name="matmul_residual",
    )(h, x, w)


def _ple_kernel(h_ref, x_ref, wg_ref, p_ref, we_ref, o_ref):
    gate = _sigmoid(_dot(x_ref[...], wg_ref[...]))
    emb = _dot(p_ref[...], we_ref[...])
    o_ref[...] = h_ref[...] + gate * emb


def _ple(h, xn, w_gate, p, w_ple, layer, *, tm=1024, tn=512):
    m, k = xn.shape
    n = w_gate.shape[2]
    kp = p.shape[2]
    return pl.pallas_call(
        _ple_kernel,
        grid=(m // tm, n // tn),
        in_specs=[
            pl.BlockSpec((tm, tn), lambda i, j: (i, j)),
            pl.BlockSpec((tm, k), lambda i, j: (i, 0)),
            _layer_spec(layer, (k, tn), lambda i, j: (0, j)),
            _layer_spec(layer, (tm, kp), lambda i, j: (i, 0)),
            _layer_spec(layer, (kp, tn), lambda i, j: (0, j)),
        ],
        out_specs=pl.BlockSpec((tm, tn), lambda i, j: (i, j)),
        out_shape=jax.ShapeDtypeStruct((m, n), F32),
        input_output_aliases={0: 0},
        compiler_params=_params("parallel", "arbitrary"),
        name="ple",
    )(h, xn, w_gate, p, w_ple)


def _merge_kernel(yl_ref, ya_ref, ys_ref, pl_ref, pa_ref, ps_ref, gl_ref, ga_ref, gs_ref, o_ref):
    acc = _sigmoid(gl_ref[...].astype(F32)) * _dot(yl_ref[...], pl_ref[...])
    acc += _sigmoid(ga_ref[...].astype(F32)) * _dot(ya_ref[...], pa_ref[...])
    acc += _sigmoid(gs_ref[...].astype(F32)) * _dot(ys_ref[...], ps_ref[...])
    o_ref[...] = acc.astype(o_ref.dtype)


def _gated_merge(y_lru, y_att, y_ssm, p_lru, p_att, p_ssm, layer, proj, gate_off, *, tm=1024, tn=512):
    m = y_lru.shape[0]
    d = p_lru.shape[2]
    goff = gate_off // tn
    nd = d // tn

    def y_spec(y):
        return pl.BlockSpec((tm, y.shape[1]), lambda i, j: (i, 0))

    def p_spec(p):
        return _layer_spec(layer, (p.shape[1], tn), lambda i, j: (0, j))

    def g_spec(b):
        return pl.BlockSpec((tm, tn), lambda i, j: (i, goff + b * nd + j))

    return pl.pallas_call(
        _merge_kernel,
        grid=(m // tm, nd),
        in_specs=[y_spec(y_lru), y_spec(y_att), y_spec(y_ssm), p_spec(p_lru), p_spec(p_att), p_spec(p_ssm),
                  g_spec(0), g_spec(1), g_spec(2)],
        out_specs=pl.BlockSpec((tm, tn), lambda i, j: (i, j)),
        out_shape=jax.ShapeDtypeStruct((m, d), BF16),
        compiler_params=_params("parallel", "arbitrary"),
        name="gated_merge",
    )(y_lru, y_att, y_ssm, p_lru, p_att, p_ssm, proj, proj, proj)


def _scan_rows8(a8, x8, hprev, row):
    for k in (1, 2, 4):
        keep = row >= k
        a_sh = jnp.where(keep, pltpu.roll(a8, k, 0), 1.0)
        x_sh = jnp.where(keep, pltpu.roll(x8, k, 0), 0.0)
        x8 = a8 * x_sh + x8
        a8 = a8 * a_sh
    return a8 * hprev + x8


def _lru_kernel(u_ref, cw_ref, cb_ref, wr_ref, br_ref, wi_ref, bi_ref, lam_ref, y_ref,
                tail_ref, h_ref, a_s, x_s, *, t_rows, n_blocks, block, conv_width):
    @pl.when(pl.program_id(1) == 0)
    def _():
        tail_ref[...] = jnp.zeros_like(tail_ref)
        h_ref[...] = jnp.zeros_like(h_ref)

    u = u_ref[0].astype(F32)
    ue = jnp.concatenate([tail_ref[...], u], axis=0)
    xc = cb_ref[...]
    for j in range(conv_width):
        off = 8 - (conv_width - 1) + j
        xc = xc + cw_ref[j:j + 1, :] * ue[off:off + t_rows, :]
    tail_ref[...] = u[t_rows - 8:, :]

    xcb = xc.astype(BF16)
    rs, igs = [], []
    for hb in range(n_blocks):
        xb = xcb[:, hb * block:(hb + 1) * block]
        rs.append(_dot(xb, wr_ref[hb]))
        igs.append(_dot(xb, wi_ref[hb]))
    r = _sigmoid(jnp.concatenate(rs, axis=1) + br_ref[...])
    ig = _sigmoid(jnp.concatenate(igs, axis=1) + bi_ref[...])
    lam = lam_ref[...]
    softplus_neg_lam = jnp.maximum(-lam, 0.0) + jnp.log1p(jnp.exp(-jnp.abs(lam)))
    log_a = (-LRU_C) * r * softplus_neg_lam
    a = jnp.exp(log_a)
    a_s[...] = a
    x_s[...] = jnp.sqrt(-jnp.tanh(log_a) * (a * a + 1.0)) * (ig * xc)

    w = a_s.shape[1]
    row = lax.broadcasted_iota(jnp.int32, (8, w), 0)

    def body(i, hprev):
        r0 = pl.multiple_of(i * 16, 16)
        h_a = _scan_rows8(a_s[pl.ds(r0, 8), :], x_s[pl.ds(r0, 8), :], hprev, row)
        h_b = _scan_rows8(a_s[pl.ds(r0 + 8, 8), :], x_s[pl.ds(r0 + 8, 8), :], h_a[7:8, :], row)
        y_ref[0, pl.ds(r0, 16), :] = jnp.concatenate([h_a, h_b], axis=0).astype(y_ref.dtype)
        return h_b[7:8, :]

    h_ref[...] = lax.fori_loop(0, t_rows // 16, body, h_ref[...])


def _rglru(proj3, conv_w, conv_b, w_rg, b_rg, w_ig, b_ig, lam, *, t_rows=256):
    bsz, seq, _ = proj3.shape
    n_blocks, block, _ = w_rg.shape
    w = n_blocks * block
    cw = conv_w.shape[0]
    vec = lambda v: v.reshape(1, w).astype(F32)
    full2 = lambda r, c: pl.BlockSpec((r, c), lambda b, t: (0, 0))
    full3 = pl.BlockSpec((n_blocks, block, block), lambda b, t: (0, 0, 0))
    return pl.pallas_call(
        functools.partial(_lru_kernel, t_rows=t_rows, n_blocks=n_blocks, block=block, conv_width=cw),
        grid=(bsz, seq // t_rows),
        in_specs=[pl.BlockSpec((1, t_rows, w), lambda b, t: (b, t, 0)), full2(cw, w), full2(1, w),
                  full3, full2(1, w), full3, full2(1, w), full2(1, w)],
        out_specs=pl.BlockSpec((1, t_rows, w), lambda b, t: (b, t, 0)),
        out_shape=jax.ShapeDtypeStruct((bsz, seq, w), BF16),
        scratch_shapes=[pltpu.VMEM((8, w), F32), pltpu.VMEM((1, w), F32),
                        pltpu.VMEM((t_rows, w), F32), pltpu.VMEM((t_rows, w), F32)],
        compiler_params=_params("parallel", "arbitrary"),
        name="rglru",
    )(proj3, conv_w.astype(F32), vec(conv_b), w_rg.astype(BF16), vec(b_rg), w_ig.astype(BF16), vec(b_ig),
      vec(lam))


def _head_rms(x, gain):
    x = x.astype(F32)
    return x * lax.rsqrt(jnp.mean(x * x, axis=-1, keepdims=True) + EPS) * gain


def _attn_kernel(q_ref, kp_ref, kc_ref, vp_ref, vc_ref, bvec_ref, qg_ref, kg_ref, o_ref, bias_s, *, qb, sub, span):
    first = pl.program_id(2) == 0

    @pl.when(first)
    def _():
        vec = jnp.broadcast_to(bvec_ref[0], (sub, ATTN_ROLL_WIDTH))
        table = pltpu.roll(vec, 0, 1, stride=1, stride_axis=0)[:, :span]
        q_chunk = (lax.broadcasted_iota(jnp.int32, table.shape, 0) >> CHUNK_SHIFT) + LEFT_CHUNKS
        k_chunk = lax.broadcasted_iota(jnp.int32, table.shape, 1) >> CHUNK_SHIFT
        in_band = jnp.logical_and(k_chunk >= q_chunk - LEFT_CHUNKS, k_chunk <= q_chunk)
        bias_s[...] = jnp.where(in_band, table, NEG_INF)

    qn = (_head_rms(q_ref[0], qg_ref[...]) * (HEAD_DIM ** -0.5)).astype(BF16)
    kn = _head_rms(jnp.concatenate([kp_ref[0], kc_ref[0]], axis=0), kg_ref[...]).astype(BF16)
    v = jnp.concatenate([vp_ref[0], vc_ref[0]], axis=0)
    bias = bias_s[...]
    outs = []
    for part in range(qb // sub):
        c0 = part * sub
        s = lax.dot_general(qn[c0:c0 + sub], kn[c0:c0 + span], (((1,), (1,)), ((), ())),
                            preferred_element_type=F32)
        s = s + bias
        col = c0 + lax.broadcasted_iota(jnp.int32, s.shape, 1)
        s = jnp.where(jnp.logical_and(first, col < ATTN_PREV), NEG_INF, s)
        p = jnp.exp(s - jnp.max(s, axis=-1, keepdims=True))
        denom = jnp.sum(p, axis=-1, keepdims=True)
        outs.append(_dot(p.astype(BF16), v[c0:c0 + span]) / denom)
    o_ref[0] = jnp.concatenate(outs, axis=0).astype(o_ref.dtype)


def _bias_vector(rel_bias, sub):
    mm = jnp.arange(ATTN_ROLL_WIDTH)
    offset = jnp.where(mm < ATTN_ROLL_WIDTH - sub, mm, mm - ATTN_ROLL_WIDTH)
    dist = ATTN_PREV - offset
    return rel_bias.astype(F32)[:, None, jnp.clip(dist, -REL_CLIP, REL_CLIP) + REL_CLIP]


def _attention(proj3, q_off, k_off, v_off, n_heads, q_gain, k_gain, rel_bias, *, qb=ATTN_QBLOCK, sub=ATTN_SUB):
    span = sub + ATTN_PREV
    assert qb % ATTN_PREV == 0 and qb % sub == 0 and sub % CHUNK == 0 and sub % LANES == 0
    assert span + sub <= ATTN_ROLL_WIDTH
    bsz, seq, _ = proj3.shape
    qo, ko, vo = q_off // HEAD_DIM, k_off // HEAD_DIM, v_off // HEAD_DIM
    ratio = qb // ATTN_PREV
    cur = lambda off: pl.BlockSpec((1, qb, HEAD_DIM), lambda b, h, n: (b, n, off + h))
    prev = lambda off: pl.BlockSpec((1, ATTN_PREV, HEAD_DIM),
                                    lambda b, h, n: (b, jnp.maximum(n * ratio - 1, 0), off + h))
    gain = pl.BlockSpec((1, HEAD_DIM), lambda b, h, n: (0, 0))
    return pl.pallas_call(
        functools.partial(_attn_kernel, qb=qb, sub=sub, span=span),
        grid=(bsz, n_heads, seq // qb),
        in_specs=[cur(qo), prev(ko), cur(ko), prev(vo), cur(vo),
                  pl.BlockSpec((1, 1, ATTN_ROLL_WIDTH), lambda b, h, n: (h, 0, 0)), gain, gain],
        out_specs=pl.BlockSpec((1, qb, HEAD_DIM), lambda b, h, n: (b, n, h)),
        out_shape=jax.ShapeDtypeStruct((bsz, seq, n_heads * HEAD_DIM), BF16),
        scratch_shapes=[pltpu.VMEM((sub, span), F32)],
        compiler_params=_params("parallel", "parallel", "arbitrary"),
        name="band_attention",
    )(proj3, proj3, proj3, proj3, proj3, _bias_vector(rel_bias, sub),
      q_gain.reshape(1, HEAD_DIM).astype(F32), k_gain.reshape(1, HEAD_DIM).astype(F32))


def _s5_tables(a_re, a_im, log_dt, b_re, b_im, c_re, c_im, d_skip, t_chunk):
    hi = lax.Precision.HIGHEST
    g, p = a_re.shape
    c = b_re.shape[-1]
    gl = LANES // c
    nj = g // gl
    dt = jnp.exp(log_dt.astype(F32))[:, None]
    ar, ai = a_re.astype(F32), a_im.astype(F32)

    def apow(tau):
        tau = jnp.asarray(tau, F32)[:, None, None]
        mag = jnp.exp(tau * dt * ar)
        return mag * jnp.cos(tau * dt * ai), mag * jnp.sin(tau * dt * ai)

    pr, pi = apow(jnp.arange(t_chunk + 1))
    abar_re, abar_im = pr[1], pi[1]
    den = ar * ar + ai * ai
    nr, ni = abar_re - 1.0, abar_im
    coef_re = (nr * ar + ni * ai) / den
    coef_im = (ni * ar - nr * ai) / den
    br, bi = b_re.astype(F32), b_im.astype(F32)
    bbar_re = coef_re[..., None] * br - coef_im[..., None] * bi
    bbar_im = coef_re[..., None] * bi + coef_im[..., None] * br
    cr, ci = c_re.astype(F32), c_im.astype(F32)
    ca_re = cr[None] * pr[:, :, None, :] - ci[None] * pi[:, :, None, :]
    ca_im = cr[None] * pi[:, :, None, :] + ci[None] * pr[:, :, None, :]
    lane = jnp.arange(LANES)
    col = jnp.arange(gl * p)
    rep_c = (lane[None, :] % c == jnp.arange(c)[:, None]).astype(BF16)
    rep_p = (col[None, :] % p == jnp.arange(p)[:, None]).astype(BF16)
    lane_lane = lane[:, None] // c == lane[None, :] // c
    lane_col = lane[:, None] // c == col[None, :] // p
    col_lane = col[:, None] // p == lane[None, :] // c

    k_tap = (jnp.einsum("tgcp,gpd->tgcd", ca_re[:t_chunk], bbar_re, precision=hi)
             - jnp.einsum("tgcp,gpd->tgcd", ca_im[:t_chunk], bbar_im, precision=hi))
    lag = jnp.arange(t_chunk)[None, :] - jnp.arange(t_chunk)[:, None]
    k_st = jnp.where((lag >= 0)[:, :, None, None, None], k_tap[jnp.maximum(lag, 0)], 0.0)
    k_rows = jnp.transpose(k_st.reshape(t_chunk, t_chunk, nj, gl, c, c), (2, 0, 1, 3, 5, 4))
    k_rep = jnp.dot(k_rows.reshape(-1, c).astype(BF16), rep_c).reshape(nj, t_chunk, t_chunk, LANES, LANES)
    k_rep = jnp.where(lane_lane[None, None, None], k_rep, 0)
    k_mat = jnp.transpose(k_rep, (0, 1, 3, 2, 4)).reshape(nj, t_chunk * LANES, t_chunk * LANES)

    rev = t_chunk - 1 - jnp.arange(t_chunk)
    s_re = pr[rev][..., None] * bbar_re[None] - pi[rev][..., None] * bbar_im[None]
    s_im = pr[rev][..., None] * bbar_im[None] + pi[rev][..., None] * bbar_re[None]

    def inc_mat(x):
        rows = jnp.transpose(x.reshape(t_chunk, nj, gl, p, c), (1, 0, 2, 4, 3))
        rep = jnp.dot(rows.reshape(-1, p).astype(BF16), rep_p).reshape(nj, t_chunk, LANES, gl * p)
        return jnp.where(lane_col[None, None], rep, 0).reshape(nj, t_chunk * LANES, gl * p)

    w_in = jnp.concatenate([k_mat, inc_mat(s_re), inc_mat(s_im)], axis=-1)

    def out_mat(x):
        rows = jnp.transpose(x.reshape(t_chunk, nj, gl, c, p), (1, 2, 4, 0, 3))
        rep = jnp.dot(rows.reshape(-1, c).astype(BF16), rep_c).reshape(nj, gl * p, t_chunk, LANES)
        return jnp.where(col_lane[None, :, None, :], rep, 0).reshape(nj, gl * p, t_chunk * LANES)

    w_state = jnp.concatenate([out_mat(ca_re[1:]), out_mat(-ca_im[1:])], axis=1)

    qr, qi = apow(t_chunk * jnp.arange(1, 9))
    lanes = lambda x: jnp.transpose(x.reshape(8, nj, gl * p), (1, 0, 2))
    consts = jnp.concatenate([lanes(qr), lanes(qi)], axis=1)
    d_row = jnp.tile(d_skip.astype(F32).reshape(nj, 1, LANES), (1, 1, t_chunk))
    return w_in, w_state, consts, d_row


def _gelu_tanh(y):
    return 0.5 * y * (1.0 + jnp.tanh(math.sqrt(2.0 / math.pi) * (y + 0.044715 * (y * y * y))))


def _s5_kernel(x_ref, win_ref, wst_ref, cst_ref, d_ref, y_ref, stage, inc_re, inc_im, hp_re, hp_im,
               *, t_chunk, n_rows, tc, gp):
    stage[...] = x_ref[...].astype(F32)
    u = jnp.concatenate([stage[pl.ds(s, n_rows, stride=t_chunk), :] for s in range(t_chunk)], axis=1)
    z = _dot(u.astype(BF16), win_ref[0])
    inc_re[...] = z[:, tc:tc + gp]
    inc_im[...] = z[:, tc + gp:]
    cst = cst_ref[0]
    c_re, c_im = cst[0:8, :], cst[8:16, :]
    row = lax.broadcasted_iota(jnp.int32, (8, gp), 0)

    def body(i, carry):
        h_re, h_im = carry
        r0 = pl.multiple_of(i * 8, 8)
        x_re = inc_re[pl.ds(r0, 8), :]
        x_im = inc_im[pl.ds(r0, 8), :]
        for k in (1, 2, 4):
            keep = row >= k
            s_re = jnp.where(keep, pltpu.roll(x_re, k, 0), 0.0)
            s_im = jnp.where(keep, pltpu.roll(x_im, k, 0), 0.0)
            m_re, m_im = c_re[k - 1:k, :], c_im[k - 1:k, :]
            x_re, x_im = x_re + m_re * s_re - m_im * s_im, x_im + m_re * s_im + m_im * s_re
        o_re = x_re + c_re * h_re - c_im * h_im
        o_im = x_im + c_re * h_im + c_im * h_re
        first = row >= 1
        hp_re[pl.ds(r0, 8), :] = jnp.where(first, pltpu.roll(o_re, 1, 0), h_re)
        hp_im[pl.ds(r0, 8), :] = jnp.where(first, pltpu.roll(o_im, 1, 0), h_im)
        return o_re[7:8, :], o_im[7:8, :]

    zero = jnp.zeros((1, gp), F32)
    lax.fori_loop(0, n_rows // 8, body, (zero, zero))
    h_prev = jnp.concatenate([hp_re[...], hp_im[...]], axis=1).astype(BF16)
    y = _gelu_tanh(z[:, :tc] + _dot(h_prev, wst_ref[0]) + d_ref[0] * u)
    for t in range(t_chunk):
        stage[pl.ds(t, n_rows, stride=t_chunk), :] = y[:, t * LANES:(t + 1) * LANES]
    y_ref[...] = stage[...].astype(y_ref.dtype)


def _s5(proj, ssm_off, ssm_w, bsz, tables, *, t_chunk=S5_CHUNK):
    w_in, w_state, consts, d_row = tables
    m = proj.shape[0]
    seq = m // bsz
    nj = ssm_w // LANES
    n_rows = seq // t_chunk
    tc = t_chunk * LANES
    gp = w_state.shape[1] // 2
    jb0 = ssm_off // LANES
    return pl.pallas_call(
        functools.partial(_s5_kernel, t_chunk=t_chunk, n_rows=n_rows, tc=tc, gp=gp),
        grid=(nj, bsz),
        in_specs=[
            pl.BlockSpec((seq, LANES), lambda j, b: (b, jb0 + j)),
            pl.BlockSpec((1, tc, tc + 2 * gp), lambda j, b: (j, 0, 0)),
            pl.BlockSpec((1, 2 * gp, tc), lambda j, b: (j, 0, 0)),
            pl.BlockSpec((1, 16, gp), lambda j, b: (j, 0, 0)),
            pl.BlockSpec((1, 1, tc), lambda j, b: (j, 0, 0)),
        ],
        out_specs=pl.BlockSpec((seq, LANES), lambda j, b: (b, j)),
        out_shape=jax.ShapeDtypeStruct((m, ssm_w), BF16),
        scratch_shapes=[pltpu.VMEM((seq, LANES), F32)] + [pltpu.VMEM((n_rows, gp), F32) for _ in range(4)],
        compiler_params=_params("parallel", "parallel"),
        name="s5",
    )(proj, w_in, w_state, consts, d_row)


def _glu_kernel(x_ref, w_ref, b_ref, o_ref, *, n):
    z = _dot(x_ref[...], w_ref[...]) + b_ref[...]
    o_ref[...] = (z[:, :n] * _sigmoid(z[:, n:])).astype(o_ref.dtype)


def _glu(x, w, layer, b, *, tm=1024):
    m, k = x.shape
    n2 = w.shape[2]
    n = n2 // 2
    return pl.pallas_call(
        functools.partial(_glu_kernel, n=n),
        grid=(m // tm,),
        in_specs=[
            pl.BlockSpec((tm, k), lambda i: (i, 0)),
            _layer_spec(layer, (k, n2), lambda i: (0, 0)),
            pl.BlockSpec((1, n2), lambda i: (0, 0)),
        ],
        out_specs=pl.BlockSpec((tm, n), lambda i: (i, 0)),
        out_shape=jax.ShapeDtypeStruct((m, n), BF16),
        compiler_params=_params("parallel"),
        name="glu",
    )(x, w, b.reshape(1, n2).astype(F32))


def _route_kernel(h_ref, g_ref, w_ref, b_ref, xp_ref, info_ref, *, n_groups, per_group):
    x = h_ref[...]
    xn = x * lax.rsqrt(jnp.mean(x * x, axis=-1, keepdims=True) + EPS) * g_ref[...]
    xp_ref[...] = _pack_halves(xn)
    n_exp = n_groups * per_group
    logits = _dot(xn.astype(BF16), w_ref[...]) + b_ref[...]
    lane = lax.broadcasted_iota(jnp.int32, logits.shape, 1).astype(F32)
    big = float(LANES)
    is_group = jnp.logical_and(lane >= n_exp, lane < n_exp + n_groups)
    gl = jnp.where(is_group, logits, -jnp.inf)
    gmax = jnp.max(gl, axis=-1, keepdims=True)
    gsel = jnp.min(jnp.where(gl == gmax, lane, big), axis=-1, keepdims=True) - n_exp
    gprob = 1.0 / jnp.sum(jnp.where(is_group, jnp.exp(logits - gmax), 0.0), axis=-1, keepdims=True)
    in_group = jnp.logical_and(lane >= gsel * per_group, lane < (gsel + 1.0) * per_group)
    el = jnp.where(in_group, logits, -jnp.inf)
    v1 = jnp.max(el, axis=-1, keepdims=True)
    i1 = jnp.min(jnp.where(el == v1, lane, big), axis=-1, keepdims=True)
    el2 = jnp.where(lane == i1, -jnp.inf, el)
    v2 = jnp.max(el2, axis=-1, keepdims=True)
    i2 = jnp.min(jnp.where(el2 == v2, lane, big), axis=-1, keepdims=True)
    e2 = jnp.exp(v2 - v1)
    w1 = gprob / (1.0 + e2)
    w2 = gprob * e2 / (1.0 + e2)
    info_ref[...] = (jnp.where(lane == 0.0, i1, 0.0) + jnp.where(lane == 1.0, i2, 0.0)
                     + jnp.where(lane == 2.0, w1, 0.0) + jnp.where(lane == 3.0, w2, 0.0))


def _route(h, gain, w_gr, b_gr, w_er, b_er, *, tm=512):
    m, d = h.shape
    n_groups, _, per_group = w_er.shape
    n_exp = n_groups * per_group
    assert n_exp + n_groups <= LANES
    w = jnp.concatenate([jnp.transpose(w_er, (1, 0, 2)).reshape(d, n_exp), w_gr], axis=1)
    w = jnp.pad(w, ((0, 0), (0, LANES - n_exp - n_groups))).astype(BF16)
    b = jnp.concatenate([b_er.reshape(n_exp), b_gr]).astype(F32)
    b = jnp.pad(b, (0, LANES - n_exp - n_groups)).reshape(1, LANES)
    return pl.pallas_call(
        functools.partial(_route_kernel, n_groups=n_groups, per_group=per_group),
        grid=(m // tm,),
        in_specs=[pl.BlockSpec((tm, d), lambda i: (i, 0)), pl.BlockSpec((1, d), lambda i: (0, 0)),
                  pl.BlockSpec((d, LANES), lambda i: (0, 0)), pl.BlockSpec((1, LANES), lambda i: (0, 0))],
        out_specs=[pl.BlockSpec((tm, d // 2), lambda i: (i, 0)), pl.BlockSpec((tm, LANES), lambda i: (i, 0))],
        out_shape=[jax.ShapeDtypeStruct((m, d // 2), U32), jax.ShapeDtypeStruct((m, LANES), F32)],
        compiler_params=_params("parallel"),
        name="moe_route",
    )(h, gain.reshape(1, d).astype(F32), w, b)


def _sorted_layout(expert_ids, n_exp, row_tile, n_tiles):
    e = expert_ids.reshape(-1)
    blk = LANES
    nb = e.shape[0] // blk
    onehot = (e[:, None] == jnp.arange(n_exp, dtype=jnp.int32)[None, :]).reshape(nb, blk, n_exp)
    tri = jnp.tril(jnp.ones((blk, blk), BF16))
    within = jnp.einsum("ij,bjk->bik", tri, onehot.astype(BF16), preferred_element_type=F32)
    totals = within[:, -1, :]
    before = jnp.dot(jnp.tril(jnp.ones((nb, nb), F32), -1), totals, precision=lax.Precision.HIGHEST)
    counts = (before[-1] + totals[-1]).astype(jnp.int32)
    rank = jnp.sum(jnp.where(onehot, within + before[:, None, :], 0.0), axis=-1).reshape(-1).astype(jnp.int32) - 1
    padded = ((counts + row_tile - 1) // row_tile) * row_tile
    ends = jnp.cumsum(padded)
    starts = ends - padded
    pos = starts[e] + rank
    tile_start = jnp.arange(n_tiles, dtype=jnp.int32) * row_tile
    tile_expert = jnp.minimum(jnp.sum((tile_start[:, None] >= ends[None, :]).astype(jnp.int32), axis=1), n_exp - 1)
    n_used = (ends[-1] // row_tile).astype(jnp.int32).reshape(1)
    return pos.astype(jnp.int32), tile_expert.astype(jnp.int32), n_used


def _row_copy(src_ref, src_row, dst_ref, dst_row, sem):
    return pltpu.make_async_copy(src_ref.at[pl.ds(src_row, 1)], dst_ref.at[pl.ds(dst_row, 1)], sem)


def _start_row_copies(n_rows, start_row):
    def body(t, c):
        start_row(t)
        return c

    lax.fori_loop(0, n_rows, body, 0, unroll=ROW_DMA_UNROLL)


def _wait_row_copies(n_copies, example_copy):
    for _ in range(n_copies):
        example_copy.wait()


def _dispatch_kernel(pos_ref, x_ref, init_ref, out_ref, sem, *, tt, n_slots):
    del init_ref

    def start_row(t):
        for k in range(n_slots):
            _row_copy(x_ref, t, out_ref, pos_ref[0, 0, n_slots * t + k], sem).start(priority=k % 2)

    _start_row_copies(tt, start_row)
    _wait_row_copies(tt * n_slots, _row_copy(x_ref, 0, out_ref, 0, sem))


def _dispatch(xp, pos, n_rows, *, tt=MOE_TOKEN_TILE, n_slots=TOP_K):
    m, d2 = xp.shape
    pos3 = pos.reshape(m // tt, 1, tt * n_slots)
    return pl.pallas_call(
        functools.partial(_dispatch_kernel, tt=tt, n_slots=n_slots),
        grid=(m // tt,),
        in_specs=[pl.BlockSpec((1, 1, tt * n_slots), lambda i: (i, 0, 0), memory_space=pltpu.SMEM),
                  pl.BlockSpec((tt, d2), lambda i: (i, 0)),
                  pl.BlockSpec(memory_space=pl.ANY)],
        out_specs=pl.BlockSpec(memory_space=pl.ANY),
        out_shape=jax.ShapeDtypeStruct((n_rows, d2), U32),
        scratch_shapes=[pltpu.SemaphoreType.DMA],
        input_output_aliases={2: 0},
        compiler_params=_params("arbitrary"),
        name="moe_dispatch",
    )(pos3, xp, jnp.zeros((n_rows, d2), U32))


def _expert_kernel(te_ref, first_ref, next_ref, nu_ref, x_ref, wg_hbm, wu_hbm, wd_hbm, o_ref,
                   stage_g, stage_u, stage_d, wg_s, wu_s, wd_s, sem, *, layer):
    t = pl.program_id(0)
    in_use = t < nu_ref[0]
    copies = ((wg_hbm, stage_g), (wu_hbm, stage_u), (wd_hbm, stage_d))

    def weight_copy(k, expert):
        src, dst = copies[k]
        return pltpu.make_async_copy(src.at[layer, expert], dst, sem.at[k])

    @pl.when(t == 0)
    def _():
        for k in range(len(copies)):
            weight_copy(k, te_ref[0]).start()

    @pl.when(jnp.logical_and(in_use, first_ref[t] == 1))
    def _():
        for k, dst in enumerate((wg_s, wu_s, wd_s)):
            weight_copy(k, 0).wait()
            dst[...] = copies[k][1][...].astype(BF16)

        @pl.when(next_ref[t] >= 0)
        def _():
            for k in range(len(copies)):
                weight_copy(k, next_ref[t]).start()

    @pl.when(in_use)
    def _():
        lo, hi = _unpack_halves(x_ref[...])
        x = jnp.concatenate([lo, hi], axis=1).astype(BF16)
        g = _dot(x, wg_s[...])
        u = _dot(x, wu_s[...])
        hid = (g * _sigmoid(g) * u).astype(BF16)
        o_ref[...] = _pack_halves(_dot(hid, wd_s[...]))

    @pl.when(jnp.logical_not(in_use))
    def _():
        o_ref[...] = jnp.zeros_like(o_ref)


def _expert_ffn(xs, tile_expert, n_used, w_gate, w_up, w_down, layer, *, tm=MOE_ROW_TILE):
    rows, d2 = xs.shape
    _, _, d, ff = w_gate.shape
    n_tiles = rows // tm
    idx = jnp.arange(n_tiles, dtype=jnp.int32)
    used = idx < n_used[0]
    first = jnp.logical_and(used, jnp.logical_or(idx == 0, tile_expert != jnp.roll(tile_expert, 1)))
    first_pos = jnp.where(first, idx, n_tiles)
    later = jnp.concatenate([lax.cummin(first_pos[::-1])[::-1][1:], jnp.full((1,), n_tiles, jnp.int32)])
    next_expert = jnp.where(later < n_tiles, tile_expert[jnp.minimum(later, n_tiles - 1)], -1).astype(jnp.int32)
    any_spec = pl.BlockSpec(memory_space=pl.ANY)
    grid_spec = pltpu.PrefetchScalarGridSpec(
        num_scalar_prefetch=4,
        grid=(n_tiles,),
        in_specs=[pl.BlockSpec((tm, d2), lambda t, *_: (t, 0)), any_spec, any_spec, any_spec],
        out_specs=pl.BlockSpec((tm, d2), lambda t, *_: (t, 0)),
        scratch_shapes=[pltpu.VMEM((d, ff), F32), pltpu.VMEM((d, ff), F32), pltpu.VMEM((ff, d), F32),
                        pltpu.VMEM((d, ff), BF16), pltpu.VMEM((d, ff), BF16), pltpu.VMEM((ff, d), BF16),
                        pltpu.SemaphoreType.DMA((3,))],
    )
    return pl.pallas_call(
        functools.partial(_expert_kernel, layer=layer),
        grid_spec=grid_spec,
        out_shape=jax.ShapeDtypeStruct((rows, d2), U32),
        compiler_params=_params("arbitrary"),
        name="moe_expert_ffn",
    )(tile_expert, first.astype(jnp.int32), next_expert, n_used, xs, w_gate, w_up, w_down)


def _combine_kernel(pos_ref, pos_next_ref, info_ref, h_ref, g_ref, y_ref, o_ref, on_ref, buf, sem, *, tt, n_slots):
    i = pl.program_id(0)
    cur = i % 2

    def gather(p_ref, half):
        def start_row(t):
            for k in range(n_slots):
                _row_copy(y_ref, p_ref[0, 0, n_slots * t + k], buf.at[half, k], t, sem.at[half]).start(priority=k % 2)

        _start_row_copies(tt, start_row)

    @pl.when(i == 0)
    def _():
        gather(pos_ref, 0)

    @pl.when(i + 1 < pl.num_programs(0))
    def _():
        gather(pos_next_ref, 1 - cur)

    _wait_row_copies(tt * n_slots, _row_copy(y_ref, 0, buf.at[cur, 0], 0, sem.at[cur]))
    info = info_ref[...]
    lane = lax.broadcasted_iota(jnp.int32, info.shape, 1)
    lo_acc = hi_acc = None
    for k in range(n_slots):
        wk = jnp.sum(jnp.where(lane == n_slots + k, info, 0.0), axis=-1, keepdims=True)
        lo, hi = _unpack_halves(buf[cur, k])
        lo_acc = wk * lo if lo_acc is None else lo_acc + wk * lo
        hi_acc = wk * hi if hi_acc is None else hi_acc + wk * hi
    out = h_ref[...] + jnp.concatenate([lo_acc, hi_acc], axis=1)
    o_ref[...] = out
    ms = jnp.mean(out * out, axis=-1, keepdims=True)
    on_ref[...] = (out * lax.rsqrt(ms + EPS) * g_ref[...]).astype(on_ref.dtype)


def _combine(h, info, ys, pos, next_gain, *, tt=MOE_TOKEN_TILE, n_slots=TOP_K):
    m, d = h.shape
    d2 = ys.shape[1]
    n_steps = m // tt
    pos3 = pos.reshape(n_steps, 1, tt * n_slots)
    return pl.pallas_call(
        functools.partial(_combine_kernel, tt=tt, n_slots=n_slots),
        grid=(n_steps,),
        in_specs=[pl.BlockSpec((1, 1, tt * n_slots), lambda i: (i, 0, 0), memory_space=pltpu.SMEM),
                  pl.BlockSpec((1, 1, tt * n_slots), lambda i: (jnp.minimum(i + 1, n_steps - 1), 0, 0),
                               memory_space=pltpu.SMEM),
                  pl.BlockSpec((tt, LANES), lambda i: (i, 0)),
                  pl.BlockSpec((tt, d), lambda i: (i, 0)),
                  pl.BlockSpec((1, d), lambda i: (0, 0)),
                  pl.BlockSpec(memory_space=pl.ANY)],
        out_specs=[pl.BlockSpec((tt, d), lambda i: (i, 0)), pl.BlockSpec((tt, d), lambda i: (i, 0))],
        out_shape=[jax.ShapeDtypeStruct((m, d), F32), jax.ShapeDtypeStruct((m, d), BF16)],
        scratch_shapes=[pltpu.VMEM((2, n_slots, tt, d2), U32), pltpu.SemaphoreType.DMA((2,))],
        input_output_aliases={3: 0},
        compiler_params=_params("arbitrary"),
        name="moe_combine",
    )(pos3, pos3, info, h, next_gain.reshape(1, d).astype(F32), ys)


def _moe(h, gain, w_gr, b_gr, w_er, b_er, w_gate, w_up, w_down, layer, next_gain):
    m = h.shape[0]
    n_exp = w_gate.shape[1]
    n_tiles = -(-(m * TOP_K + n_exp * (MOE_ROW_TILE - 1)) // MOE_ROW_TILE)
    xp, info = _route(h, gain, w_gr, b_gr, w_er, b_er)
    expert_ids = info[:, :TOP_K].astype(jnp.int32)
    pos, tile_expert, n_used = _sorted_layout(expert_ids, n_exp, MOE_ROW_TILE, n_tiles)
    xs = _dispatch(xp, pos, n_tiles * MOE_ROW_TILE)
    ys = _expert_ffn(xs, tile_expert, n_used, w_gate, w_up, w_down, layer)
    return _combine(h, info, ys, pos, next_gain)


def kernel(x, p, mix_gain, w_in, conv_w, conv_b, w_rgate, b_rgate, w_igate, b_igate, lru_lambda, q_gain, k_gain, rel_bias, ssm_a_re, ssm_a_im, ssm_log_dt, ssm_b_re, ssm_b_im, ssm_c_re, ssm_c_im, ssm_d, w_glu, b_glu, w_proj_lru, w_proj_att, w_proj_ssm, w_out, ffn_gain, w_group_router, b_group_router, w_expert_router, b_expert_router, w_up, w_gate, w_down, ple_gain, w_ple, w_ple_gate):
    bsz, seq, d = x.shape
    depth = w_in.shape[0]
    m = bsz * seq
    lru_w = w_proj_lru.shape[1]
    att_w = w_proj_att.shape[1]
    ssm_w = w_proj_ssm.shape[1]
    n_heads = att_w // HEAD_DIM
    q_off, k_off, v_off = lru_w, lru_w + att_w, lru_w + 2 * att_w
    ssm_off = lru_w + 3 * att_w
    gate_off = ssm_off + ssm_w
    in_width = w_in.shape[2]

    bf = lambda w: w.astype(BF16)
    w_glu, w_proj_lru, w_proj_att, w_proj_ssm, w_out = map(bf, (w_glu, w_proj_lru, w_proj_att, w_proj_ssm, w_out))
    w_ple_gate, w_ple = bf(w_ple_gate), bf(w_ple)
    p = bf(p.reshape(depth, m, p.shape[-1]))

    h = x.reshape(m, d).astype(F32)
    for i in range(depth):
        xn = _rmsnorm(h, mix_gain[i])
        proj = _matmul(xn, w_in, i)
        proj3 = proj.reshape(bsz, seq, in_width)
        y_lru = _rglru(proj3, conv_w[i], conv_b[i], w_rgate[i], b_rgate[i], w_igate[i], b_igate[i], lru_lambda[i])
        y_att = _attention(proj3, q_off, k_off, v_off, n_heads, q_gain[i], k_gain[i], rel_bias[i])
        tables = _s5_tables(ssm_a_re[i], ssm_a_im[i], ssm_log_dt[i], ssm_b_re[i], ssm_b_im[i],
                            ssm_c_re[i], ssm_c_im[i], ssm_d[i], S5_CHUNK)
        y_ssm = _glu(_s5(proj, ssm_off, ssm_w, bsz, tables), w_glu, i, b_glu[i])
        merged = _gated_merge(y_lru.reshape(m, lru_w), y_att.reshape(m, att_w), y_ssm,
                              w_proj_lru, w_proj_att, w_proj_ssm, i, proj, gate_off)
        h = _matmul_residual(h, merged, w_out, i, in_place=i > 0)
        h, hn = _moe(h, ffn_gain[i], w_group_router[i], b_group_router[i], w_expert_router[i], b_expert_router[i],
                     w_gate, w_up, w_down, i, ple_gain[i])
        h = _ple(h, hn, w_ple_gate, p, w_ple, i)
    return h.reshape(bsz, seq, d)
```

```python
import functools
import math

import jax
import jax.numpy as jnp
from jax import lax
from jax.experimental import pallas as pl
from jax.experimental.pallas import tpu as pltpu

F32 = jnp.float32
BF16 = jnp.bfloat16
U32 = jnp.uint32

CHUNK = 64
CHUNK_SHIFT = CHUNK.bit_length() - 1
assert 1 << CHUNK_SHIFT == CHUNK
LEFT_CHUNKS = 8
REL_CLIP = 128
HEAD_DIM = 128
LRU_C = 8.0
TOP_K = 2
EPS = 1e-6
NEG_INF = -1e30

LANES = 128
S5_CHUNK = 8
ATTN_PREV = LEFT_CHUNKS * CHUNK
ATTN_QBLOCK = 8 * ATTN_PREV
ATTN_SUB = 128
ATTN_ROLL_WIDTH = 1024
MOE_ROW_TILE = 256
MOE_TOKEN_TILE = 256
ROW_DMA_UNROLL = 8
V7X_VMEM_LIMIT_BYTES = 56 * 1024 * 1024
HI16 = 0xFFFF0000


def _params(*semantics):
    return pltpu.CompilerParams(dimension_semantics=semantics, vmem_limit_bytes=V7X_VMEM_LIMIT_BYTES)


def _dot(a, b):
    return jnp.dot(a, b, preferred_element_type=F32)


def _sigmoid(x):
    return 0.5 * jnp.tanh(0.5 * x) + 0.5


def _pack_halves(x):
    n = x.shape[1] // 2
    lo = pltpu.bitcast(x[:, :n].astype(BF16).astype(F32), U32)
    hi = pltpu.bitcast(x[:, n:].astype(BF16).astype(F32), U32)
    return (lo >> 16) | (hi & jnp.uint32(HI16))


def _unpack_halves(u):
    return pltpu.bitcast(u << 16, F32), pltpu.bitcast(u & jnp.uint32(HI16), F32)


def _rmsnorm_kernel(x_ref, g_ref, o_ref):
    x = x_ref[...].astype(F32)
    ms = jnp.mean(x * x, axis=-1, keepdims=True)
    o_ref[...] = (x * lax.rsqrt(ms + EPS) * g_ref[...]).astype(o_ref.dtype)


def _rmsnorm(x, gain, *, tm=512):
    m, d = x.shape
    return pl.pallas_call(
        _rmsnorm_kernel,
        grid=(m // tm,),
        in_specs=[pl.BlockSpec((tm, d), lambda i: (i, 0)), pl.BlockSpec((1, d), lambda i: (0, 0))],
        out_specs=pl.BlockSpec((tm, d), lambda i: (i, 0)),
        out_shape=jax.ShapeDtypeStruct((m, d), BF16),
        compiler_params=_params("parallel"),
        name="rmsnorm",
    )(x, gain.reshape(1, d).astype(F32))


def _mm_kernel(x_ref, w_hbm, o_ref, stage, w_s, sem, *, layer, tn):
    j, i = pl.program_id(0), pl.program_id(1)

    def weight_copy(col_tile):
        return pltpu.make_async_copy(w_hbm.at[layer, :, pl.ds(col_tile * tn, tn)], stage, sem)

    @pl.when(jnp.logical_and(j == 0, i == 0))
    def _():
        weight_copy(0).start()

    @pl.when(i == 0)
    def _():
        weight_copy(0).wait()
        w_s[...] = stage[...].astype(BF16)

        @pl.when(j + 1 < pl.num_programs(0))
        def _():
            weight_copy(j + 1).start()

    o_ref[...] = _dot(x_ref[...], w_s[...]).astype(o_ref.dtype)


def _layer_spec(layer, block, index_map):
    return pl.BlockSpec((None,) + tuple(block), lambda *a: (layer,) + tuple(index_map(*a)))


def _matmul(x, w, layer, *, tm=1024, tn=1024, out_dtype=BF16):
    m, k = x.shape
    n = w.shape[2]
    return pl.pallas_call(
        functools.partial(_mm_kernel, layer=layer, tn=tn),
        grid=(n // tn, m // tm),
        in_specs=[pl.BlockSpec((tm, k), lambda j, i: (i, 0)), pl.BlockSpec(memory_space=pl.ANY)],
        out_specs=pl.BlockSpec((tm, tn), lambda j, i: (i, j)),
        out_shape=jax.ShapeDtypeStruct((m, n), out_dtype),
        scratch_shapes=[pltpu.VMEM((k, tn), F32), pltpu.VMEM((k, tn), BF16), pltpu.SemaphoreType.DMA],
        compiler_params=_params("arbitrary", "arbitrary"),
        name="matmul",
    )(x, w)


def _mm_residual_kernel(h_ref, x_ref, w_ref, o_ref):
    o_ref[...] = h_ref[...] + _dot(x_ref[...], w_ref[...])


def _matmul_residual(h, x, w, layer, *, in_place, tm=1024, tn=1024):
    m, k = x.shape
    n = w.shape[2]
    return pl.pallas_call(
        _mm_residual_kernel,
        grid=(m // tm, n // tn),
        in_specs=[
            pl.BlockSpec((tm, tn), lambda i, j: (i, j)),
            pl.BlockSpec((tm, k), lambda i, j: (i, 0)),
            _layer_spec(layer, (k, tn), lambda i, j: (0, j)),
        ],
        out_specs=pl.BlockSpec((tm, tn), lambda i, j: (i, j)),
        out_shape=jax.ShapeDtypeStruct((m, n), F32),
        input_output_aliases={0: 0} if in_place else {},
        compiler_params=_params("parallel", "arbitrary"),
        name="matmul_residual",
    )(h, x, w)


def _ple_kernel(h_ref, x_ref, wg_ref, p_ref, we_ref, o_ref):
    gate = _sigmoid(_dot(x_ref[...], wg_ref[...]))
    emb = _dot(p_ref[...], we_ref[...])
    o_ref[...] = h_ref[...] + gate * emb


def _ple(h, xn, w_gate, p, w_ple, layer, *, tm=512, tn=1024):
    m, k = xn.shape
    n = w_gate.shape[2]
    kp = p.shape[2]
    return pl.pallas_call(
        _ple_kernel,
        grid=(m // tm, n // tn),
        in_specs=[
            pl.BlockSpec((tm, tn), lambda i, j: (i, j)),
            pl.BlockSpec((tm, k), lambda i, j: (i, 0)),
            _layer_spec(layer, (k, tn), lambda i, j: (0, j)),
            _layer_spec(layer, (tm, kp), lambda i, j: (i, 0)),
            _layer_spec(layer, (kp, tn), lambda i, j: (0, j)),
        ],
        out_specs=pl.BlockSpec((tm, tn), lambda i, j: (i, j)),
        out_shape=jax.ShapeDtypeStruct((m, n), F32),
        input_output_aliases={0: 0},
        compiler_params=_params("parallel", "arbitrary"),
        name="ple",
    )(h, xn, w_gate, p, w_ple)


def _merge_kernel(yl_ref, ya_ref, ys_ref, pl_ref, pa_ref, ps_ref, gl_ref, ga_ref, gs_ref, o_ref):
    acc = _sigmoid(gl_ref[...].astype(F32)) * _dot(yl_ref[...], pl_ref[...])
    acc += _sigmoid(ga_ref[...].astype(F32)) * _dot(ya_ref[...], pa_ref[...])
    acc += _sigmoid(gs_ref[...].astype(F32)) * _dot(ys_ref[...], ps_ref[...])
    o_ref[...] = acc.astype(o_ref.dtype)


def _gated_merge(y_lru, y_att, y_ssm, p_lru, p_att, p_ssm, layer, proj, gate_off, *, tm=1024, tn=512):
    m = y_lru.shape[0]
    d = p_lru.shape[2]
    goff = gate_off // tn
    nd = d // tn

    def y_spec(y):
        return pl.BlockSpec((tm, y.shape[1]), lambda i, j: (i, 0))

    def p_spec(p):
        return _layer_spec(layer, (p.shape[1], tn), lambda i, j: (0, j))

    def g_spec(b):
        return pl.BlockSpec((tm, tn), lambda i, j: (i, goff + b * nd + j))

    return pl.pallas_call(
        _merge_kernel,
        grid=(m // tm, nd),
        in_specs=[y_spec(y_lru), y_spec(y_att), y_spec(y_ssm), p_spec(p_lru), p_spec(p_att), p_spec(p_ssm),
                  g_spec(0), g_spec(1), g_spec(2)],
        out_specs=pl.BlockSpec((tm, tn), lambda i, j: (i, j)),
        out_shape=jax.ShapeDtypeStruct((m, d), BF16),
        compiler_params=_params("parallel", "arbitrary"),
        name="gated_merge",
    )(y_lru, y_att, y_ssm, p_lru, p_att, p_ssm, proj, proj, proj)


def _scan_rows8(a8, x8, hprev, row):
    for k in (1, 2, 4):
        keep = row >= k
        a_sh = jnp.where(keep, pltpu.roll(a8, k, 0), 1.0)
        x_sh = jnp.where(keep, pltpu.roll(x8, k, 0), 0.0)
        x8 = a8 * x_sh + x8
        a8 = a8 * a_sh
    return a8 * hprev + x8


def _lru_kernel(u_ref, cw_ref, cb_ref, wr_ref, br_ref, wi_ref, bi_ref, lam_ref, y_ref,
                tail_ref, h_ref, a_s, x_s, *, t_rows, n_blocks, block, conv_width):
    @pl.when(pl.program_id(1) == 0)
    def _():
        tail_ref[...] = jnp.zeros_like(tail_ref)
        h_ref[...] = jnp.zeros_like(h_ref)

    u = u_ref[0].astype(F32)
    ue = jnp.concatenate([tail_ref[...], u], axis=0)
    xc = cb_ref[...]
    for j in range(conv_width):
        off = 8 - (conv_width - 1) + j
        xc = xc + cw_ref[j:j + 1, :] * ue[off:off + t_rows, :]
    tail_ref[...] = u[t_rows - 8:, :]

    xcb = xc.astype(BF16)
    rs, igs = [], []
    for hb in range(n_blocks):
        xb = xcb[:, hb * block:(hb + 1) * block]
        rs.append(_dot(xb, wr_ref[hb]))
        igs.append(_dot(xb, wi_ref[hb]))
    r = _sigmoid(jnp.concatenate(rs, axis=1) + br_ref[...])
    ig = _sigmoid(jnp.concatenate(igs, axis=1) + bi_ref[...])
    lam = lam_ref[...]
    softplus_neg_lam = jnp.maximum(-lam, 0.0) + jnp.log1p(jnp.exp(-jnp.abs(lam)))
    log_a = (-LRU_C) * r * softplus_neg_lam
    a = jnp.exp(log_a)
    a_s[...] = a
    x_s[...] = jnp.sqrt(-jnp.tanh(log_a) * (a * a + 1.0)) * (ig * xc)

    w = a_s.shape[1]
    row = lax.broadcasted_iota(jnp.int32, (8, w), 0)

    def body(i, hprev):
        r0 = pl.multiple_of(i * 16, 16)
        h_a = _scan_rows8(a_s[pl.ds(r0, 8), :], x_s[pl.ds(r0, 8), :], hprev, row)
        h_b = _scan_rows8(a_s[pl.ds(r0 + 8, 8), :], x_s[pl.ds(r0 + 8, 8), :], h_a[7:8, :], row)
        y_ref[0, pl.ds(r0, 16), :] = jnp.concatenate([h_a, h_b], axis=0).astype(y_ref.dtype)
        return h_b[7:8, :]

    h_ref[...] = lax.fori_loop(0, t_rows // 16, body, h_ref[...])


def _rglru(proj3, conv_w, conv_b, w_rg, b_rg, w_ig, b_ig, lam, *, t_rows=256):
    bsz, seq, _ = proj3.shape
    n_blocks, block, _ = w_rg.shape
    w = n_blocks * block
    cw = conv_w.shape[0]
    vec = lambda v: v.reshape(1, w).astype(F32)
    full2 = lambda r, c: pl.BlockSpec((r, c), lambda b, t: (0, 0))
    full3 = pl.BlockSpec((n_blocks, block, block), lambda b, t: (0, 0, 0))
    return pl.pallas_call(
        functools.partial(_lru_kernel, t_rows=t_rows, n_blocks=n_blocks, block=block, conv_width=cw),
        grid=(bsz, seq // t_rows),
        in_specs=[pl.BlockSpec((1, t_rows, w), lambda b, t: (b, t, 0)), full2(cw, w), full2(1, w),
                  full3, full2(1, w), full3, full2(1, w), full2(1, w)],
        out_specs=pl.BlockSpec((1, t_rows, w), lambda b, t: (b, t, 0)),
        out_shape=jax.ShapeDtypeStruct((bsz, seq, w), BF16),
        scratch_shapes=[pltpu.VMEM((8, w), F32), pltpu.VMEM((1, w), F32),
                        pltpu.VMEM((t_rows, w), F32), pltpu.VMEM((t_rows, w), F32)],
        compiler_params=_params("parallel", "arbitrary"),
        name="rglru",
    )(proj3, conv_w.astype(F32), vec(conv_b), w_rg.astype(BF16), vec(b_rg), w_ig.astype(BF16), vec(b_ig),
      vec(lam))


def _head_rms(x, gain):
    x = x.astype(F32)
    return x * lax.rsqrt(jnp.mean(x * x, axis=-1, keepdims=True) + EPS) * gain


def _attn_kernel(q_ref, kp_ref, kc_ref, vp_ref, vc_ref, bvec_ref, qg_ref, kg_ref, o_ref, bias_s, *, qb, sub, span):
    first = pl.program_id(2) == 0

    @pl.when(first)
    def _():
        vec = jnp.broadcast_to(bvec_ref[0], (sub, ATTN_ROLL_WIDTH))
        table = pltpu.roll(vec, 0, 1, stride=1, stride_axis=0)[:, :span]
        q_chunk = (lax.broadcasted_iota(jnp.int32, table.shape, 0) >> CHUNK_SHIFT) + LEFT_CHUNKS
        k_chunk = lax.broadcasted_iota(jnp.int32, table.shape, 1) >> CHUNK_SHIFT
        in_band = jnp.logical_and(k_chunk >= q_chunk - LEFT_CHUNKS, k_chunk <= q_chunk)
        bias_s[...] = jnp.where(in_band, table, NEG_INF)

    qn = (_head_rms(q_ref[0], qg_ref[...]) * (HEAD_DIM ** -0.5)).astype(BF16)
    kn = _head_rms(jnp.concatenate([kp_ref[0], kc_ref[0]], axis=0), kg_ref[...]).astype(BF16)
    v = jnp.concatenate([vp_ref[0], vc_ref[0]], axis=0)
    bias = bias_s[...]
    outs = []
    for part in range(qb // sub):
        c0 = part * sub
        s = lax.dot_general(qn[c0:c0 + sub], kn[c0:c0 + span], (((1,), (1,)), ((), ())),
                            preferred_element_type=F32)
        s = s + bias
        col = c0 + lax.broadcasted_iota(jnp.int32, s.shape, 1)
        s = jnp.where(jnp.logical_and(first, col < ATTN_PREV), NEG_INF, s)
        p = jnp.exp(s - jnp.max(s, axis=-1, keepdims=True))
        denom = jnp.sum(p, axis=-1, keepdims=True)
        outs.append(_dot(p.astype(BF16), v[c0:c0 + span]) / denom)
    o_ref[0] = jnp.concatenate(outs, axis=0).astype(o_ref.dtype)


def _bias_vector(rel_bias, sub):
    mm = jnp.arange(ATTN_ROLL_WIDTH)
    offset = jnp.where(mm < ATTN_ROLL_WIDTH - sub, mm, mm - ATTN_ROLL_WIDTH)
    dist = ATTN_PREV - offset
    return rel_bias.astype(F32)[:, None, jnp.clip(dist, -REL_CLIP, REL_CLIP) + REL_CLIP]


def _attention(proj3, q_off, k_off, v_off, n_heads, q_gain, k_gain, rel_bias, *, qb=ATTN_QBLOCK, sub=ATTN_SUB):
    span = sub + ATTN_PREV
    assert qb % ATTN_PREV == 0 and qb % sub == 0 and sub % CHUNK == 0 and sub % LANES == 0
    assert span + sub <= ATTN_ROLL_WIDTH
    bsz, seq, _ = proj3.shape
    qo, ko, vo = q_off // HEAD_DIM, k_off // HEAD_DIM, v_off // HEAD_DIM
    ratio = qb // ATTN_PREV
    cur = lambda off: pl.BlockSpec((1, qb, HEAD_DIM), lambda b, h, n: (b, n, off + h))
    prev = lambda off: pl.BlockSpec((1, ATTN_PREV, HEAD_DIM),
                                    lambda b, h, n: (b, jnp.maximum(n * ratio - 1, 0), off + h))
    gain = pl.BlockSpec((1, HEAD_DIM), lambda b, h, n: (0, 0))
    return pl.pallas_call(
        functools.partial(_attn_kernel, qb=qb, sub=sub, span=span),
        grid=(bsz, n_heads, seq // qb),
        in_specs=[cur(qo), prev(ko), cur(ko), prev(vo), cur(vo),
                  pl.BlockSpec((1, 1, ATTN_ROLL_WIDTH), lambda b, h, n: (h, 0, 0)), gain, gain],
        out_specs=pl.BlockSpec((1, qb, HEAD_DIM), lambda b, h, n: (b, n, h)),
        out_shape=jax.ShapeDtypeStruct((bsz, seq, n_heads * HEAD_DIM), BF16),
        scratch_shapes=[pltpu.VMEM((sub, span), F32)],
        compiler_params=_params("parallel", "parallel", "arbitrary"),
        name="band_attention",
    )(proj3, proj3, proj3, proj3, proj3, _bias_vector(rel_bias, sub),
      q_gain.reshape(1, HEAD_DIM).astype(F32), k_gain.reshape(1, HEAD_DIM).astype(F32))


def _s5_tables(a_re, a_im, log_dt, b_re, b_im, c_re, c_im, d_skip, t_chunk):
    hi = lax.Precision.HIGHEST
    g, p = a_re.shape
    c = b_re.shape[-1]
    gl = LANES // c
    nj = g // gl
    dt = jnp.exp(log_dt.astype(F32))[:, None]
    ar, ai = a_re.astype(F32), a_im.astype(F32)

    def apow(tau):
        tau = jnp.asarray(tau, F32)[:, None, None]
        mag = jnp.exp(tau * dt * ar)
        return mag * jnp.cos(tau * dt * ai), mag * jnp.sin(tau * dt * ai)

    pr, pi = apow(jnp.arange(t_chunk + 1))
    abar_re, abar_im = pr[1], pi[1]
    den = ar * ar + ai * ai
    nr, ni = abar_re - 1.0, abar_im
    coef_re = (nr * ar + ni * ai) / den
    coef_im = (ni * ar - nr * ai) / den
    br, bi = b_re.astype(F32), b_im.astype(F32)
    bbar_re = coef_re[..., None] * br - coef_im[..., None] * bi
    bbar_im = coef_re[..., None] * bi + coef_im[..., None] * br
    cr, ci = c_re.astype(F32), c_im.astype(F32)
    ca_re = cr[None] * pr[:, :, None, :] - ci[None] * pi[:, :, None, :]
    ca_im = cr[None] * pi[:, :, None, :] + ci[None] * pr[:, :, None, :]
    lane = jnp.arange(LANES)
    col = jnp.arange(gl * p)
    rep_c = (lane[None, :] % c == jnp.arange(c)[:, None]).astype(BF16)
    rep_p = (col[None, :] % p == jnp.arange(p)[:, None]).astype(BF16)
    lane_lane = lane[:, None] // c == lane[None, :] // c
    lane_col = lane[:, None] // c == col[None, :] // p
    col_lane = col[:, None] // p == lane[None, :] // c

    k_tap = (jnp.einsum("tgcp,gpd->tgcd", ca_re[:t_chunk], bbar_re, precision=hi)
             - jnp.einsum("tgcp,gpd->tgcd", ca_im[:t_chunk], bbar_im, precision=hi))
    lag = jnp.arange(t_chunk)[None, :] - jnp.arange(t_chunk)[:, None]
    k_st = jnp.where((lag >= 0)[:, :, None, None, None], k_tap[jnp.maximum(lag, 0)], 0.0)
    k_rows = jnp.transpose(k_st.reshape(t_chunk, t_chunk, nj, gl, c, c), (2, 0, 1, 3, 5, 4))
    k_rep = jnp.dot(k_rows.reshape(-1, c).astype(BF16), rep_c).reshape(nj, t_chunk, t_chunk, LANES, LANES)
    k_rep = jnp.where(lane_lane[None, None, None], k_rep, 0)
    k_mat = jnp.transpose(k_rep, (0, 1, 3, 2, 4)).reshape(nj, t_chunk * LANES, t_chunk * LANES)

    rev = t_chunk - 1 - jnp.arange(t_chunk)
    s_re = pr[rev][..., None] * bbar_re[None] - pi[rev][..., None] * bbar_im[None]
    s_im = pr[rev][..., None] * bbar_im[None] + pi[rev][..., None] * bbar_re[None]

    def inc_mat(x):
        rows = jnp.transpose(x.reshape(t_chunk, nj, gl, p, c), (1, 0, 2, 4, 3))
        rep = jnp.dot(rows.reshape(-1, p).astype(BF16), rep_p).reshape(nj, t_chunk, LANES, gl * p)
        return jnp.where(lane_col[None, None], rep, 0).reshape(nj, t_chunk * LANES, gl * p)

    w_in = jnp.concatenate([k_mat, inc_mat(s_re), inc_mat(s_im)], axis=-1)

    def out_mat(x):
        rows = jnp.transpose(x.reshape(t_chunk, nj, gl, c, p), (1, 2, 4, 0, 3))
        rep = jnp.dot(rows.reshape(-1, c).astype(BF16), rep_c).reshape(nj, gl * p, t_chunk, LANES)
        return jnp.where(col_lane[None, :, None, :], rep, 0).reshape(nj, gl * p, t_chunk * LANES)

    w_state = jnp.concatenate([out_mat(ca_re[1:]), out_mat(-ca_im[1:])], axis=1)

    qr, qi = apow(t_chunk * jnp.arange(1, 9))
    lanes = lambda x: jnp.transpose(x.reshape(8, nj, gl * p), (1, 0, 2))
    consts = jnp.concatenate([lanes(qr), lanes(qi)], axis=1)
    d_row = jnp.tile(d_skip.astype(F32).reshape(nj, 1, LANES), (1, 1, t_chunk))
    return w_in, w_state, consts, d_row


def _gelu_tanh(y):
    return 0.5 * y * (1.0 + jnp.tanh(math.sqrt(2.0 / math.pi) * (y + 0.044715 * (y * y * y))))


def _s5_kernel(x_ref, win_ref, wst_ref, cst_ref, d_ref, y_ref, stage, inc_re, inc_im, hp_re, hp_im,
               *, t_chunk, n_rows, tc, gp):
    stage[...] = x_ref[...].astype(F32)
    u = jnp.concatenate([stage[pl.ds(s, n_rows, stride=t_chunk), :] for s in range(t_chunk)], axis=1)
    z = _dot(u.astype(BF16), win_ref[0])
    inc_re[...] = z[:, tc:tc + gp]
    inc_im[...] = z[:, tc + gp:]
    cst = cst_ref[0]
    c_re, c_im = cst[0:8, :], cst[8:16, :]
    row = lax.broadcasted_iota(jnp.int32, (8, gp), 0)

    def body(i, carry):
        h_re, h_im = carry
        r0 = pl.multiple_of(i * 8, 8)
        x_re = inc_re[pl.ds(r0, 8), :]
        x_im = inc_im[pl.ds(r0, 8), :]
        for k in (1, 2, 4):
            keep = row >= k
            s_re = jnp.where(keep, pltpu.roll(x_re, k, 0), 0.0)
            s_im = jnp.where(keep, pltpu.roll(x_im, k, 0), 0.0)
            m_re, m_im = c_re[k - 1:k, :], c_im[k - 1:k, :]
            x_re, x_im = x_re + m_re * s_re - m_im * s_im, x_im + m_re * s_im + m_im * s_re
        o_re = x_re + c_re * h_re - c_im * h_im
        o_im = x_im + c_re * h_im + c_im * h_re
        first = row >= 1
        hp_re[pl.ds(r0, 8), :] = jnp.where(first, pltpu.roll(o_re, 1, 0), h_re)
        hp_im[pl.ds(r0, 8), :] = jnp.where(first, pltpu.roll(o_im, 1, 0), h_im)
        return o_re[7:8, :], o_im[7:8, :]

    zero = jnp.zeros((1, gp), F32)
    lax.fori_loop(0, n_rows // 8, body, (zero, zero))
    h_prev = jnp.concatenate([hp_re[...], hp_im[...]], axis=1).astype(BF16)
    y = _gelu_tanh(z[:, :tc] + _dot(h_prev, wst_ref[0]) + d_ref[0] * u)
    for t in range(t_chunk):
        stage[pl.ds(t, n_rows, stride=t_chunk), :] = y[:, t * LANES:(t + 1) * LANES]
    y_ref[...] = stage[...].astype(y_ref.dtype)


def _s5(proj, ssm_off, ssm_w, bsz, tables, *, t_chunk=S5_CHUNK):
    w_in, w_state, consts, d_row = tables
    m = proj.shape[0]
    seq = m // bsz
    nj = ssm_w // LANES
    n_rows = seq // t_chunk
    tc = t_chunk * LANES
    gp = w_state.shape[1] // 2
    jb0 = ssm_off // LANES
    return pl.pallas_call(
        functools.partial(_s5_kernel, t_chunk=t_chunk, n_rows=n_rows, tc=tc, gp=gp),
        grid=(nj, bsz),
        in_specs=[
            pl.BlockSpec((seq, LANES), lambda j, b: (b, jb0 + j)),
            pl.BlockSpec((1, tc, tc + 2 * gp), lambda j, b: (j, 0, 0)),
            pl.BlockSpec((1, 2 * gp, tc), lambda j, b: (j, 0, 0)),
            pl.BlockSpec((1, 16, gp), lambda j, b: (j, 0, 0)),
            pl.BlockSpec((1, 1, tc), lambda j, b: (j, 0, 0)),
        ],
        out_specs=pl.BlockSpec((seq, LANES), lambda j, b: (b, j)),
        out_shape=jax.ShapeDtypeStruct((m, ssm_w), BF16),
        scratch_shapes=[pltpu.VMEM((seq, LANES), F32)] + [pltpu.VMEM((n_rows, gp), F32) for _ in range(4)],
        compiler_params=_params("parallel", "parallel"),
        name="s5",
    )(proj, w_in, w_state, consts, d_row)


def _glu_kernel(x_ref, w_ref, b_ref, o_ref, *, n):
    z = _dot(x_ref[...], w_ref[...]) + b_ref[...]
    o_ref[...] = (z[:, :n] * _sigmoid(z[:, n:])).astype(o_ref.dtype)


def _glu(x, w, layer, b, *, tm=1024):
    m, k = x.shape
    n2 = w.shape[2]
    n = n2 // 2
    return pl.pallas_call(
        functools.partial(_glu_kernel, n=n),
        grid=(m // tm,),
        in_specs=[
            pl.BlockSpec((tm, k), lambda i: (i, 0)),
            _layer_spec(layer, (k, n2), lambda i: (0, 0)),
            pl.BlockSpec((1, n2), lambda i: (0, 0)),
        ],
        out_specs=pl.BlockSpec((tm, n), lambda i: (i, 0)),
        out_shape=jax.ShapeDtypeStruct((m, n), BF16),
        compiler_params=_params("parallel"),
        name="glu",
    )(x, w, b.reshape(1, n2).astype(F32))


def _route_kernel(h_ref, g_ref, w_ref, b_ref, xp_ref, info_ref, *, n_groups, per_group):
    x = h_ref[...]
    xn = x * lax.rsqrt(jnp.mean(x * x, axis=-1, keepdims=True) + EPS) * g_ref[...]
    xp_ref[...] = _pack_halves(xn)
    n_exp = n_groups * per_group
    logits = _dot(xn.astype(BF16), w_ref[...]) + b_ref[...]
    lane = lax.broadcasted_iota(jnp.int32, logits.shape, 1).astype(F32)
    big = float(LANES)
    is_group = jnp.logical_and(lane >= n_exp, lane < n_exp + n_groups)
    gl = jnp.where(is_group, logits, -jnp.inf)
    gmax = jnp.max(gl, axis=-1, keepdims=True)
    gsel = jnp.min(jnp.where(gl == gmax, lane, big), axis=-1, keepdims=True) - n_exp
    gprob = 1.0 / jnp.sum(jnp.where(is_group, jnp.exp(logits - gmax), 0.0), axis=-1, keepdims=True)
    in_group = jnp.logical_and(lane >= gsel * per_group, lane < (gsel + 1.0) * per_group)
    el = jnp.where(in_group, logits, -jnp.inf)
    v1 = jnp.max(el, axis=-1, keepdims=True)
    i1 = jnp.min(jnp.where(el == v1, lane, big), axis=-1, keepdims=True)
    el2 = jnp.where(lane == i1, -jnp.inf, el)
    v2 = jnp.max(el2, axis=-1, keepdims=True)
    i2 = jnp.min(jnp.where(el2 == v2, lane, big), axis=-1, keepdims=True)
    e2 = jnp.exp(v2 - v1)
    w1 = gprob / (1.0 + e2)
    w2 = gprob * e2 / (1.0 + e2)
    info_ref[...] = (jnp.where(lane == 0.0, i1, 0.0) + jnp.where(lane == 1.0, i2, 0.0)
                     + jnp.where(lane == 2.0, w1, 0.0) + jnp.where(lane == 3.0, w2, 0.0))


def _route(h, gain, w_gr, b_gr, w_er, b_er, *, tm=512):
    m, d = h.shape
    n_groups, _, per_group = w_er.shape
    n_exp = n_groups * per_group
    assert n_exp + n_groups <= LANES
    w = jnp.concatenate([jnp.transpose(w_er, (1, 0, 2)).reshape(d, n_exp), w_gr], axis=1)
    w = jnp.pad(w, ((0, 0), (0, LANES - n_exp - n_groups))).astype(BF16)
    b = jnp.concatenate([b_er.reshape(n_exp), b_gr]).astype(F32)
    b = jnp.pad(b, (0, LANES - n_exp - n_groups)).reshape(1, LANES)
    return pl.pallas_call(
        functools.partial(_route_kernel, n_groups=n_groups, per_group=per_group),
        grid=(m // tm,),
        in_specs=[pl.BlockSpec((tm, d), lambda i: (i, 0)), pl.BlockSpec((1, d), lambda i: (0, 0)),
                  pl.BlockSpec((d, LANES), lambda i: (0, 0)), pl.BlockSpec((1, LANES), lambda i: (0, 0))],
        out_specs=[pl.BlockSpec((tm, d // 2), lambda i: (i, 0)), pl.BlockSpec((tm, LANES), lambda i: (i, 0))],
        out_shape=[jax.ShapeDtypeStruct((m, d // 2), U32), jax.ShapeDtypeStruct((m, LANES), F32)],
        compiler_params=_params("parallel"),
        name="moe_route",
    )(h, gain.reshape(1, d).astype(F32), w, b)


def _sorted_layout(expert_ids, n_exp, row_tile, n_tiles):
    e = expert_ids.reshape(-1)
    blk = LANES
    nb = e.shape[0] // blk
    onehot = (e[:, None] == jnp.arange(n_exp, dtype=jnp.int32)[None, :]).reshape(nb, blk, n_exp)
    tri = jnp.tril(jnp.ones((blk, blk), BF16))
    within = jnp.einsum("ij,bjk->bik", tri, onehot.astype(BF16), preferred_element_type=F32)
    totals = within[:, -1, :]
    before = jnp.dot(jnp.tril(jnp.ones((nb, nb), F32), -1), totals, precision=lax.Precision.HIGHEST)
    counts = (before[-1] + totals[-1]).astype(jnp.int32)
    rank = jnp.sum(jnp.where(onehot, within + before[:, None, :], 0.0), axis=-1).reshape(-1).astype(jnp.int32) - 1
    padded = ((counts + row_tile - 1) // row_tile) * row_tile
    ends = jnp.cumsum(padded)
    starts = ends - padded
    pos = starts[e] + rank
    tile_start = jnp.arange(n_tiles, dtype=jnp.int32) * row_tile
    tile_expert = jnp.minimum(jnp.sum((tile_start[:, None] >= ends[None, :]).astype(jnp.int32), axis=1), n_exp - 1)
    n_used = (ends[-1] // row_tile).astype(jnp.int32).reshape(1)
    return pos.astype(jnp.int32), tile_expert.astype(jnp.int32), n_used


def _row_copy(src_ref, src_row, dst_ref, dst_row, sem):
    return pltpu.make_async_copy(src_ref.at[pl.ds(src_row, 1)], dst_ref.at[pl.ds(dst_row, 1)], sem)


def _start_row_copies(n_rows, start_row):
    def body(t, c):
        start_row(t)
        return c

    lax.fori_loop(0, n_rows, body, 0, unroll=ROW_DMA_UNROLL)


def _wait_row_copies(n_copies, example_copy):
    for _ in range(n_copies):
        example_copy.wait()


def _dispatch_kernel(pos_ref, x_ref, init_ref, out_ref, sem, *, tt, n_slots):
    del init_ref

    def start_row(t):
        for k in range(n_slots):
            _row_copy(x_ref, t, out_ref, pos_ref[0, 0, n_slots * t + k], sem).start(priority=k % 2)

    _start_row_copies(tt, start_row)
    _wait_row_copies(tt * n_slots, _row_copy(x_ref, 0, out_ref, 0, sem))


def _dispatch(xp, pos, n_rows, *, tt=MOE_TOKEN_TILE, n_slots=TOP_K):
    m, d2 = xp.shape
    pos3 = pos.reshape(m // tt, 1, tt * n_slots)
    return pl.pallas_call(
        functools.partial(_dispatch_kernel, tt=tt, n_slots=n_slots),
        grid=(m // tt,),
        in_specs=[pl.BlockSpec((1, 1, tt * n_slots), lambda i: (i, 0, 0), memory_space=pltpu.SMEM),
                  pl.BlockSpec((tt, d2), lambda i: (i, 0)),
                  pl.BlockSpec(memory_space=pl.ANY)],
        out_specs=pl.BlockSpec(memory_space=pl.ANY),
        out_shape=jax.ShapeDtypeStruct((n_rows, d2), U32),
        scratch_shapes=[pltpu.SemaphoreType.DMA],
        input_output_aliases={2: 0},
        compiler_params=_params("arbitrary"),
        name="moe_dispatch",
    )(pos3, xp, jnp.zeros((n_rows, d2), U32))


def _expert_kernel(te_ref, first_ref, next_ref, nu_ref, x_ref, wg_hbm, wu_hbm, wd_hbm, o_ref,
                   stage_g, stage_u, stage_d, wg_s, wu_s, wd_s, sem, *, layer):
    t = pl.program_id(0)
    in_use = t < nu_ref[0]
    copies = ((wg_hbm, stage_g), (wu_hbm, stage_u), (wd_hbm, stage_d))

    def weight_copy(k, expert):
        src, dst = copies[k]
        return pltpu.make_async_copy(src.at[layer, expert], dst, sem.at[k])

    @pl.when(t == 0)
    def _():
        for k in range(len(copies)):
            weight_copy(k, te_ref[0]).start()

    @pl.when(jnp.logical_and(in_use, first_ref[t] == 1))
    def _():
        for k, dst in enumerate((wg_s, wu_s, wd_s)):
            weight_copy(k, 0).wait()
            dst[...] = copies[k][1][...].astype(BF16)

        @pl.when(next_ref[t] >= 0)
        def _():
            for k in range(len(copies)):
                weight_copy(k, next_ref[t]).start()

    @pl.when(in_use)
    def _():
        lo, hi = _unpack_halves(x_ref[...])
        x = jnp.concatenate([lo, hi], axis=1).astype(BF16)
        g = _dot(x, wg_s[...])
        u = _dot(x, wu_s[...])
        hid = (g * _sigmoid(g) * u).astype(BF16)
        o_ref[...] = _pack_halves(_dot(hid, wd_s[...]))

    @pl.when(jnp.logical_not(in_use))
    def _():
        o_ref[...] = jnp.zeros_like(o_ref)


def _expert_ffn(xs, tile_expert, n_used, w_gate, w_up, w_down, layer, *, tm=MOE_ROW_TILE):
    rows, d2 = xs.shape
    _, _, d, ff = w_gate.shape
    n_tiles = rows // tm
    idx = jnp.arange(n_tiles, dtype=jnp.int32)
    used = idx < n_used[0]
    first = jnp.logical_and(used, jnp.logical_or(idx == 0, tile_expert != jnp.roll(tile_expert, 1)))
    first_pos = jnp.where(first, idx, n_tiles)
    later = jnp.concatenate([lax.cummin(first_pos[::-1])[::-1][1:], jnp.full((1,), n_tiles, jnp.int32)])
    next_expert = jnp.where(later < n_tiles, tile_expert[jnp.minimum(later, n_tiles - 1)], -1).astype(jnp.int32)
    any_spec = pl.BlockSpec(memory_space=pl.ANY)
    grid_spec = pltpu.PrefetchScalarGridSpec(
        num_scalar_prefetch=4,
        grid=(n_tiles,),
        in_specs=[pl.BlockSpec((tm, d2), lambda t, *_: (t, 0)), any_spec, any_spec, any_spec],
        out_specs=pl.BlockSpec((tm, d2), lambda t, *_: (t, 0)),
        scratch_shapes=[pltpu.VMEM((d, ff), F32), pltpu.VMEM((d, ff), F32), pltpu.VMEM((ff, d), F32),
                        pltpu.VMEM((d, ff), BF16), pltpu.VMEM((d, ff), BF16), pltpu.VMEM((ff, d), BF16),
                        pltpu.SemaphoreType.DMA((3,))],
    )
    return pl.pallas_call(
        functools.partial(_expert_kernel, layer=layer),
        grid_spec=grid_spec,
        out_shape=jax.ShapeDtypeStruct((rows, d2), U32),
        compiler_params=_params("arbitrary"),
        name="moe_expert_ffn",
    )(tile_expert, first.astype(jnp.int32), next_expert, n_used, xs, w_gate, w_up, w_down)


def _combine_kernel(pos_ref, pos_next_ref, info_ref, h_ref, g_ref, y_ref, o_ref, on_ref, buf, sem, *, tt, n_slots):
    i = pl.program_id(0)
    cur = i % 2

    def gather(p_ref, half):
        def start_row(t):
            for k in range(n_slots):
                _row_copy(y_ref, p_ref[0, 0, n_slots * t + k], buf.at[half, k], t, sem.at[half]).start(priority=k % 2)

        _start_row_copies(tt, start_row)

    @pl.when(i == 0)
    def _():
        gather(pos_ref, 0)

    @pl.when(i + 1 < pl.num_programs(0))
    def _():
        gather(pos_next_ref, 1 - cur)

    _wait_row_copies(tt * n_slots, _row_copy(y_ref, 0, buf.at[cur, 0], 0, sem.at[cur]))
    info = info_ref[...]
    lane = lax.broadcasted_iota(jnp.int32, info.shape, 1)
    lo_acc = hi_acc = None
    for k in range(n_slots):
        wk = jnp.sum(jnp.where(lane == n_slots + k, info, 0.0), axis=-1, keepdims=True)
        lo, hi = _unpack_halves(buf[cur, k])
        lo_acc = wk * lo if lo_acc is None else lo_acc + wk * lo
        hi_acc = wk * hi if hi_acc is None else hi_acc + wk * hi
    out = h_ref[...] + jnp.concatenate([lo_acc, hi_acc], axis=1)
    o_ref[...] = out
    ms = jnp.mean(out * out, axis=-1, keepdims=True)
    on_ref[...] = (out * lax.rsqrt(ms + EPS) * g_ref[...]).astype(on_ref.dtype)


def _combine(h, info, ys, pos, next_gain, *, tt=MOE_TOKEN_TILE, n_slots=TOP_K):
    m, d = h.shape
    d2 = ys.shape[1]
    n_steps = m // tt
    pos3 = pos.reshape(n_steps, 1, tt * n_slots)
    return pl.pallas_call(
        functools.partial(_combine_kernel, tt=tt, n_slots=n_slots),
        grid=(n_steps,),
        in_specs=[pl.BlockSpec((1, 1, tt * n_slots), lambda i: (i, 0, 0), memory_space=pltpu.SMEM),
                  pl.BlockSpec((1, 1, tt * n_slots), lambda i: (jnp.minimum(i + 1, n_steps - 1), 0, 0),
                               memory_space=pltpu.SMEM),
                  pl.BlockSpec((tt, LANES), lambda i: (i, 0)),
                  pl.BlockSpec((tt, d), lambda i: (i, 0)),
                  pl.BlockSpec((1, d), lambda i: (0, 0)),
                  pl.BlockSpec(memory_space=pl.ANY)],
        out_specs=[pl.BlockSpec((tt, d), lambda i: (i, 0)), pl.BlockSpec((tt, d), lambda i: (i, 0))],
        out_shape=[jax.ShapeDtypeStruct((m, d), F32), jax.ShapeDtypeStruct((m, d), BF16)],
        scratch_shapes=[pltpu.VMEM((2, n_slots, tt, d2), U32), pltpu.SemaphoreType.DMA((2,))],
        input_output_aliases={3: 0},
        compiler_params=_params("arbitrary"),
        name="moe_combine",
    )(pos3, pos3, info, h, next_gain.reshape(1, d).astype(F32), ys)


def _moe(h, gain, w_gr, b_gr, w_er, b_er, w_gate, w_up, w_down, layer, next_gain):
    m = h.shape[0]
    n_exp = w_gate.shape[1]
    n_tiles = -(-(m * TOP_K + n_exp * (MOE_ROW_TILE - 1)) // MOE_ROW_TILE)
    xp, info = _route(h, gain, w_gr, b_gr, w_er, b_er)
    expert_ids = info[:, :TOP_K].astype(jnp.int32)
    pos, tile_expert, n_used = _sorted_layout(expert_ids, n_exp, MOE_ROW_TILE, n_tiles)
    xs = _dispatch(xp, pos, n_tiles * MOE_ROW_TILE)
    ys = _expert_ffn(xs, tile_expert, n_used, w_gate, w_up, w_down, layer)
    return _combine(h, info, ys, pos, next_gain)


def kernel(x, p, mix_gain, w_in, conv_w, conv_b, w_rgate, b_rgate, w_igate, b_igate, lru_lambda, q_gain, k_gain, rel_bias, ssm_a_re, ssm_a_im, ssm_log_dt, ssm_b_re, ssm_b_im, ssm_c_re, ssm_c_im, ssm_d, w_glu, b_glu, w_proj_lru, w_proj_att, w_proj_ssm, w_out, ffn_gain, w_group_router, b_group_router, w_expert_router, b_expert_router, w_up, w_gate, w_down, ple_gain, w_ple, w_ple_gate):
    bsz, seq, d = x.shape
    depth = w_in.shape[0]
    m = bsz * seq
    lru_w = w_proj_lru.shape[1]
    att_w = w_proj_att.shape[1]
    ssm_w = w_proj_ssm.shape[1]
    n_heads = att_w // HEAD_DIM
    q_off, k_off, v_off = lru_w, lru_w + att_w, lru_w + 2 * att_w
    ssm_off = lru_w + 3 * att_w
    gate_off = ssm_off + ssm_w
    in_width = w_in.shape[2]

    bf = lambda w: w.astype(BF16)
    w_glu, w_proj_lru, w_proj_att, w_proj_ssm, w_out = map(bf, (w_glu, w_proj_lru, w_proj_att, w_proj_ssm, w_out))
    w_ple_gate, w_ple = bf(w_ple_gate), bf(w_ple)
    p = bf(p.reshape(depth, m, p.shape[-1]))

    h = x.reshape(m, d).astype(F32)
    for i in range(depth):
        xn = _rmsnorm(h, mix_gain[i])
        proj = _matmul(xn, w_in, i)
        proj3 = proj.reshape(bsz, seq, in_width)
        y_lru = _rglru(proj3, conv_w[i], conv_b[i], w_rgate[i], b_rgate[i], w_igate[i], b_igate[i], lru_lambda[i])
        y_att = _attention(proj3, q_off, k_off, v_off, n_heads, q_gain[i], k_gain[i], rel_bias[i])
        tables = _s5_tables(ssm_a_re[i], ssm_a_im[i], ssm_log_dt[i], ssm_b_re[i], ssm_b_im[i],
                            ssm_c_re[i], ssm_c_im[i], ssm_d[i], S5_CHUNK)
        y_ssm = _glu(_s5(proj, ssm_off, ssm_w, bsz, tables), w_glu, i, b_glu[i])
        merged = _gated_merge(y_lru.reshape(m, lru_w), y_att.reshape(m, att_w), y_ssm,
                              w_proj_lru, w_proj_att, w_proj_ssm, i, proj, gate_off)
        h = _matmul_residual(h, merged, w_out, i, in_place=i > 0)
        h, hn = _moe(h, ffn_gain[i], w_group_router[i], b_group_router[i], w_expert_router[i], b_expert_router[i],
                     w_gate, w_up, w_down, i, ple_gain[i])
        h = _ple(h, hn, w_ple_gate, p, w_ple, i)
    return h.reshape(bsz, seq, d)
```

```python
import functools
import math

import jax
import jax.numpy as jnp
from jax import lax
from jax.experimental import pallas as pl
from jax.experimental.pallas import tpu as pltpu

F32 = jnp.float32
BF16 = jnp.bfloat16
U32 = jnp.uint32

CHUNK = 64
CHUNK_SHIFT = CHUNK.bit_length() - 1
assert 1 << CHUNK_SHIFT == CHUNK
LEFT_CHUNKS = 8
REL_CLIP = 128
HEAD_DIM = 128
LRU_C = 8.0
TOP_K = 2
EPS = 1e-6
NEG_INF = -1e30

LANES = 128
S5_CHUNK = 8
ATTN_PREV = LEFT_CHUNKS * CHUNK
ATTN_QBLOCK = 4 * ATTN_PREV
ATTN_SUB = 128
ATTN_ROLL_WIDTH = 1024
MOE_ROW_TILE = 256
MOE_TOKEN_TILE = 256
ROW_DMA_UNROLL = 8
V7X_VMEM_LIMIT_BYTES = 56 * 1024 * 1024
HI16 = 0xFFFF0000


def _params(*semantics):
    return pltpu.CompilerParams(dimension_semantics=semantics, vmem_limit_bytes=V7X_VMEM_LIMIT_BYTES)


def _dot(a, b):
    return jnp.dot(a, b, preferred_element_type=F32)


def _sigmoid(x):
    return 0.5 * jnp.tanh(0.5 * x) + 0.5


def _pack_halves(x):
    n = x.shape[1] // 2
    lo = pltpu.bitcast(x[:, :n].astype(BF16).astype(F32), U32)
    hi = pltpu.bitcast(x[:, n:].astype(BF16).astype(F32), U32)
    return (lo >> 16) | (hi & jnp.uint32(HI16))


def _unpack_halves(u):
    return pltpu.bitcast(u << 16, F32), pltpu.bitcast(u & jnp.uint32(HI16), F32)


def _rmsnorm_kernel(x_ref, g_ref, o_ref):
    x = x_ref[...].astype(F32)
    ms = jnp.mean(x * x, axis=-1, keepdims=True)
    o_ref[...] = (x * lax.rsqrt(ms + EPS) * g_ref[...]).astype(o_ref.dtype)


def _rmsnorm(x, gain, *, tm=512):
    m, d = x.shape
    return pl.pallas_call(
        _rmsnorm_kernel,
        grid=(m // tm,),
        in_specs=[pl.BlockSpec((tm, d), lambda i: (i, 0)), pl.BlockSpec((1, d), lambda i: (0, 0))],
        out_specs=pl.BlockSpec((tm, d), lambda i: (i, 0)),
        out_shape=jax.ShapeDtypeStruct((m, d), BF16),
        compiler_params=_params("parallel"),
        name="rmsnorm",
    )(x, gain.reshape(1, d).astype(F32))


def _mm_kernel(x_ref, w_hbm, o_ref, stage, w_s, sem, *, layer, tn):
    j, i = pl.program_id(0), pl.program_id(1)

    def weight_copy(col_tile):
        return pltpu.make_async_copy(w_hbm.at[layer, :, pl.ds(col_tile * tn, tn)], stage, sem)

    @pl.when(jnp.logical_and(j == 0, i == 0))
    def _():
        weight_copy(0).start()

    @pl.when(i == 0)
    def _():
        weight_copy(0).wait()
        w_s[...] = stage[...].astype(BF16)

        @pl.when(j + 1 < pl.num_programs(0))
        def _():
            weight_copy(j + 1).start()

    o_ref[...] = _dot(x_ref[...], w_s[...]).astype(o_ref.dtype)


def _layer_spec(layer, block, index_map):
    return pl.BlockSpec((None,) + tuple(block), lambda *a: (layer,) + tuple(index_map(*a)))


def _matmul(x, w, layer, *, tm=1024, tn=1024, out_dtype=BF16):
    m, k = x.shape
    n = w.shape[2]
    return pl.pallas_call(
        functools.partial(_mm_kernel, layer=layer, tn=tn),
        grid=(n // tn, m // tm),
        in_specs=[pl.BlockSpec((tm, k), lambda j, i: (i, 0)), pl.BlockSpec(memory_space=pl.ANY)],
        out_specs=pl.BlockSpec((tm, tn), lambda j, i: (i, j)),
        out_shape=jax.ShapeDtypeStruct((m, n), out_dtype),
        scratch_shapes=[pltpu.VMEM((k, tn), F32), pltpu.VMEM((k, tn), BF16), pltpu.SemaphoreType.DMA],
        compiler_params=_params("arbitrary", "arbitrary"),
        name="matmul",
    )(x, w)


def _mm_residual_kernel(h_ref, x_ref, w_ref, o_ref):
    o_ref[...] = h_ref[...] + _dot(x_ref[...], w_ref[...])


def _matmul_residual(h, x, w, layer, *, in_place, tm=1024, tn=1024):
    m, k = x.shape
    n = w.shape[2]
    return pl.pallas_call(
        _mm_residual_kernel,
        grid=(m // tm, n // tn),
        in_specs=[
            pl.BlockSpec((tm, tn), lambda i, j: (i, j)),
            pl.BlockSpec((tm, k), lambda i, j: (i, 0)),
            _layer_spec(layer, (k, tn), lambda i, j: (0, j)),
        ],
        out_specs=pl.BlockSpec((tm, tn), lambda i, j: (i, j)),
        out_shape=jax.ShapeDtypeStruct((m, n), F32),
        input_output_aliases={0: 0} if in_place else {},
        compiler_params=_params("parallel", "arbitrary"),
        name="matmul_residual",
    )(h, x, w)


def _ple_kernel(h_ref, x_ref, wg_ref, p_ref, we_ref, o_ref):
    gate = _sigmoid(_dot(x_ref[...], wg_ref[...]))
    emb = _dot(p_ref[...], we_ref[...])
    o_ref[...] = h_ref[...] + gate * emb


def _ple(h, xn, w_gate, p, w_ple, layer, *, tm=1024, tn=512):
    m, k = xn.shape
    n = w_gate.shape[2]
    kp = p.shape[2]
    def outer(h_hbm, x_hbm, wg_hbm, p_hbm, we_hbm, o_hbm):
        pltpu.emit_pipeline(
            _ple_kernel,
            grid=(m // tm, n // tn),
            in_specs=[
                pl.BlockSpec((tm, tn), lambda i, j: (i, j)),
                pl.BlockSpec((tm, k), lambda i, j: (i, 0)),
                pl.BlockSpec((k, tn), lambda i, j: (0, j), pipeline_mode=pl.Buffered(3)),
                pl.BlockSpec((tm, kp), lambda i, j: (i, 0)),
                pl.BlockSpec((kp, tn), lambda i, j: (0, j)),
            ],
            out_specs=[pl.BlockSpec((tm, tn), lambda i, j: (i, j))],
        )(h_hbm, x_hbm, wg_hbm.at[layer], p_hbm.at[layer], we_hbm.at[layer], o_hbm)

    any_spec = pl.BlockSpec(memory_space=pl.ANY)
    return pl.pallas_call(
        outer,
        in_specs=[any_spec] * 5,
        out_specs=any_spec,
        out_shape=jax.ShapeDtypeStruct((m, n), F32),
        input_output_aliases={0: 0},
        compiler_params=pltpu.CompilerParams(vmem_limit_bytes=V7X_VMEM_LIMIT_BYTES),
        name="ple",
    )(h, xn, w_gate, p, w_ple)


def _merge_kernel(yl_ref, ya_ref, ys_ref, pl_ref, pa_ref, ps_ref, gl_ref, ga_ref, gs_ref, o_ref):
    acc = _sigmoid(gl_ref[...].astype(F32)) * _dot(yl_ref[...], pl_ref[...])
    acc += _sigmoid(ga_ref[...].astype(F32)) * _dot(ya_ref[...], pa_ref[...])
    acc += _sigmoid(gs_ref[...].astype(F32)) * _dot(ys_ref[...], ps_ref[...])
    o_ref[...] = acc.astype(o_ref.dtype)


def _gated_merge(y_lru, y_att, y_ssm, p_lru, p_att, p_ssm, layer, proj, gate_off, *, tm=1024, tn=512):
    m = y_lru.shape[0]
    d = p_lru.shape[2]
    goff = gate_off // tn
    nd = d // tn

    def y_spec(y):
        return pl.BlockSpec((tm, y.shape[1]), lambda i, j: (i, 0))

    def p_spec(p):
        return _layer_spec(layer, (p.shape[1], tn), lambda i, j: (0, j))

    def g_spec(b):
        return pl.BlockSpec((tm, tn), lambda i, j: (i, goff + b * nd + j))

    return pl.pallas_call(
        _merge_kernel,
        grid=(m // tm, nd),
        in_specs=[y_spec(y_lru), y_spec(y_att), y_spec(y_ssm), p_spec(p_lru), p_spec(p_att), p_spec(p_ssm),
                  g_spec(0), g_spec(1), g_spec(2)],
        out_specs=pl.BlockSpec((tm, tn), lambda i, j: (i, j)),
        out_shape=jax.ShapeDtypeStruct((m, d), BF16),
        compiler_params=_params("parallel", "arbitrary"),
        name="gated_merge",
    )(y_lru, y_att, y_ssm, p_lru, p_att, p_ssm, proj, proj, proj)


def _scan_rows8(a8, x8, hprev, row):
    for k in (1, 2, 4):
        keep = row >= k
        a_sh = jnp.where(keep, pltpu.roll(a8, k, 0), 1.0)
        x_sh = jnp.where(keep, pltpu.roll(x8, k, 0), 0.0)
        x8 = a8 * x_sh + x8
        a8 = a8 * a_sh
    return a8 * hprev + x8


def _lru_kernel(u_ref, cw_ref, cb_ref, wr_ref, br_ref, wi_ref, bi_ref, lam_ref, y_ref,
                tail_ref, h_ref, a_s, x_s, *, t_rows, n_blocks, block, conv_width):
    @pl.when(pl.program_id(1) == 0)
    def _():
        tail_ref[...] = jnp.zeros_like(tail_ref)
        h_ref[...] = jnp.zeros_like(h_ref)

    u = u_ref[0].astype(F32)
    ue = jnp.concatenate([tail_ref[...], u], axis=0)
    xc = cb_ref[...]
    for j in range(conv_width):
        off = 8 - (conv_width - 1) + j
        xc = xc + cw_ref[j:j + 1, :] * ue[off:off + t_rows, :]
    tail_ref[...] = u[t_rows - 8:, :]

    xcb = xc.astype(BF16)
    rs, igs = [], []
    for hb in range(n_blocks):
        xb = xcb[:, hb * block:(hb + 1) * block]
        rs.append(_dot(xb, wr_ref[hb]))
        igs.append(_dot(xb, wi_ref[hb]))
    r = _sigmoid(jnp.concatenate(rs, axis=1) + br_ref[...])
    ig = _sigmoid(jnp.concatenate(igs, axis=1) + bi_ref[...])
    lam = lam_ref[...]
    softplus_neg_lam = jnp.maximum(-lam, 0.0) + jnp.log1p(jnp.exp(-jnp.abs(lam)))
    log_a = (-LRU_C) * r * softplus_neg_lam
    a = jnp.exp(log_a)
    a_s[...] = a
    x_s[...] = jnp.sqrt(-jnp.tanh(log_a) * (a * a + 1.0)) * (ig * xc)

    w = a_s.shape[1]
    row = lax.broadcasted_iota(jnp.int32, (8, w), 0)

    def body(i, hprev):
        r0 = pl.multiple_of(i * 16, 16)
        h_a = _scan_rows8(a_s[pl.ds(r0, 8), :], x_s[pl.ds(r0, 8), :], hprev, row)
        h_b = _scan_rows8(a_s[pl.ds(r0 + 8, 8), :], x_s[pl.ds(r0 + 8, 8), :], h_a[7:8, :], row)
        y_ref[0, pl.ds(r0, 16), :] = jnp.concatenate([h_a, h_b], axis=0).astype(y_ref.dtype)
        return h_b[7:8, :]

    h_ref[...] = lax.fori_loop(0, t_rows // 16, body, h_ref[...])


def _rglru(proj3, conv_w, conv_b, w_rg, b_rg, w_ig, b_ig, lam, *, t_rows=256):
    bsz, seq, _ = proj3.shape
    n_blocks, block, _ = w_rg.shape
    w = n_blocks * block
    cw = conv_w.shape[0]
    vec = lambda v: v.reshape(1, w).astype(F32)
    full2 = lambda r, c: pl.BlockSpec((r, c), lambda b, t: (0, 0))
    full3 = pl.BlockSpec((n_blocks, block, block), lambda b, t: (0, 0, 0))
    return pl.pallas_call(
        functools.partial(_lru_kernel, t_rows=t_rows, n_blocks=n_blocks, block=block, conv_width=cw),
        grid=(bsz, seq // t_rows),
        in_specs=[pl.BlockSpec((1, t_rows, w), lambda b, t: (b, t, 0)), full2(cw, w), full2(1, w),
                  full3, full2(1, w), full3, full2(1, w), full2(1, w)],
        out_specs=pl.BlockSpec((1, t_rows, w), lambda b, t: (b, t, 0)),
        out_shape=jax.ShapeDtypeStruct((bsz, seq, w), BF16),
        scratch_shapes=[pltpu.VMEM((8, w), F32), pltpu.VMEM((1, w), F32),
                        pltpu.VMEM((t_rows, w), F32), pltpu.VMEM((t_rows, w), F32)],
        compiler_params=_params("parallel", "arbitrary"),
        name="rglru",
    )(proj3, conv_w.astype(F32), vec(conv_b), w_rg.astype(BF16), vec(b_rg), w_ig.astype(BF16), vec(b_ig),
      vec(lam))


def _head_rms(x, gain):
    x = x.astype(F32)
    return x * lax.rsqrt(jnp.mean(x * x, axis=-1, keepdims=True) + EPS) * gain


def _attn_kernel(q_ref, kp_ref, kc_ref, vp_ref, vc_ref, bvec_ref, qg_ref, kg_ref, o_ref, bias_s, *, qb, sub, span):
    first = pl.program_id(2) == 0

    @pl.when(first)
    def _():
        vec = jnp.broadcast_to(bvec_ref[0], (sub, ATTN_ROLL_WIDTH))
        table = pltpu.roll(vec, 0, 1, stride=1, stride_axis=0)[:, :span]
        q_chunk = (lax.broadcasted_iota(jnp.int32, table.shape, 0) >> CHUNK_SHIFT) + LEFT_CHUNKS
        k_chunk = lax.broadcasted_iota(jnp.int32, table.shape, 1) >> CHUNK_SHIFT
        in_band = jnp.logical_and(k_chunk >= q_chunk - LEFT_CHUNKS, k_chunk <= q_chunk)
        bias_s[...] = jnp.where(in_band, table, NEG_INF)

    qn = (_head_rms(q_ref[0], qg_ref[...]) * (HEAD_DIM ** -0.5)).astype(BF16)
    kn = _head_rms(jnp.concatenate([kp_ref[0], kc_ref[0]], axis=0), kg_ref[...]).astype(BF16)
    v = jnp.concatenate([vp_ref[0], vc_ref[0]], axis=0)
    bias = bias_s[...]
    outs = []
    for part in range(qb // sub):
        c0 = part * sub
        s = lax.dot_general(qn[c0:c0 + sub], kn[c0:c0 + span], (((1,), (1,)), ((), ())),
                            preferred_element_type=F32)
        s = s + bias
        col = c0 + lax.broadcasted_iota(jnp.int32, s.shape, 1)
        s = jnp.where(jnp.logical_and(first, col < ATTN_PREV), NEG_INF, s)
        p = jnp.exp(s - jnp.max(s, axis=-1, keepdims=True))
        denom = jnp.sum(p, axis=-1, keepdims=True)
        outs.append(_dot(p.astype(BF16), v[c0:c0 + span]) / denom)
    o_ref[0] = jnp.concatenate(outs, axis=0).astype(o_ref.dtype)


def _bias_vector(rel_bias, sub):
    mm = jnp.arange(ATTN_ROLL_WIDTH)
    offset = jnp.where(mm < ATTN_ROLL_WIDTH - sub, mm, mm - ATTN_ROLL_WIDTH)
    dist = ATTN_PREV - offset
    return rel_bias.astype(F32)[:, None, jnp.clip(dist, -REL_CLIP, REL_CLIP) + REL_CLIP]


def _attention(proj3, q_off, k_off, v_off, n_heads, q_gain, k_gain, rel_bias, *, qb=ATTN_QBLOCK, sub=ATTN_SUB):
    span = sub + ATTN_PREV
    assert qb % ATTN_PREV == 0 and qb % sub == 0 and sub % CHUNK == 0 and sub % LANES == 0
    assert span + sub <= ATTN_ROLL_WIDTH
    bsz, seq, _ = proj3.shape
    qo, ko, vo = q_off // HEAD_DIM, k_off // HEAD_DIM, v_off // HEAD_DIM
    ratio = qb // ATTN_PREV
    cur = lambda off: pl.BlockSpec((1, qb, HEAD_DIM), lambda b, h, n: (b, n, off + h))
    prev = lambda off: pl.BlockSpec((1, ATTN_PREV, HEAD_DIM),
                                    lambda b, h, n: (b, jnp.maximum(n * ratio - 1, 0), off + h))
    gain = pl.BlockSpec((1, HEAD_DIM), lambda b, h, n: (0, 0))
    return pl.pallas_call(
        functools.partial(_attn_kernel, qb=qb, sub=sub, span=span),
        grid=(bsz, n_heads, seq // qb),
        in_specs=[cur(qo), prev(ko), cur(ko), prev(vo), cur(vo),
                  pl.BlockSpec((1, 1, ATTN_ROLL_WIDTH), lambda b, h, n: (h, 0, 0)), gain, gain],
        out_specs=pl.BlockSpec((1, qb, HEAD_DIM), lambda b, h, n: (b, n, h)),
        out_shape=jax.ShapeDtypeStruct((bsz, seq, n_heads * HEAD_DIM), BF16),
        scratch_shapes=[pltpu.VMEM((sub, span), F32)],
        compiler_params=_params("parallel", "parallel", "arbitrary"),
        name="band_attention",
    )(proj3, proj3, proj3, proj3, proj3, _bias_vector(rel_bias, sub),
      q_gain.reshape(1, HEAD_DIM).astype(F32), k_gain.reshape(1, HEAD_DIM).astype(F32))


def _s5_tables(a_re, a_im, log_dt, b_re, b_im, c_re, c_im, d_skip, t_chunk):
    hi = lax.Precision.HIGHEST
    g, p = a_re.shape
    c = b_re.shape[-1]
    gl = LANES // c
    nj = g // gl
    dt = jnp.exp(log_dt.astype(F32))[:, None]
    ar, ai = a_re.astype(F32), a_im.astype(F32)

    def apow(tau):
        tau = jnp.asarray(tau, F32)[:, None, None]
        mag = jnp.exp(tau * dt * ar)
        return mag * jnp.cos(tau * dt * ai), mag * jnp.sin(tau * dt * ai)

    pr, pi = apow(jnp.arange(t_chunk + 1))
    abar_re, abar_im = pr[1], pi[1]
    den = ar * ar + ai * ai
    nr, ni = abar_re - 1.0, abar_im
    coef_re = (nr * ar + ni * ai) / den
    coef_im = (ni * ar - nr * ai) / den
    br, bi = b_re.astype(F32), b_im.astype(F32)
    bbar_re = coef_re[..., None] * br - coef_im[..., None] * bi
    bbar_im = coef_re[..., None] * bi + coef_im[..., None] * br
    cr, ci = c_re.astype(F32), c_im.astype(F32)
    ca_re = cr[None] * pr[:, :, None, :] - ci[None] * pi[:, :, None, :]
    ca_im = cr[None] * pi[:, :, None, :] + ci[None] * pr[:, :, None, :]
    lane = jnp.arange(LANES)
    col = jnp.arange(gl * p)
    rep_c = (lane[None, :] % c == jnp.arange(c)[:, None]).astype(BF16)
    rep_p = (col[None, :] % p == jnp.arange(p)[:, None]).astype(BF16)
    lane_lane = lane[:, None] // c == lane[None, :] // c
    lane_col = lane[:, None] // c == col[None, :] // p
    col_lane = col[:, None] // p == lane[None, :] // c

    k_tap = (jnp.einsum("tgcp,gpd->tgcd", ca_re[:t_chunk], bbar_re, precision=hi)
             - jnp.einsum("tgcp,gpd->tgcd", ca_im[:t_chunk], bbar_im, precision=hi))
    lag = jnp.arange(t_chunk)[None, :] - jnp.arange(t_chunk)[:, None]
    k_st = jnp.where((lag >= 0)[:, :, None, None, None], k_tap[jnp.maximum(lag, 0)], 0.0)
    k_rows = jnp.transpose(k_st.reshape(t_chunk, t_chunk, nj, gl, c, c), (2, 0, 1, 3, 5, 4))
    k_rep = jnp.dot(k_rows.reshape(-1, c).astype(BF16), rep_c).reshape(nj, t_chunk, t_chunk, LANES, LANES)
    k_rep = jnp.where(lane_lane[None, None, None], k_rep, 0)
    k_mat = jnp.transpose(k_rep, (0, 1, 3, 2, 4)).reshape(nj, t_chunk * LANES, t_chunk * LANES)

    rev = t_chunk - 1 - jnp.arange(t_chunk)
    s_re = pr[rev][..., None] * bbar_re[None] - pi[rev][..., None] * bbar_im[None]
    s_im = pr[rev][..., None] * bbar_im[None] + pi[rev][..., None] * bbar_re[None]

    def inc_mat(x):
        rows = jnp.transpose(x.reshape(t_chunk, nj, gl, p, c), (1, 0, 2, 4, 3))
        rep = jnp.dot(rows.reshape(-1, p).astype(BF16), rep_p).reshape(nj, t_chunk, LANES, gl * p)
        return jnp.where(lane_col[None, None], rep, 0).reshape(nj, t_chunk * LANES, gl * p)

    w_in = jnp.concatenate([k_mat, inc_mat(s_re), inc_mat(s_im)], axis=-1)

    def out_mat(x):
        rows = jnp.transpose(x.reshape(t_chunk, nj, gl, c, p), (1, 2, 4, 0, 3))
        rep = jnp.dot(rows.reshape(-1, c).astype(BF16), rep_c).reshape(nj, gl * p, t_chunk, LANES)
        return jnp.where(col_lane[None, :, None, :], rep, 0).reshape(nj, gl * p, t_chunk * LANES)

    w_state = jnp.concatenate([out_mat(ca_re[1:]), out_mat(-ca_im[1:])], axis=1)

    qr, qi = apow(t_chunk * jnp.arange(1, 9))
    lanes = lambda x: jnp.transpose(x.reshape(8, nj, gl * p), (1, 0, 2))
    consts = jnp.concatenate([lanes(qr), lanes(qi)], axis=1)
    d_row = jnp.tile(d_skip.astype(F32).reshape(nj, 1, LANES), (1, 1, t_chunk))
    return w_in, w_state, consts, d_row


def _gelu_tanh(y):
    return 0.5 * y * (1.0 + jnp.tanh(math.sqrt(2.0 / math.pi) * (y + 0.044715 * (y * y * y))))


def _s5_kernel(x_ref, win_ref, wst_ref, cst_ref, d_ref, y_ref, stage, inc_re, inc_im, hp_re, hp_im,
               *, t_chunk, n_rows, tc, gp):
    stage[...] = x_ref[...].astype(F32)
    u = jnp.concatenate([stage[pl.ds(s, n_rows, stride=t_chunk), :] for s in range(t_chunk)], axis=1)
    z = _dot(u.astype(BF16), win_ref[0])
    inc_re[...] = z[:, tc:tc + gp]
    inc_im[...] = z[:, tc + gp:]
    cst = cst_ref[0]
    c_re, c_im = cst[0:8, :], cst[8:16, :]
    row = lax.broadcasted_iota(jnp.int32, (8, gp), 0)

    def body(i, carry):
        h_re, h_im = carry
        r0 = pl.multiple_of(i * 8, 8)
        x_re = inc_re[pl.ds(r0, 8), :]
        x_im = inc_im[pl.ds(r0, 8), :]
        for k in (1, 2, 4):
            keep = row >= k
            s_re = jnp.where(keep, pltpu.roll(x_re, k, 0), 0.0)
            s_im = jnp.where(keep, pltpu.roll(x_im, k, 0), 0.0)
            m_re, m_im = c_re[k - 1:k, :], c_im[k - 1:k, :]
            x_re, x_im = x_re + m_re * s_re - m_im * s_im, x_im + m_re * s_im + m_im * s_re
        o_re = x_re + c_re * h_re - c_im * h_im
        o_im = x_im + c_re * h_im + c_im * h_re
        first = row >= 1
        hp_re[pl.ds(r0, 8), :] = jnp.where(first, pltpu.roll(o_re, 1, 0), h_re)
        hp_im[pl.ds(r0, 8), :] = jnp.where(first, pltpu.roll(o_im, 1, 0), h_im)
        return o_re[7:8, :], o_im[7:8, :]

    zero = jnp.zeros((1, gp), F32)
    lax.fori_loop(0, n_rows // 8, body, (zero, zero))
    h_prev = jnp.concatenate([hp_re[...], hp_im[...]], axis=1).astype(BF16)
    y = _gelu_tanh(z[:, :tc] + _dot(h_prev, wst_ref[0]) + d_ref[0] * u)
    for t in range(t_chunk):
        stage[pl.ds(t, n_rows, stride=t_chunk), :] = y[:, t * LANES:(t + 1) * LANES]
    y_ref[...] = stage[...].astype(y_ref.dtype)


def _s5(proj, ssm_off, ssm_w, bsz, tables, *, t_chunk=S5_CHUNK):
    w_in, w_state, consts, d_row = tables
    m = proj.shape[0]
    seq = m // bsz
    nj = ssm_w // LANES
    n_rows = seq // t_chunk
    tc = t_chunk * LANES
    gp = w_state.shape[1] // 2
    jb0 = ssm_off // LANES
    return pl.pallas_call(
        functools.partial(_s5_kernel, t_chunk=t_chunk, n_rows=n_rows, tc=tc, gp=gp),
        grid=(nj, bsz),
        in_specs=[
            pl.BlockSpec((seq, LANES), lambda j, b: (b, jb0 + j)),
            pl.BlockSpec((1, tc, tc + 2 * gp), lambda j, b: (j, 0, 0)),
            pl.BlockSpec((1, 2 * gp, tc), lambda j, b: (j, 0, 0)),
            pl.BlockSpec((1, 16, gp), lambda j, b: (j, 0, 0)),
            pl.BlockSpec((1, 1, tc), lambda j, b: (j, 0, 0)),
        ],
        out_specs=pl.BlockSpec((seq, LANES), lambda j, b: (b, j)),
        out_shape=jax.ShapeDtypeStruct((m, ssm_w), BF16),
        scratch_shapes=[pltpu.VMEM((seq, LANES), F32)] + [pltpu.VMEM((n_rows, gp), F32) for _ in range(4)],
        compiler_params=_params("parallel", "parallel"),
        name="s5",
    )(proj, w_in, w_state, consts, d_row)


def _glu_kernel(x_ref, w_ref, b_ref, o_ref, *, n):
    z = _dot(x_ref[...], w_ref[...]) + b_ref[...]
    o_ref[...] = (z[:, :n] * _sigmoid(z[:, n:])).astype(o_ref.dtype)


def _glu(x, w, layer, b, *, tm=1024):
    m, k = x.shape
    n2 = w.shape[2]
    n = n2 // 2
    return pl.pallas_call(
        functools.partial(_glu_kernel, n=n),
        grid=(m // tm,),
        in_specs=[
            pl.BlockSpec((tm, k), lambda i: (i, 0)),
            _layer_spec(layer, (k, n2), lambda i: (0, 0)),
            pl.BlockSpec((1, n2), lambda i: (0, 0)),
        ],
        out_specs=pl.BlockSpec((tm, n), lambda i: (i, 0)),
        out_shape=jax.ShapeDtypeStruct((m, n), BF16),
        compiler_params=_params("parallel"),
        name="glu",
    )(x, w, b.reshape(1, n2).astype(F32))


def _route_kernel(h_ref, g_ref, w_ref, b_ref, xp_ref, info_ref, *, n_groups, per_group):
    x = h_ref[...]
    xn = x * lax.rsqrt(jnp.mean(x * x, axis=-1, keepdims=True) + EPS) * g_ref[...]
    xp_ref[...] = _pack_halves(xn)
    n_exp = n_groups * per_group
    logits = _dot(xn.astype(BF16), w_ref[...]) + b_ref[...]
    lane = lax.broadcasted_iota(jnp.int32, logits.shape, 1).astype(F32)
    big = float(LANES)
    is_group = jnp.logical_and(lane >= n_exp, lane < n_exp + n_groups)
    gl = jnp.where(is_group, logits, -jnp.inf)
    gmax = jnp.max(gl, axis=-1, keepdims=True)
    gsel = jnp.min(jnp.where(gl == gmax, lane, big), axis=-1, keepdims=True) - n_exp
    gprob = 1.0 / jnp.sum(jnp.where(is_group, jnp.exp(logits - gmax), 0.0), axis=-1, keepdims=True)
    in_group = jnp.logical_and(lane >= gsel * per_group, lane < (gsel + 1.0) * per_group)
    el = jnp.where(in_group, logits, -jnp.inf)
    v1 = jnp.max(el, axis=-1, keepdims=True)
    i1 = jnp.min(jnp.where(el == v1, lane, big), axis=-1, keepdims=True)
    el2 = jnp.where(lane == i1, -jnp.inf, el)
    v2 = jnp.max(el2, axis=-1, keepdims=True)
    i2 = jnp.min(jnp.where(el2 == v2, lane, big), axis=-1, keepdims=True)
    e2 = jnp.exp(v2 - v1)
    w1 = gprob / (1.0 + e2)
    w2 = gprob * e2 / (1.0 + e2)
    info_ref[...] = (jnp.where(lane == 0.0, i1, 0.0) + jnp.where(lane == 1.0, i2, 0.0)
                     + jnp.where(lane == 2.0, w1, 0.0) + jnp.where(lane == 3.0, w2, 0.0))


def _route(h, gain, w_gr, b_gr, w_er, b_er, *, tm=512):
    m, d = h.shape
    n_groups, _, per_group = w_er.shape
    n_exp = n_groups * per_group
    assert n_exp + n_groups <= LANES
    w = jnp.concatenate([jnp.transpose(w_er, (1, 0, 2)).reshape(d, n_exp), w_gr], axis=1)
    w = jnp.pad(w, ((0, 0), (0, LANES - n_exp - n_groups))).astype(BF16)
    b = jnp.concatenate([b_er.reshape(n_exp), b_gr]).astype(F32)
    b = jnp.pad(b, (0, LANES - n_exp - n_groups)).reshape(1, LANES)
    return pl.pallas_call(
        functools.partial(_route_kernel, n_groups=n_groups, per_group=per_group),
        grid=(m // tm,),
        in_specs=[pl.BlockSpec((tm, d), lambda i: (i, 0)), pl.BlockSpec((1, d), lambda i: (0, 0)),
                  pl.BlockSpec((d, LANES), lambda i: (0, 0)), pl.BlockSpec((1, LANES), lambda i: (0, 0))],
        out_specs=[pl.BlockSpec((tm, d // 2), lambda i: (i, 0)), pl.BlockSpec((tm, LANES), lambda i: (i, 0))],
        out_shape=[jax.ShapeDtypeStruct((m, d // 2), U32), jax.ShapeDtypeStruct((m, LANES), F32)],
        compiler_params=_params("parallel"),
        name="moe_route",
    )(h, gain.reshape(1, d).astype(F32), w, b)


def _sorted_layout(expert_ids, n_exp, row_tile, n_tiles):
    e = expert_ids.reshape(-1)
    blk = LANES
    nb = e.shape[0] // blk
    onehot = (e[:, None] == jnp.arange(n_exp, dtype=jnp.int32)[None, :]).reshape(nb, blk, n_exp)
    tri = jnp.tril(jnp.ones((blk, blk), BF16))
    within = jnp.einsum("ij,bjk->bik", tri, onehot.astype(BF16), preferred_element_type=F32)
    totals = within[:, -1, :]
    before = jnp.dot(jnp.tril(jnp.ones((nb, nb), F32), -1), totals, precision=lax.Precision.HIGHEST)
    counts = (before[-1] + totals[-1]).astype(jnp.int32)
    rank = jnp.sum(jnp.where(onehot, within + before[:, None, :], 0.0), axis=-1).reshape(-1).astype(jnp.int32) - 1
    padded = ((counts + row_tile - 1) // row_tile) * row_tile
    ends = jnp.cumsum(padded)
    starts = ends - padded
    pos = starts[e] + rank
    tile_start = jnp.arange(n_tiles, dtype=jnp.int32) * row_tile
    tile_expert = jnp.minimum(jnp.sum((tile_start[:, None] >= ends[None, :]).astype(jnp.int32), axis=1), n_exp - 1)
    n_used = (ends[-1] // row_tile).astype(jnp.int32).reshape(1)
    return pos.astype(jnp.int32), tile_expert.astype(jnp.int32), n_used


def _row_copy(src_ref, src_row, dst_ref, dst_row, sem):
    return pltpu.make_async_copy(src_ref.at[pl.ds(src_row, 1)], dst_ref.at[pl.ds(dst_row, 1)], sem)


def _start_row_copies(n_rows, start_row):
    def body(t, c):
        start_row(t)
        return c

    lax.fori_loop(0, n_rows, body, 0, unroll=ROW_DMA_UNROLL)


def _wait_row_copies(n_copies, example_copy):
    for _ in range(n_copies):
        example_copy.wait()


def _dispatch_kernel(pos_ref, x_ref, init_ref, out_ref, sem, *, tt, n_slots):
    del init_ref

    def start_row(t):
        for k in range(n_slots):
            _row_copy(x_ref, t, out_ref, pos_ref[0, 0, n_slots * t + k], sem).start(priority=k % 2)

    _start_row_copies(tt, start_row)
    _wait_row_copies(tt * n_slots, _row_copy(x_ref, 0, out_ref, 0, sem))


def _dispatch(xp, pos, n_rows, *, tt=MOE_TOKEN_TILE, n_slots=TOP_K):
    m, d2 = xp.shape
    pos3 = pos.reshape(m // tt, 1, tt * n_slots)
    return pl.pallas_call(
        functools.partial(_dispatch_kernel, tt=tt, n_slots=n_slots),
        grid=(m // tt,),
        in_specs=[pl.BlockSpec((1, 1, tt * n_slots), lambda i: (i, 0, 0), memory_space=pltpu.SMEM),
                  pl.BlockSpec((tt, d2), lambda i: (i, 0)),
                  pl.BlockSpec(memory_space=pl.ANY)],
        out_specs=pl.BlockSpec(memory_space=pl.ANY),
        out_shape=jax.ShapeDtypeStruct((n_rows, d2), U32),
        scratch_shapes=[pltpu.SemaphoreType.DMA],
        input_output_aliases={2: 0},
        compiler_params=_params("arbitrary"),
        name="moe_dispatch",
    )(pos3, xp, jnp.zeros((n_rows, d2), U32))


def _expert_kernel(te_ref, first_ref, next_ref, nu_ref, x_ref, wg_hbm, wu_hbm, wd_hbm, o_ref,
                   stage_g, stage_u, stage_d, wg_s, wu_s, wd_s, sem, *, layer):
    t = pl.program_id(0)
    in_use = t < nu_ref[0]
    copies = ((wg_hbm, stage_g), (wu_hbm, stage_u), (wd_hbm, stage_d))

    def weight_copy(k, expert):
        src, dst = copies[k]
        return pltpu.make_async_copy(src.at[layer, expert], dst, sem.at[k])

    @pl.when(t == 0)
    def _():
        for k in range(len(copies)):
            weight_copy(k, te_ref[0]).start()

    @pl.when(jnp.logical_and(in_use, first_ref[t] == 1))
    def _():
        for k, dst in enumerate((wg_s, wu_s, wd_s)):
            weight_copy(k, 0).wait()
            dst[...] = copies[k][1][...].astype(BF16)

        @pl.when(next_ref[t] >= 0)
        def _():
            for k in range(len(copies)):
                weight_copy(k, next_ref[t]).start()

    @pl.when(in_use)
    def _():
        lo, hi = _unpack_halves(x_ref[...])
        x = jnp.concatenate([lo, hi], axis=1).astype(BF16)
        g = _dot(x, wg_s[...])
        u = _dot(x, wu_s[...])
        hid = (g * _sigmoid(g) * u).astype(BF16)
        o_ref[...] = _pack_halves(_dot(hid, wd_s[...]))

    @pl.when(jnp.logical_not(in_use))
    def _():
        o_ref[...] = jnp.zeros_like(o_ref)


def _expert_ffn(xs, tile_expert, n_used, w_gate, w_up, w_down, layer, *, tm=MOE_ROW_TILE):
    rows, d2 = xs.shape
    _, _, d, ff = w_gate.shape
    n_tiles = rows // tm
    idx = jnp.arange(n_tiles, dtype=jnp.int32)
    used = idx < n_used[0]
    first = jnp.logical_and(used, jnp.logical_or(idx == 0, tile_expert != jnp.roll(tile_expert, 1)))
    first_pos = jnp.where(first, idx, n_tiles)
    later = jnp.concatenate([lax.cummin(first_pos[::-1])[::-1][1:], jnp.full((1,), n_tiles, jnp.int32)])
    next_expert = jnp.where(later < n_tiles, tile_expert[jnp.minimum(later, n_tiles - 1)], -1).astype(jnp.int32)
    any_spec = pl.BlockSpec(memory_space=pl.ANY)
    grid_spec = pltpu.PrefetchScalarGridSpec(
        num_scalar_prefetch=4,
        grid=(n_tiles,),
        in_specs=[pl.BlockSpec((tm, d2), lambda t, *_: (t, 0)), any_spec, any_spec, any_spec],
        out_specs=pl.BlockSpec((tm, d2), lambda t, *_: (t, 0)),
        scratch_shapes=[pltpu.VMEM((d, ff), F32), pltpu.VMEM((d, ff), F32), pltpu.VMEM((ff, d), F32),
                        pltpu.VMEM((d, ff), BF16), pltpu.VMEM((d, ff), BF16), pltpu.VMEM((ff, d), BF16),
                        pltpu.SemaphoreType.DMA((3,))],
    )
    return pl.pallas_call(
        functools.partial(_expert_kernel, layer=layer),
        grid_spec=grid_spec,
        out_shape=jax.ShapeDtypeStruct((rows, d2), U32),
        compiler_params=_params("arbitrary"),
        name="moe_expert_ffn",
    )(tile_expert, first.astype(jnp.int32), next_expert, n_used, xs, w_gate, w_up, w_down)


def _combine_kernel(pos_ref, pos_next_ref, info_ref, h_ref, g_ref, y_ref, o_ref, on_ref, buf, sem, *, tt, n_slots):
    i = pl.program_id(0)
    cur = i % 2

    def gather(p_ref, half):
        def start_row(t):
            for k in range(n_slots):
                _row_copy(y_ref, p_ref[0, 0, n_slots * t + k], buf.at[half, k], t, sem.at[half]).start(priority=k % 2)

        _start_row_copies(tt, start_row)

    @pl.when(i == 0)
    def _():
        gather(pos_ref, 0)

    @pl.when(i + 1 < pl.num_programs(0))
    def _():
        gather(pos_next_ref, 1 - cur)

    _wait_row_copies(tt * n_slots, _row_copy(y_ref, 0, buf.at[cur, 0], 0, sem.at[cur]))
    info = info_ref[...]
    lane = lax.broadcasted_iota(jnp.int32, info.shape, 1)
    lo_acc = hi_acc = None
    for k in range(n_slots):
        wk = jnp.sum(jnp.where(lane == n_slots + k, info, 0.0), axis=-1, keepdims=True)
        lo, hi = _unpack_halves(buf[cur, k])
        lo_acc = wk * lo if lo_acc is None else lo_acc + wk * lo
        hi_acc = wk * hi if hi_acc is None else hi_acc + wk * hi
    out = h_ref[...] + jnp.concatenate([lo_acc, hi_acc], axis=1)
    o_ref[...] = out
    ms = jnp.mean(out * out, axis=-1, keepdims=True)
    on_ref[...] = (out * lax.rsqrt(ms + EPS) * g_ref[...]).astype(on_ref.dtype)


def _combine(h, info, ys, pos, next_gain, *, tt=MOE_TOKEN_TILE, n_slots=TOP_K):
    m, d = h.shape
    d2 = ys.shape[1]
    n_steps = m // tt
    pos3 = pos.reshape(n_steps, 1, tt * n_slots)
    return pl.pallas_call(
        functools.partial(_combine_kernel, tt=tt, n_slots=n_slots),
        grid=(n_steps,),
        in_specs=[pl.BlockSpec((1, 1, tt * n_slots), lambda i: (i, 0, 0), memory_space=pltpu.SMEM),
                  pl.BlockSpec((1, 1, tt * n_slots), lambda i: (jnp.minimum(i + 1, n_steps - 1), 0, 0),
                               memory_space=pltpu.SMEM),
                  pl.BlockSpec((tt, LANES), lambda i: (i, 0)),
                  pl.BlockSpec((tt, d), lambda i: (i, 0)),
                  pl.BlockSpec((1, d), lambda i: (0, 0)),
                  pl.BlockSpec(memory_space=pl.ANY)],
        out_specs=[pl.BlockSpec((tt, d), lambda i: (i, 0)), pl.BlockSpec((tt, d), lambda i: (i, 0))],
        out_shape=[jax.ShapeDtypeStruct((m, d), F32), jax.ShapeDtypeStruct((m, d), BF16)],
        scratch_shapes=[pltpu.VMEM((2, n_slots, tt, d2), U32), pltpu.SemaphoreType.DMA((2,))],
        input_output_aliases={3: 0},
        compiler_params=_params("arbitrary"),
        name="moe_combine",
    )(pos3, pos3, info, h, next_gain.reshape(1, d).astype(F32), ys)


def _moe(h, gain, w_gr, b_gr, w_er, b_er, w_gate, w_up, w_down, layer, next_gain):
    m = h.shape[0]
    n_exp = w_gate.shape[1]
    n_tiles = -(-(m * TOP_K + n_exp * (MOE_ROW_TILE - 1)) // MOE_ROW_TILE)
    xp, info = _route(h, gain, w_gr, b_gr, w_er, b_er)
    expert_ids = info[:, :TOP_K].astype(jnp.int32)
    pos, tile_expert, n_used = _sorted_layout(expert_ids, n_exp, MOE_ROW_TILE, n_tiles)
    xs = _dispatch(xp, pos, n_tiles * MOE_ROW_TILE)
    ys = _expert_ffn(xs, tile_expert, n_used, w_gate, w_up, w_down, layer)
    return _combine(h, info, ys, pos, next_gain)


def kernel(x, p, mix_gain, w_in, conv_w, conv_b, w_rgate, b_rgate, w_igate, b_igate, lru_lambda, q_gain, k_gain, rel_bias, ssm_a_re, ssm_a_im, ssm_log_dt, ssm_b_re, ssm_b_im, ssm_c_re, ssm_c_im, ssm_d, w_glu, b_glu, w_proj_lru, w_proj_att, w_proj_ssm, w_out, ffn_gain, w_group_router, b_group_router, w_expert_router, b_expert_router, w_up, w_gate, w_down, ple_gain, w_ple, w_ple_gate):
    bsz, seq, d = x.shape
    depth = w_in.shape[0]
    m = bsz * seq
    lru_w = w_proj_lru.shape[1]
    att_w = w_proj_att.shape[1]
    ssm_w = w_proj_ssm.shape[1]
    n_heads = att_w // HEAD_DIM
    q_off, k_off, v_off = lru_w, lru_w + att_w, lru_w + 2 * att_w
    ssm_off = lru_w + 3 * att_w
    gate_off = ssm_off + ssm_w
    in_width = w_in.shape[2]

    bf = lambda w: w.astype(BF16)
    w_glu, w_proj_lru, w_proj_att, w_proj_ssm, w_out = map(bf, (w_glu, w_proj_lru, w_proj_att, w_proj_ssm, w_out))
    w_ple_gate, w_ple = bf(w_ple_gate), bf(w_ple)
    p = bf(p.reshape(depth, m, p.shape[-1]))

    h = x.reshape(m, d).astype(F32)
    for i in range(depth):
        xn = _rmsnorm(h, mix_gain[i])
        proj = _matmul(xn, w_in, i)
        proj3 = proj.reshape(bsz, seq, in_width)
        y_lru = _rglru(proj3, conv_w[i], conv_b[i], w_rgate[i], b_rgate[i], w_igate[i], b_igate[i], lru_lambda[i])
        y_att = _attention(proj3, q_off, k_off, v_off, n_heads, q_gain[i], k_gain[i], rel_bias[i])
        tables = _s5_tables(ssm_a_re[i], ssm_a_im[i], ssm_log_dt[i], ssm_b_re[i], ssm_b_im[i],
                            ssm_c_re[i], ssm_c_im[i], ssm_d[i], S5_CHUNK)
        y_ssm = _glu(_s5(proj, ssm_off, ssm_w, bsz, tables), w_glu, i, b_glu[i])
        merged = _gated_merge(y_lru.reshape(m, lru_w), y_att.reshape(m, att_w), y_ssm,
                              w_proj_lru, w_proj_att, w_proj_ssm, i, proj, gate_off)
        h = _matmul_residual(h, merged, w_out, i, in_place=i > 0)
        h, hn = _moe(h, ffn_gain[i], w_group_router[i], b_group_router[i], w_expert_router[i], b_expert_router[i],
                     w_gate, w_up, w_down, i, ple_gain[i])
        h = _ple(h, hn, w_ple_gate, p, w_ple, i)
    return h.reshape(bsz, seq, d)
```
